```python
import math
import jax
import jax.numpy as jnp
from jax import lax
import numpy as np

D_MODEL = 2048
BATCH = 1
SEQ = 8192
DEPTH = 2
DEC_BATCH = 128
DEC_SEQ = 4
PAST_LEN = 2048
PAGE_SIZE = 128

N_A = DEPTH // 2
N_B = DEPTH - N_A
EPS = 1e-6
PLE_DIM = 256
HA_DK = 128
HA_HEADS = D_MODEL // HA_DK
HA_DV = D_MODEL // HA_HEADS
HA_CHUNK = 32
NB_DH = 128
NB_HEADS = D_MODEL // NB_DH
NB_KV = 4
CMP_LEN = 32
CMP_STRIDE = 16
SLC_LEN = 64
N_SELECT = 16
WINDOW = 512
Q_BLOCK = 128
ROPE_THETA = 10000.0
D_FF = ((8 * D_MODEL // 3 + 127) // 128) * 128
CONV_W = 3
NEG = -1e30
BIG = 1e30

kernel_name = 'hgrn2_nsa_yoco_decode_step'


def _rmsnorm(x, g):
    xf = x.astype(jnp.float32)
    y = xf * lax.rsqrt(jnp.mean(xf * xf, axis=-1, keepdims=True) + EPS)
    return (y * g.astype(jnp.float32)).astype(x.dtype)


def _rope(x, pos):
    half = x.shape[-1] // 2
    inv = ROPE_THETA ** (-jnp.arange(half, dtype=jnp.float32) / half)
    ang = pos.astype(jnp.float32)[:, None] * inv[None, :]
    cos, sin = jnp.cos(ang)[:, None, :], jnp.sin(ang)[:, None, :]
    xf = x.astype(jnp.float32)
    x1, x2 = xf[..., :half], xf[..., half:]
    return jnp.concatenate([x1 * cos - x2 * sin, x2 * cos + x1 * sin], axis=-1).astype(x.dtype)


def _masked_softmax(s, mask):
    s = jnp.where(mask, s, NEG)
    m = jnp.max(s, axis=-1, keepdims=True)
    e = jnp.where(mask, jnp.exp(s - m), 0.0)
    return e / jnp.maximum(jnp.sum(e, axis=-1, keepdims=True), 1e-30)


def _gqa_attend(q, k, v, mask):
    bq, t, h, d = q.shape
    g = k.shape[2]
    qg = q.reshape(bq, t, g, h // g, d).astype(jnp.float32)
    s = jnp.einsum('btgjd,bkgd->btgjk', qg, k.astype(jnp.float32)) * (d ** -0.5)
    p = _masked_softmax(s, mask[:, :, None, None, :])
    o = jnp.einsum('btgjk,bkgd->btgjd', p, v.astype(jnp.float32))
    return o.reshape(bq, t, h, d).astype(q.dtype), p


def _gla_chunked(q, k, v, logf, s0):
    b, t, h, dk = q.shape
    dv = v.shape[-1]
    c = math.gcd(t, HA_CHUNK)
    n = t // c

    def to_chunks(a):
        return jnp.transpose(a.reshape(b, n, c, h, a.shape[-1]), (1, 0, 3, 2, 4))

    qc, kc, vc, gc = to_chunks(q), to_chunks(k), to_chunks(v), to_chunks(logf)
    bcum = jnp.cumsum(gc, axis=3)
    blast = bcum[:, :, :, -1, :]
    q_in = qc * jnp.exp(bcum)
    k_in = kc * jnp.exp(-bcum)
    k_dec = kc * jnp.exp(blast[:, :, :, None, :] - bcum)
    causal = jnp.tril(jnp.ones((c, c), dtype=bool))
    att = jnp.where(causal, jnp.einsum('nbhik,nbhjk->nbhij', q_in, k_in), 0.0)
    o_intra = jnp.einsum('nbhij,nbhjv->nbhiv', att, vc)

    def step(s, inp):
        qi, kd, vi, dl = inp
        o_inter = jnp.einsum('bhik,bhkv->bhiv', qi, s)
        s = jnp.exp(dl)[..., None] * s + jnp.einsum('bhik,bhiv->bhkv', kd, vi)
        return s, o_inter

    s_final, o_inter = lax.scan(step, s0.astype(jnp.float32), (q_in, k_dec, vc, blast))
    o = o_intra + o_inter
    return jnp.transpose(o, (1, 0, 3, 2, 4)).reshape(b, t, h, dv), s_final


def _hgrn2_layer(xn, s0, lb, w_in, out_gain, w_out):
    b, t, _ = xn.shape
    hk = HA_HEADS * HA_DK
    hv = HA_HEADS * HA_DV
    proj = (xn @ w_in).astype(jnp.float32)
    q = jax.nn.silu(proj[..., :hk])
    f = lb + (1.0 - lb) * jax.nn.sigmoid(proj[..., hk:2 * hk])
    i_in = proj[..., 2 * hk:2 * hk + hv]
    gate = proj[..., 2 * hk + hv:]
    shp_k = (b, t, HA_HEADS, HA_DK)
    shp_v = (b, t, HA_HEADS, HA_DV)
    o, s_new = _gla_chunked(q.reshape(shp_k), (1.0 - f).reshape(shp_k), i_in.reshape(shp_v),
                            jnp.log(f).reshape(shp_k), s0)
    o = _rmsnorm(o, out_gain.reshape(HA_HEADS, HA_DV)) * jax.nn.silu(gate).reshape(shp_v)
    return o.reshape(b, t, hv).astype(xn.dtype) @ w_out, s_new


def _conv_ffn(xn, prev, w_up, conv_w, conv_b, w_down):
    t = xn.shape[1]
    up = xn @ w_up
    u, gt = up[..., :D_FF], up[..., D_FF:]
    ext = jnp.concatenate([prev.astype(gt.dtype), gt], axis=1)
    conv = conv_b
    for j in range(CONV_W):
        conv = conv + conv_w[j] * ext[:, j:j + t]
    return (jax.nn.silu(conv) * u) @ w_down, ext[:, t:]


def _ple(h, p, gain, w_p, w_g):
    gate = jax.nn.sigmoid(_rmsnorm(h, gain) @ w_g)
    return h + (p.astype(h.dtype) @ w_p) * gate


def _gather_pages(cache, page_table):
    rows = cache[page_table]
    return rows.reshape(page_table.shape[0], page_table.shape[1] * cache.shape[1], cache.shape[2], cache.shape[3])


def _compress(rows, pe, w1, w2):
    b, l, g, d = rows.shape
    n_sub = l // CMP_STRIDE
    r = CMP_LEN // CMP_STRIDE
    n_cmp = n_sub - r + 1
    sub = rows[:, :n_sub * CMP_STRIDE].reshape(b, n_sub, CMP_STRIDE, g, d)
    pre = None
    for j in range(r):
        sl = slice(j * CMP_STRIDE, (j + 1) * CMP_STRIDE)
        part = jnp.einsum('bnsgd,sde->bnge', sub + pe[sl][None, None, :, None, :], w1[sl])[:, j:j + n_cmp]
        pre = part if pre is None else pre + part
    return jnp.einsum('bnge,ef->bngf', jax.nn.gelu(pre), w2)


def _cmp_slc_attend(q, qr, qpos, kcb, vcb, kpad, vpad, q_blk):
    b, t, h, d = q.shape
    g = kcb.shape[2]
    n_cmp = kcb.shape[1]
    n_slc = kpad.shape[1] // SLC_LEN
    n_sel = min(N_SELECT, n_slc)
    cstart = jnp.arange(n_cmp) * CMP_STRIDE
    kend = cstart + (CMP_LEN - 1)
    sstart = jnp.arange(n_slc) * SLC_LEN
    overlap = ((cstart[:, None] <= sstart[None, :] + (SLC_LEN - 1)) & (kend[:, None] >= sstart[None, :])).astype(jnp.float32)
    sidx = jnp.arange(n_slc)
    offs = jnp.arange(SLC_LEN)
    bi = jnp.arange(b)[:, None, None, None]
    gi = jnp.arange(g)[None, None, :, None]

    def block(args):
        qc_, qr_, pb = args
        mask_c = (kend[None, :] <= pb[:, None])[None]
        o_c, p_c = _gqa_attend(qc_, kcb, vcb, mask_c)
        imp = jnp.einsum('bqgjn,ns->bqgs', p_c, overlap)
        cur = pb // SLC_LEN
        forced = (sidx[None, :] == 0) | (sidx[None, :] == cur[:, None]) | (sidx[None, :] == cur[:, None] - 1)
        future = sstart[None, :] > pb[:, None]
        score = jnp.where(future[None, :, None, :], NEG, jnp.where(forced[None, :, None, :], BIG, imp))
        _, sel = lax.top_k(score, n_sel)
        tok = (sel[..., None] * SLC_LEN + offs).reshape(b, q_blk, g, n_sel * SLC_LEN)
        ks = kpad[bi, tok, gi].astype(jnp.float32)
        vs = vpad[bi, tok, gi].astype(jnp.float32)
        qg = qr_.reshape(b, q_blk, g, h // g, d).astype(jnp.float32)
        s = jnp.einsum('bqgjd,bqgnd->bqgjn', qg, ks) * (d ** -0.5)
        p_s = _masked_softmax(s, (tok <= pb[None, :, None, None])[:, :, :, None, :])
        o_s = jnp.einsum('bqgjn,bqgnd->bqgjd', p_s, vs).reshape(b, q_blk, h, d)
        return o_c, o_s.astype(q.dtype)

    nq = t // q_blk
    qs = jnp.moveaxis(q.reshape(b, nq, q_blk, h, d), 1, 0)
    qrs = jnp.moveaxis(qr.reshape(b, nq, q_blk, h, d), 1, 0)
    ps = qpos.reshape(nq, q_blk)
    o_c, o_s = lax.map(block, (qs, qrs, ps))
    return (jnp.moveaxis(o_c, 0, 1).reshape(b, t, h, d), jnp.moveaxis(o_s, 0, 1).reshape(b, t, h, d))


def _window_banded(q, k, v):
    b, t, h, d = q.shape
    g = k.shape[2]
    nb = t // Q_BLOCK
    span = WINDOW + Q_BLOCK
    padw = ((0, 0), (WINDOW, 0), (0, 0), (0, 0))
    kp, vp = jnp.pad(k, padw), jnp.pad(v, padw)
    idx = jnp.arange(nb)[:, None] * Q_BLOCK + jnp.arange(span)[None, :]
    kb = kp[:, idx].reshape(b * nb, span, g, d)
    vb = vp[:, idx].reshape(b * nb, span, g, d)
    kpos = idx - WINDOW
    qpos = jnp.arange(t).reshape(nb, Q_BLOCK)
    diff = qpos[:, :, None] - kpos[:, None, :]
    mask = jnp.tile((diff >= 0) & (diff <= WINDOW), (b, 1, 1))
    o, _ = _gqa_attend(q.reshape(b * nb, Q_BLOCK, h, d), kb, vb, mask)
    return o.reshape(b, t, h, d)


def _nsa_layer(xn, pos, shared, w_in, q_gain, w_out, banded, q_blk):
    kcb, vcb, kpad, vpad, win_k, win_v, win_pos = shared
    b, t, _ = xn.shape
    hd = NB_HEADS * NB_DH
    proj = xn @ w_in
    q = _rmsnorm(proj[..., :hd].reshape(b, t, NB_HEADS, NB_DH), q_gain)
    gates = jax.nn.sigmoid(proj[..., hd:].astype(jnp.float32)).reshape(b, t, 3, NB_HEADS)
    qr = _rope(q, pos)
    o_c, o_s = _cmp_slc_attend(q, qr, pos, kcb, vcb, kpad, vpad, q_blk)
    if banded:
        o_w = _window_banded(qr, win_k, win_v)
    else:
        diff = pos[:, None] - win_pos[None, :]
        o_w, _ = _gqa_attend(qr, win_k, win_v, ((diff >= 0) & (diff <= WINDOW))[None])
    o = (gates[:, :, 0, :, None] * o_c + gates[:, :, 1, :, None] * o_s + gates[:, :, 2, :, None] * o_w)
    return o.astype(xn.dtype).reshape(b, t, hd) @ w_out


def _shared_kv(h, pos, start, past, wt):
    b, t, _ = h.shape
    kv = (_rmsnorm(h, wt['kv_norm']) @ wt['kv_w']).reshape(b, t, 6, NB_KV, NB_DH)
    k_cmp, v_cmp = kv[:, :, 0], kv[:, :, 1]
    k_slc = _rope(_rmsnorm(kv[:, :, 2], wt['k_norm_slc']), pos)
    v_slc = kv[:, :, 3]
    k_win = _rope(_rmsnorm(kv[:, :, 4], wt['k_norm_win']), pos)
    v_win = kv[:, :, 5]
    if past is None:
        kc_full, vc_full, ks_full, vs_full = k_cmp, v_cmp, k_slc, v_slc
        win_k, win_v, win_pos = k_win, v_win, None
        keep = min(WINDOW, t)
    else:
        page_table, ckc, cvc, cks, cvs, cwk, cwv = past
        kc_full = jnp.concatenate([_gather_pages(ckc, page_table).astype(k_cmp.dtype), k_cmp], axis=1)
        vc_full = jnp.concatenate([_gather_pages(cvc, page_table).astype(v_cmp.dtype), v_cmp], axis=1)
        ks_full = jnp.concatenate([_gather_pages(cks, page_table).astype(k_slc.dtype), k_slc], axis=1)
        vs_full = jnp.concatenate([_gather_pages(cvs, page_table).astype(v_slc.dtype), v_slc], axis=1)
        keep = cwk.shape[1]
        win_k = jnp.concatenate([cwk.astype(k_win.dtype), k_win], axis=1)
        win_v = jnp.concatenate([cwv.astype(v_win.dtype), v_win], axis=1)
        win_pos = start - keep + jnp.arange(keep + t)
    kcb = _rmsnorm(_compress(kc_full, wt['cmp_pe_k'], wt['cmp_w1_k'], wt['cmp_w2_k']), wt['k_norm_cmp'])
    vcb = _compress(vc_full, wt['cmp_pe_v'], wt['cmp_w1_v'], wt['cmp_w2_v'])
    l = ks_full.shape[1]
    n_slc = -(-l // SLC_LEN)
    padw = ((0, 0), (0, n_slc * SLC_LEN - l), (0, 0), (0, 0))
    kpad, vpad = jnp.pad(ks_full, padw), jnp.pad(vs_full, padw)
    shared = (kcb, vcb, kpad, vpad, win_k, win_v, win_pos)
    rows = (k_cmp, v_cmp, k_slc, v_slc, win_k[:, -keep:], win_v[:, -keep:])
    return shared, rows


def _trunk(x, p, start, s_hgrn, s_conv, past, wt):
    b, t, _ = x.shape
    pos = start + jnp.arange(t)
    banded = past is None
    q_blk = math.gcd(t, Q_BLOCK) if banded else 1
    lb = jnp.cumsum(jax.nn.softmax(wt['a_lb_logits'].astype(jnp.float32), axis=0), axis=0)
    h = x
    hgrn_out, conv_out = [], []
    shared, rows = None, None
    for i in range(DEPTH):
        hn = _rmsnorm(h, wt['norm_mix'][i])
        if i < N_A:
            y, s = _hgrn2_layer(hn, s_hgrn[i], lb[i], wt['a_w_in'][i], wt['a_out_norm'][i], wt['a_w_out'][i])
            hgrn_out.append(s)
        else:
            j = i - N_A
            y = _nsa_layer(hn, pos, shared, wt['b_w_in'][j], wt['b_q_norm'][j], wt['b_w_out'][j], banded, q_blk)
        h = h + y
        f, cs = _conv_ffn(_rmsnorm(h, wt['norm_ffn'][i]), s_conv[i], wt['ffn_w_up'][i], wt['ffn_conv_w'][i],
                          wt['ffn_conv_b'][i], wt['ffn_w_down'][i])
        conv_out.append(cs)
        h = h + f
        h = _ple(h, p[i], wt['norm_ple'][i], wt['ple_w_p'][i], wt['ple_w_g'][i])
        if i == N_A - 1:
            shared, rows = _shared_kv(h, pos, start, past, wt)
    return h, jnp.stack(hgrn_out), jnp.stack(conv_out), rows


def setup_inputs(seed: int = 0) -> dict:
    key = jax.random.key(seed)
    kit = iter([jax.random.fold_in(key, i) for i in range(64)])
    f32 = jnp.float32

    def nrm(shape, scale=1.0):
        return jax.random.normal(next(kit), shape, f32) * scale

    def gain(shape):
        return 1.0 + nrm(shape, 0.02)

    n_pages = PAST_LEN // PAGE_SIZE
    n_used = DEC_BATCH * n_pages
    n_phys = n_used + max(1, n_used // 4)
    win_keep = min(WINDOW, PAST_LEN)
    hk = HA_HEADS * HA_DK
    hv = HA_HEADS * HA_DV
    hd = NB_HEADS * NB_DH
    page_shape = (n_phys, PAGE_SIZE, NB_KV, NB_DH)
    page_table = jax.random.permutation(next(kit), n_phys)[:n_used].reshape(DEC_BATCH, n_pages).astype(jnp.int32)
    return {
        'x_prompt': nrm((BATCH, SEQ, D_MODEL)),
        'x_sample': nrm((DEC_BATCH, DEC_SEQ, D_MODEL)),
        'state_hgrn': nrm((N_A, DEC_BATCH, HA_HEADS, HA_DK, HA_DV), 0.5),
        'state_conv': nrm((DEPTH, DEC_BATCH, CONV_W - 1, D_FF)),
        'cache_cmp_k': nrm(page_shape),
        'cache_cmp_v': nrm(page_shape),
        'cache_slc_k': nrm(page_shape),
        'cache_slc_v': nrm(page_shape),
        'cache_win_k': nrm((DEC_BATCH, win_keep, NB_KV, NB_DH)),
        'cache_win_v': nrm((DEC_BATCH, win_keep, NB_KV, NB_DH)),
        'page_table': page_table,
        'p_prompt': nrm((DEPTH, BATCH, SEQ, PLE_DIM)),
        'p_sample': nrm((DEPTH, DEC_BATCH, DEC_SEQ, PLE_DIM)),
        'norm_mix': gain((DEPTH, D_MODEL)),
        'norm_ffn': gain((DEPTH, D_MODEL)),
        'norm_ple': gain((DEPTH, D_MODEL)),
        'a_w_in': nrm((N_A, D_MODEL, 2 * hk + 2 * hv), D_MODEL ** -0.5),
        'a_lb_logits': nrm((N_A + 1, hk), 0.5),
        'a_out_norm': gain((N_A, hv)),
        'a_w_out': nrm((N_A, hv, D_MODEL), hv ** -0.5),
        'kv_norm': gain((D_MODEL,)),
        'kv_w': nrm((D_MODEL, 6 * NB_KV * NB_DH), D_MODEL ** -0.5),
        'k_norm_cmp': gain((NB_DH,)),
        'k_norm_slc': gain((NB_DH,)),
        'k_norm_win': gain((NB_DH,)),
        'cmp_pe_k': nrm((CMP_LEN, NB_DH), 0.5),
        'cmp_pe_v': nrm((CMP_LEN, NB_DH), 0.5),
        'cmp_w1_k': nrm((CMP_LEN, NB_DH, NB_DH), (CMP_LEN * NB_DH) ** -0.5),
        'cmp_w2_k': nrm((NB_DH, NB_DH), NB_DH ** -0.5),
        'cmp_w1_v': nrm((CMP_LEN, NB_DH, NB_DH), (CMP_LEN * NB_DH) ** -0.5),
        'cmp_w2_v': nrm((NB_DH, NB_DH), NB_DH ** -0.5),
        'b_w_in': nrm((N_B, D_MODEL, hd + 3 * NB_HEADS), D_MODEL ** -0.5),
        'b_q_norm': gain((N_B, NB_DH)),
        'b_w_out': nrm((N_B, hd, D_MODEL), hd ** -0.5),
        'ffn_w_up': nrm((DEPTH, D_MODEL, 2 * D_FF), D_MODEL ** -0.5),
        'ffn_conv_w': nrm((DEPTH, CONV_W, D_FF), CONV_W ** -0.5),
        'ffn_conv_b': nrm((DEPTH, D_FF), 0.02),
        'ffn_w_down': nrm((DEPTH, D_FF, D_MODEL), D_FF ** -0.5),
        'ple_w_p': nrm((DEPTH, PLE_DIM, D_MODEL), PLE_DIM ** -0.5),
        'ple_w_g': nrm((DEPTH, D_MODEL, D_MODEL), D_MODEL ** -0.5),
    }


def reference(x_prompt, x_sample, state_hgrn, state_conv, cache_cmp_k, cache_cmp_v, cache_slc_k, cache_slc_v,
              cache_win_k, cache_win_v, page_table, p_prompt, p_sample, norm_mix, norm_ffn, norm_ple,
              a_w_in, a_lb_logits, a_out_norm, a_w_out, kv_norm, kv_w, k_norm_cmp, k_norm_slc, k_norm_win,
              cmp_pe_k, cmp_pe_v, cmp_w1_k, cmp_w2_k, cmp_w1_v, cmp_w2_v, b_w_in, b_q_norm, b_w_out,
              ffn_w_up, ffn_conv_w, ffn_conv_b, ffn_w_down, ple_w_p, ple_w_g):
    wt = {
        'norm_mix': norm_mix, 'norm_ffn': norm_ffn, 'norm_ple': norm_ple,
        'a_w_in': a_w_in, 'a_lb_logits': a_lb_logits, 'a_out_norm': a_out_norm, 'a_w_out': a_w_out,
        'kv_norm': kv_norm, 'kv_w': kv_w, 'k_norm_cmp': k_norm_cmp, 'k_norm_slc': k_norm_slc,
        'k_norm_win': k_norm_win, 'cmp_pe_k': cmp_pe_k, 'cmp_pe_v': cmp_pe_v, 'cmp_w1_k': cmp_w1_k,
        'cmp_w2_k': cmp_w2_k, 'cmp_w1_v': cmp_w1_v, 'cmp_w2_v': cmp_w2_v,
        'b_w_in': b_w_in, 'b_q_norm': b_q_norm, 'b_w_out': b_w_out,
        'ffn_w_up': ffn_w_up, 'ffn_conv_w': ffn_conv_w, 'ffn_conv_b': ffn_conv_b, 'ffn_w_down': ffn_w_down,
        'ple_w_p': ple_w_p, 'ple_w_g': ple_w_g,
    }
    bp = x_prompt.shape[0]
    s_hgrn0 = jnp.zeros((N_A, bp, HA_HEADS, HA_DK, HA_DV), jnp.float32)
    s_conv0 = jnp.zeros((DEPTH, bp, CONV_W - 1, D_FF), x_prompt.dtype)
    y_prompt, hgrn_p, conv_p, rows_p = _trunk(x_prompt, p_prompt, 0, s_hgrn0, s_conv0, None, wt)
    past = (page_table, cache_cmp_k, cache_cmp_v, cache_slc_k, cache_slc_v, cache_win_k, cache_win_v)
    y_sample, hgrn_s, conv_s, rows_s = _trunk(x_sample, p_sample, PAST_LEN, state_hgrn, state_conv, past, wt)
    kc_p, vc_p, ks_p, vs_p, wk_p, wv_p = rows_p
    kc_s, vc_s, ks_s, vs_s, wk_s, wv_s = rows_s
    return (y_prompt, y_sample, hgrn_p, hgrn_s, conv_p, conv_s, kc_p, vc_p, ks_p, vs_p,
            kc_s, vc_s, ks_s, vs_s, wk_p, wv_p, wk_s, wv_s)
```

```python
import functools
import math

import jax
import jax.numpy as jnp
from jax import lax
from jax.experimental import pallas as pl
from jax.experimental.pallas import tpu as pltpu

D_MODEL = 2048
DEPTH = 2
N_A = DEPTH // 2
EPS = 1e-6
HA_DK = 128
HA_HEADS = D_MODEL // HA_DK
HA_DV = D_MODEL // HA_HEADS
HA_CHUNK = 32
NB_DH = 128
NB_HEADS = D_MODEL // NB_DH
NB_KV = 4
HEADS_PER_GROUP = NB_HEADS // NB_KV
CMP_LEN = 32
CMP_STRIDE = 16
SLC_LEN = 64
N_SELECT = 16
WINDOW = 512
Q_BLOCK = 128
ROPE_THETA = 10000.0
D_FF = ((8 * D_MODEL // 3 + 127) // 128) * 128
CONV_W = 3
NEG = -1e30
BIG = 1e30
REMOVED = -3.0e38
LANES = 128
SLC_KV_TILE = 256
VMEM_LIMIT = 48 * 1024 * 1024


def _cmp_topk_kernel(q_ref, k_ref, v_ref, ov_ref, o_ref, sel_ref, *, q_blk, n_cmp_pad, n_slc, n_sel):
    i = pl.program_id(1)
    pos = i * q_blk + lax.broadcasted_iota(jnp.int32, (q_blk, 1), 0)
    kend = lax.broadcasted_iota(jnp.int32, (1, n_cmp_pad), 1) * CMP_STRIDE + (CMP_LEN - 1)
    mask_c = kend <= pos
    k = k_ref[...]
    v = v_ref[...]
    scale = NB_DH ** -0.5
    psum = jnp.zeros((q_blk, n_cmp_pad), jnp.float32)
    for j in range(HEADS_PER_GROUP):
        qj = q_ref[:, j * NB_DH:(j + 1) * NB_DH].astype(jnp.bfloat16)
        s = lax.dot_general(qj, k, (((1,), (1,)), ((), ())), preferred_element_type=jnp.float32) * scale
        s = jnp.where(mask_c, s, NEG)
        m = jnp.max(s, axis=-1, keepdims=True)
        e = jnp.where(mask_c, jnp.exp(s - m), 0.0)
        p = e / jnp.maximum(jnp.sum(e, axis=-1, keepdims=True), 1e-30)
        o_ref[:, j * NB_DH:(j + 1) * NB_DH] = jnp.dot(p.astype(jnp.bfloat16), v, preferred_element_type=jnp.float32)
        psum = psum + p
    hi = psum.astype(jnp.bfloat16)
    lo = (psum - hi.astype(jnp.float32)).astype(jnp.bfloat16)
    ov = ov_ref[...]
    imp = (jnp.dot(hi, ov, preferred_element_type=jnp.float32)
           + jnp.dot(lo, ov, preferred_element_type=jnp.float32))
    sidx = lax.broadcasted_iota(jnp.int32, (1, n_slc), 1)
    cur = lax.shift_right_logical(pos, 6)
    forced = (sidx == 0) | (sidx == cur) | (sidx == cur - 1)
    future = sidx * SLC_LEN > pos
    score = jnp.where(future, NEG, jnp.where(forced, BIG, imp))
    lane = lax.broadcasted_iota(jnp.int32, (q_blk, n_slc), 1).astype(jnp.float32)
    sel = jnp.zeros((q_blk, n_slc), jnp.float32)
    for _ in range(n_sel):
        m = jnp.max(score, axis=-1, keepdims=True)
        idx = jnp.min(jnp.where(score == m, lane, float(n_slc)), axis=-1, keepdims=True)
        hit = lane == idx
        sel = jnp.where(hit, 1.0, sel)
        score = jnp.where(hit, REMOVED, score)
    sel_ref[...] = sel.astype(sel_ref.dtype)


def _cmp_topk_prompt(q2d, kcb, vcb):
    t = q2d.shape[0]
    n_cmp = kcb.shape[0]
    n_cmp_pad = -(-n_cmp // LANES) * LANES
    n_slc = t // SLC_LEN
    assert SLC_LEN == 64 and n_slc == LANES and t % Q_BLOCK == 0
    n_sel = min(N_SELECT, n_slc)
    padc = ((0, 0), (0, n_cmp_pad - n_cmp), (0, 0))
    k = jnp.pad(jnp.transpose(kcb, (1, 0, 2)), padc).astype(jnp.bfloat16)
    v = jnp.pad(jnp.transpose(vcb, (1, 0, 2)), padc).astype(jnp.bfloat16)
    cstart = jnp.arange(n_cmp_pad) * CMP_STRIDE
    sstart = jnp.arange(n_slc) * SLC_LEN
    overlap = ((cstart[:, None] <= sstart[None, :] + (SLC_LEN - 1))
               & (cstart[:, None] + (CMP_LEN - 1) >= sstart[None, :])
               & (jnp.arange(n_cmp_pad)[:, None] < n_cmp)).astype(jnp.bfloat16)
    gd = HEADS_PER_GROUP * NB_DH
    kern = functools.partial(_cmp_topk_kernel, q_blk=Q_BLOCK, n_cmp_pad=n_cmp_pad, n_slc=n_slc, n_sel=n_sel)
    return pl.pallas_call(
        kern,
        grid=(NB_KV, t // Q_BLOCK),
        in_specs=[
            pl.BlockSpec((Q_BLOCK, gd), lambda g, i: (i, g)),
            pl.BlockSpec((None, n_cmp_pad, NB_DH), lambda g, i: (g, 0, 0)),
            pl.BlockSpec((None, n_cmp_pad, NB_DH), lambda g, i: (g, 0, 0)),
            pl.BlockSpec((n_cmp_pad, n_slc), lambda g, i: (0, 0)),
        ],
        out_specs=[
            pl.BlockSpec((Q_BLOCK, gd), lambda g, i: (i, g)),
            pl.BlockSpec((Q_BLOCK, n_slc), lambda g, i: (i, g)),
        ],
        out_shape=[
            jax.ShapeDtypeStruct((t, NB_HEADS * NB_DH), jnp.float32),
            jax.ShapeDtypeStruct((t, NB_KV * n_slc), jnp.bfloat16),
        ],
        compiler_params=pltpu.CompilerParams(
            dimension_semantics=("arbitrary", "arbitrary"), vmem_limit_bytes=VMEM_LIMIT),
        name="cmp_topk_prompt",
    )(q2d, k, v, overlap)


def _slc_attn_kernel(q_ref, k_ref, v_ref, sel_ref, o_ref, m_ref, l_ref, acc_ref, *, q_blk, n_slc, kv_tile):
    i = pl.program_id(1)
    rows = HEADS_PER_GROUP * q_blk
    blocks_per_tile = kv_tile // SLC_LEN
    scale = NB_DH ** -0.5
    q4 = jnp.concatenate([q_ref[:, j * NB_DH:(j + 1) * NB_DH] for j in range(HEADS_PER_GROUP)], axis=0)
    sel = sel_ref[...]
    pos = i * q_blk + lax.broadcasted_iota(jnp.int32, (q_blk, 1), 0)
    m_ref[...] = jnp.full(m_ref.shape, NEG, jnp.float32)
    l_ref[...] = jnp.zeros(l_ref.shape, jnp.float32)
    acc_ref[...] = jnp.zeros(acc_ref.shape, jnp.float32)
    blk_row = lax.broadcasted_iota(jnp.int32, (n_slc, kv_tile), 0)
    blk_col = lax.shift_right_logical(lax.broadcasted_iota(jnp.int32, (n_slc, kv_tile), 1), 6)
    key_off = lax.broadcasted_iota(jnp.int32, (1, kv_tile), 1)
    n_tiles = lax.div(i * q_blk + (q_blk - 1), kv_tile) + 1

    def body(j, carry):
        start = pl.multiple_of(j * kv_tile, kv_tile)
        kt = k_ref[pl.ds(start, kv_tile), :]
        vt = v_ref[pl.ds(start, kv_tile), :]
        s = lax.dot_general(q4, kt, (((1,), (1,)), ((), ())), preferred_element_type=jnp.float32) * scale
        expand = jnp.where(blk_row == blk_col + j * blocks_per_tile, 1.0, 0.0).astype(jnp.bfloat16)
        picked = jnp.dot(sel, expand, preferred_element_type=jnp.float32) > 0.5
        mask = picked & (key_off + start <= pos)
        mask = jnp.concatenate([mask] * HEADS_PER_GROUP, axis=0)
        s = jnp.where(mask, s, NEG)
        m_prev = m_ref[...]
        m_new = jnp.maximum(m_prev, jnp.max(s, axis=-1, keepdims=True))
        alpha = jnp.exp(m_prev - m_new)
        p = jnp.where(mask, jnp.exp(s - m_new), 0.0)
        l_ref[...] = alpha * l_ref[...] + jnp.sum(p, axis=-1, keepdims=True)
        acc_ref[...] = alpha * acc_ref[...] + jnp.dot(p.astype(jnp.bfloat16), vt, preferred_element_type=jnp.float32)
        m_ref[...] = m_new
        return carry

    lax.fori_loop(0, n_tiles, body, 0)
    out = acc_ref[...] / jnp.maximum(l_ref[...], 1e-30)
    for j in range(HEADS_PER_GROUP):
        o_ref[:, j * NB_DH:(j + 1) * NB_DH] = out[j * q_blk:(j + 1) * q_blk, :]


def _slc_attn_prompt(qr2d, kpad, vpad, sel):
    t = qr2d.shape[0]
    n_slc = t // SLC_LEN
    assert t % SLC_KV_TILE == 0 and SLC_KV_TILE % SLC_LEN == 0
    k = jnp.transpose(kpad, (1, 0, 2)).astype(jnp.bfloat16)
    v = jnp.transpose(vpad, (1, 0, 2)).astype(jnp.bfloat16)
    gd = HEADS_PER_GROUP * NB_DH
    rows = HEADS_PER_GROUP * Q_BLOCK
    kern = functools.partial(_slc_attn_kernel, q_blk=Q_BLOCK, n_slc=n_slc, kv_tile=SLC_KV_TILE)
    return pl.pallas_call(
        kern,
        grid=(NB_KV, t // Q_BLOCK),
        in_specs=[
            pl.BlockSpec((Q_BLOCK, gd), lambda g, i: (i, g)),
            pl.BlockSpec((None, t, NB_DH), lambda g, i: (g, 0, 0)),
            pl.BlockSpec((None, t, NB_DH), lambda g, i: (g, 0, 0)),
            pl.BlockSpec((Q_BLOCK, n_slc), lambda g, i: (i, g)),
        ],
        out_specs=pl.BlockSpec((Q_BLOCK, gd), lambda g, i: (i, g)),
        out_shape=jax.ShapeDtypeStruct((t, NB_HEADS * NB_DH), jnp.float32),
        scratch_shapes=[
            pltpu.VMEM((rows, 1), jnp.float32),
            pltpu.VMEM((rows, 1), jnp.float32),
            pltpu.VMEM((rows, NB_DH), jnp.float32),
        ],
        compiler_params=pltpu.CompilerParams(
            dimension_semantics=("arbitrary", "arbitrary"), vmem_limit_bytes=VMEM_LIMIT),
        name="slc_attn_prompt",
    )(qr2d.astype(jnp.bfloat16), k, v, sel)


def _rmsnorm(x, g):
    xf = x.astype(jnp.float32)
    y = xf * lax.rsqrt(jnp.mean(xf * xf, axis=-1, keepdims=True) + EPS)
    return (y * g.astype(jnp.float32)).astype(x.dtype)


def _rope(x, pos):
    half = x.shape[-1] // 2
    inv = ROPE_THETA ** (-jnp.arange(half, dtype=jnp.float32) / half)
    ang = pos.astype(jnp.float32)[:, None] * inv[None, :]
    cos, sin = jnp.cos(ang)[:, None, :], jnp.sin(ang)[:, None, :]
    xf = x.astype(jnp.float32)
    x1, x2 = xf[..., :half], xf[..., half:]
    return jnp.concatenate([x1 * cos - x2 * sin, x2 * cos + x1 * sin], axis=-1).astype(x.dtype)


def _masked_softmax(s, mask):
    s = jnp.where(mask, s, NEG)
    m = jnp.max(s, axis=-1, keepdims=True)
    e = jnp.where(mask, jnp.exp(s - m), 0.0)
    return e / jnp.maximum(jnp.sum(e, axis=-1, keepdims=True), 1e-30)


def _gqa_attend(q, k, v, mask):
    bq, t, h, d = q.shape
    g = k.shape[2]
    qg = q.reshape(bq, t, g, h // g, d).astype(jnp.float32)
    s = jnp.einsum('btgjd,bkgd->btgjk', qg, k.astype(jnp.float32)) * (d ** -0.5)
    p = _masked_softmax(s, mask[:, :, None, None, :])
    o = jnp.einsum('btgjk,bkgd->btgjd', p, v.astype(jnp.float32))
    return o.reshape(bq, t, h, d).astype(q.dtype), p


def _gla_chunked(q, k, v, logf, s0):
    b, t, h, dk = q.shape
    dv = v.shape[-1]
    c = math.gcd(t, HA_CHUNK)
    n = t // c

    def to_chunks(a):
        return jnp.transpose(a.reshape(b, n, c, h, a.shape[-1]), (1, 0, 3, 2, 4))

    qc, kc, vc, gc = to_chunks(q), to_chunks(k), to_chunks(v), to_chunks(logf)
    bcum = jnp.cumsum(gc, axis=3)
    blast = bcum[:, :, :, -1, :]
    q_in = qc * jnp.exp(bcum)
    k_in = kc * jnp.exp(-bcum)
    k_dec = kc * jnp.exp(blast[:, :, :, None, :] - bcum)
    causal = jnp.tril(jnp.ones((c, c), dtype=bool))
    att = jnp.where(causal, jnp.einsum('nbhik,nbhjk->nbhij', q_in, k_in), 0.0)
    o_intra = jnp.einsum('nbhij,nbhjv->nbhiv', att, vc)

    def step(s, inp):
        qi, kd, vi, dl = inp
        o_inter = jnp.einsum('bhik,bhkv->bhiv', qi, s)
        s = jnp.exp(dl)[..., None] * s + jnp.einsum('bhik,bhiv->bhkv', kd, vi)
        return s, o_inter

    s_final, o_inter = lax.scan(step, s0.astype(jnp.float32), (q_in, k_dec, vc, blast))
    o = o_intra + o_inter
    return jnp.transpose(o, (1, 0, 3, 2, 4)).reshape(b, t, h, dv), s_final


def _hgrn2_layer(xn, s0, lb, w_in, out_gain, w_out):
    b, t, _ = xn.shape
    hk = HA_HEADS * HA_DK
    hv = HA_HEADS * HA_DV
    proj = (xn @ w_in).astype(jnp.float32)
    q = jax.nn.silu(proj[..., :hk])
    f = lb + (1.0 - lb) * jax.nn.sigmoid(proj[..., hk:2 * hk])
    i_in = proj[..., 2 * hk:2 * hk + hv]
    gate = proj[..., 2 * hk + hv:]
    shp_k = (b, t, HA_HEADS, HA_DK)
    shp_v = (b, t, HA_HEADS, HA_DV)
    o, s_new = _gla_chunked(q.reshape(shp_k), (1.0 - f).reshape(shp_k), i_in.reshape(shp_v),
                            jnp.log(f).reshape(shp_k), s0)
    o = _rmsnorm(o, out_gain.reshape(HA_HEADS, HA_DV)) * jax.nn.silu(gate).reshape(shp_v)
    return o.reshape(b, t, hv).astype(xn.dtype) @ w_out, s_new


def _conv_ffn(xn, prev, w_up, conv_w, conv_b, w_down):
    t = xn.shape[1]
    up = xn @ w_up
    u, gt = up[..., :D_FF], up[..., D_FF:]
    ext = jnp.concatenate([prev.astype(gt.dtype), gt], axis=1)
    conv = conv_b
    for j in range(CONV_W):
        conv = conv + conv_w[j] * ext[:, j:j + t]
    return (jax.nn.silu(conv) * u) @ w_down, ext[:, t:]


def _ple(h, p, gain, w_p, w_g):
    gate = jax.nn.sigmoid(_rmsnorm(h, gain) @ w_g)
    return h + (p.astype(h.dtype) @ w_p) * gate


def _gather_pages(cache, page_table):
    rows = cache[page_table]
    return rows.reshape(page_table.shape[0], page_table.shape[1] * cache.shape[1], cache.shape[2], cache.shape[3])


def _compress(rows, pe, w1, w2):
    b, l, g, d = rows.shape
    n_sub = l // CMP_STRIDE
    r = CMP_LEN // CMP_STRIDE
    n_cmp = n_sub - r + 1
    sub = rows[:, :n_sub * CMP_STRIDE].reshape(b, n_sub, CMP_STRIDE, g, d)
    pre = None
    for j in range(r):
        sl = slice(j * CMP_STRIDE, (j + 1) * CMP_STRIDE)
        part = jnp.einsum('bnsgd,sde->bnge', sub + pe[sl][None, None, :, None, :], w1[sl])[:, j:j + n_cmp]
        pre = part if pre is None else pre + part
    return jnp.einsum('bnge,ef->bngf', jax.nn.gelu(pre), w2)


def _cmp_slc_attend_sample(q, qr, qpos, kcb, vcb, kpad, vpad, q_blk):
    b, t, h, d = q.shape
    g = kcb.shape[2]
    n_cmp = kcb.shape[1]
    n_slc = kpad.shape[1] // SLC_LEN
    n_sel = min(N_SELECT, n_slc)
    cstart = jnp.arange(n_cmp) * CMP_STRIDE
    kend = cstart + (CMP_LEN - 1)
    sstart = jnp.arange(n_slc) * SLC_LEN
    overlap = ((cstart[:, None] <= sstart[None, :] + (SLC_LEN - 1)) & (kend[:, None] >= sstart[None, :])).astype(jnp.float32)
    sidx = jnp.arange(n_slc)
    offs = jnp.arange(SLC_LEN)
    bi = jnp.arange(b)[:, None, None, None]
    gi = jnp.arange(g)[None, None, :, None]

    def block(args):
        qc_, qr_, pb = args
        mask_c = (kend[None, :] <= pb[:, None])[None]
        o_c, p_c = _gqa_attend(qc_, kcb, vcb, mask_c)
        imp = jnp.einsum('bqgjn,ns->bqgs', p_c, overlap)
        cur = pb // SLC_LEN
        forced = (sidx[None, :] == 0) | (sidx[None, :] == cur[:, None]) | (sidx[None, :] == cur[:, None] - 1)
        future = sstart[None, :] > pb[:, None]
        score = jnp.where(future[None, :, None, :], NEG, jnp.where(forced[None, :, None, :], BIG, imp))
        _, sel = lax.top_k(score, n_sel)
        tok = (sel[..., None] * SLC_LEN + offs).reshape(b, q_blk, g, n_sel * SLC_LEN)
        ks = kpad[bi, tok, gi].astype(jnp.float32)
        vs = vpad[bi, tok, gi].astype(jnp.float32)
        qg = qr_.reshape(b, q_blk, g, h // g, d).astype(jnp.float32)
        s = jnp.einsum('bqgjd,bqgnd->bqgjn', qg, ks) * (d ** -0.5)
        p_s = _masked_softmax(s, (tok <= pb[None, :, None, None])[:, :, :, None, :])
        o_s = jnp.einsum('bqgjn,bqgnd->bqgjd', p_s, vs).reshape(b, q_blk, h, d)
        return o_c, o_s.astype(q.dtype)

    nq = t // q_blk
    qs = jnp.moveaxis(q.reshape(b, nq, q_blk, h, d), 1, 0)
    qrs = jnp.moveaxis(qr.reshape(b, nq, q_blk, h, d), 1, 0)
    ps = qpos.reshape(nq, q_blk)
    o_c, o_s = lax.map(block, (qs, qrs, ps))
    return (jnp.moveaxis(o_c, 0, 1).reshape(b, t, h, d), jnp.moveaxis(o_s, 0, 1).reshape(b, t, h, d))


def _window_banded(q, k, v):
    b, t, h, d = q.shape
    g = k.shape[2]
    nb = t // Q_BLOCK
    span = WINDOW + Q_BLOCK
    padw = ((0, 0), (WINDOW, 0), (0, 0), (0, 0))
    kp, vp = jnp.pad(k, padw), jnp.pad(v, padw)
    idx = jnp.arange(nb)[:, None] * Q_BLOCK + jnp.arange(span)[None, :]
    kb = kp[:, idx].reshape(b * nb, span, g, d)
    vb = vp[:, idx].reshape(b * nb, span, g, d)
    kpos = idx - WINDOW
    qpos = jnp.arange(t).reshape(nb, Q_BLOCK)
    diff = qpos[:, :, None] - kpos[:, None, :]
    mask = jnp.tile((diff >= 0) & (diff <= WINDOW), (b, 1, 1))
    o, _ = _gqa_attend(q.reshape(b * nb, Q_BLOCK, h, d), kb, vb, mask)
    return o.reshape(b, t, h, d)


def _nsa_layer(xn, pos, shared, w_in, q_gain, w_out, banded, q_blk):
    kcb, vcb, kpad, vpad, win_k, win_v, win_pos = shared
    b, t, _ = xn.shape
    hd = NB_HEADS * NB_DH
    proj = xn @ w_in
    q = _rmsnorm(proj[..., :hd].reshape(b, t, NB_HEADS, NB_DH), q_gain)
    gates = jax.nn.sigmoid(proj[..., hd:].astype(jnp.float32)).reshape(b, t, 3, NB_HEADS)
    qr = _rope(q, pos)
    if banded:
        assert b == 1
        o_c2, sel = _cmp_topk_prompt(q.reshape(t, hd), kcb[0], vcb[0])
        o_s2 = _slc_attn_prompt(qr.reshape(t, hd), kpad[0], vpad[0], sel)
        o_c = o_c2.reshape(b, t, NB_HEADS, NB_DH)
        o_s = o_s2.reshape(b, t, NB_HEADS, NB_DH)
        o_w = _window_banded(qr, win_k, win_v)
    else:
        o_c, o_s = _cmp_slc_attend_sample(q, qr, pos, kcb, vcb, kpad, vpad, q_blk)
        diff = pos[:, None] - win_pos[None, :]
        o_w, _ = _gqa_attend(qr, win_k, win_v, ((diff >= 0) & (diff <= WINDOW))[None])
    o = (gates[:, :, 0, :, None] * o_c + gates[:, :, 1, :, None] * o_s + gates[:, :, 2, :, None] * o_w)
    return o.astype(xn.dtype).reshape(b, t, hd) @ w_out


def _shared_kv(h, pos, start, past, wt):
    b, t, _ = h.shape
    kv = (_rmsnorm(h, wt['kv_norm']) @ wt['kv_w']).reshape(b, t, 6, NB_KV, NB_DH)
    k_cmp, v_cmp = kv[:, :, 0], kv[:, :, 1]
    k_slc = _rope(_rmsnorm(kv[:, :, 2], wt['k_norm_slc']), pos)
    v_slc = kv[:, :, 3]
    k_win = _rope(_rmsnorm(kv[:, :, 4], wt['k_norm_win']), pos)
    v_win = kv[:, :, 5]
    if past is None:
        kc_full, vc_full, ks_full, vs_full = k_cmp, v_cmp, k_slc, v_slc
        win_k, win_v, win_pos = k_win, v_win, None
        keep = min(WINDOW, t)
    else:
        page_table, ckc, cvc, cks, cvs, cwk, cwv = past
        kc_full = jnp.concatenate([_gather_pages(ckc, page_table).astype(k_cmp.dtype), k_cmp], axis=1)
        vc_full = jnp.concatenate([_gather_pages(cvc, page_table).astype(v_cmp.dtype), v_cmp], axis=1)
        ks_full = jnp.concatenate([_gather_pages(cks, page_table).astype(k_slc.dtype), k_slc], axis=1)
        vs_full = jnp.concatenate([_gather_pages(cvs, page_table).astype(v_slc.dtype), v_slc], axis=1)
        keep = cwk.shape[1]
        win_k = jnp.concatenate([cwk.astype(k_win.dtype), k_win], axis=1)
        win_v = jnp.concatenate([cwv.astype(v_win.dtype), v_win], axis=1)
        win_pos = start - keep + jnp.arange(keep + t)
    kcb = _rmsnorm(_compress(kc_full, wt['cmp_pe_k'], wt['cmp_w1_k'], wt['cmp_w2_k']), wt['k_norm_cmp'])
    vcb = _compress(vc_full, wt['cmp_pe_v'], wt['cmp_w1_v'], wt['cmp_w2_v'])
    l = ks_full.shape[1]
    n_slc = -(-l // SLC_LEN)
    padw = ((0, 0), (0, n_slc * SLC_LEN - l), (0, 0), (0, 0))
    kpad, vpad = jnp.pad(ks_full, padw), jnp.pad(vs_full, padw)
    shared = (kcb, vcb, kpad, vpad, win_k, win_v, win_pos)
    rows = (k_cmp, v_cmp, k_slc, v_slc, win_k[:, -keep:], win_v[:, -keep:])
    return shared, rows


def _trunk(x, p, start, s_hgrn, s_conv, past, wt):
    b, t, _ = x.shape
    pos = start + jnp.arange(t)
    banded = past is None
    q_blk = math.gcd(t, Q_BLOCK) if banded else 1
    lb = jnp.cumsum(jax.nn.softmax(wt['a_lb_logits'].astype(jnp.float32), axis=0), axis=0)
    h = x
    hgrn_out, conv_out = [], []
    shared, rows = None, None
    for i in range(DEPTH):
        hn = _rmsnorm(h, wt['norm_mix'][i])
        if i < N_A:
            y, s = _hgrn2_layer(hn, s_hgrn[i], lb[i], wt['a_w_in'][i], wt['a_out_norm'][i], wt['a_w_out'][i])
            hgrn_out.append(s)
        else:
            j = i - N_A
            y = _nsa_layer(hn, pos, shared, wt['b_w_in'][j], wt['b_q_norm'][j], wt['b_w_out'][j], banded, q_blk)
        h = h + y
        f, cs = _conv_ffn(_rmsnorm(h, wt['norm_ffn'][i]), s_conv[i], wt['ffn_w_up'][i], wt['ffn_conv_w'][i],
                          wt['ffn_conv_b'][i], wt['ffn_w_down'][i])
        conv_out.append(cs)
        h = h + f
        h = _ple(h, p[i], wt['norm_ple'][i], wt['ple_w_p'][i], wt['ple_w_g'][i])
        if i == N_A - 1:
            shared, rows = _shared_kv(h, pos, start, past, wt)
    return h, jnp.stack(hgrn_out), jnp.stack(conv_out), rows


def kernel(x_prompt, x_sample, state_hgrn, state_conv, cache_cmp_k, cache_cmp_v, cache_slc_k, cache_slc_v,
           cache_win_k, cache_win_v, page_table, p_prompt, p_sample, norm_mix, norm_ffn, norm_ple,
           a_w_in, a_lb_logits, a_out_norm, a_w_out, kv_norm, kv_w, k_norm_cmp, k_norm_slc, k_norm_win,
           cmp_pe_k, cmp_pe_v, cmp_w1_k, cmp_w2_k, cmp_w1_v, cmp_w2_v, b_w_in, b_q_norm, b_w_out,
           ffn_w_up, ffn_conv_w, ffn_conv_b, ffn_w_down, ple_w_p, ple_w_g):
    wt = {
        'norm_mix': norm_mix, 'norm_ffn': norm_ffn, 'norm_ple': norm_ple,
        'a_w_in': a_w_in, 'a_lb_logits': a_lb_logits, 'a_out_norm': a_out_norm, 'a_w_out': a_w_out,
        'kv_norm': kv_norm, 'kv_w': kv_w, 'k_norm_cmp': k_norm_cmp, 'k_norm_slc': k_norm_slc,
        'k_norm_win': k_norm_win, 'cmp_pe_k': cmp_pe_k, 'cmp_pe_v': cmp_pe_v, 'cmp_w1_k': cmp_w1_k,
        'cmp_w2_k': cmp_w2_k, 'cmp_w1_v': cmp_w1_v, 'cmp_w2_v': cmp_w2_v,
        'b_w_in': b_w_in, 'b_q_norm': b_q_norm, 'b_w_out': b_w_out,
        'ffn_w_up': ffn_w_up, 'ffn_conv_w': ffn_conv_w, 'ffn_conv_b': ffn_conv_b, 'ffn_w_down': ffn_w_down,
        'ple_w_p': ple_w_p, 'ple_w_g': ple_w_g,
    }
    bp, past_len = x_prompt.shape[0], page_table.shape[1] * cache_cmp_k.shape[1]
    s_hgrn0 = jnp.zeros((N_A, bp, HA_HEADS, HA_DK, HA_DV), jnp.float32)
    s_conv0 = jnp.zeros((DEPTH, bp, CONV_W - 1, D_FF), x_prompt.dtype)
    y_prompt, hgrn_p, conv_p, rows_p = _trunk(x_prompt, p_prompt, 0, s_hgrn0, s_conv0, None, wt)
    past = (page_table, cache_cmp_k, cache_cmp_v, cache_slc_k, cache_slc_v, cache_win_k, cache_win_v)
    y_sample, hgrn_s, conv_s, rows_s = _trunk(x_sample, p_sample, past_len, state_hgrn, state_conv, past, wt)
    kc_p, vc_p, ks_p, vs_p, wk_p, wv_p = rows_p
    kc_s, vc_s, ks_s, vs_s, wk_s, wv_s = rows_s
    return (y_prompt, y_sample, hgrn_p, hgrn_s, conv_p, conv_s, kc_p, vc_p, ks_p, vs_p,
            kc_s, vc_s, ks_s, vs_s, wk_p, wv_p, wk_s, wv_s)
```

```python
import functools
import math

import jax
import jax.numpy as jnp
from jax import lax
from jax.experimental import pallas as pl
from jax.experimental.pallas import tpu as pltpu

D_MODEL = 2048
DEPTH = 2
N_A = DEPTH // 2
EPS = 1e-6
HA_DK = 128
HA_HEADS = D_MODEL // HA_DK
HA_DV = D_MODEL // HA_HEADS
HA_CHUNK = 32
NB_DH = 128
NB_HEADS = D_MODEL // NB_DH
NB_KV = 4
HEADS_PER_GROUP = NB_HEADS // NB_KV
CMP_LEN = 32
CMP_STRIDE = 16
SLC_LEN = 64
N_SELECT = 16
WINDOW = 512
Q_BLOCK = 128
ROPE_THETA = 10000.0
D_FF = ((8 * D_MODEL // 3 + 127) // 128) * 128
CONV_W = 3
NEG = -1e30
BIG = 1e30
REMOVED = -3.0e38
LANES = 128
SLC_KV_TILE = 512
WIN_SPAN = WINDOW + Q_BLOCK
GLA_ROWS = 256
GLA_HEADS = 4
VMEM_LIMIT = 48 * 1024 * 1024
_NT = (((1,), (1,)), ((), ()))
_TN = (((0,), (0,)), ((), ()))


def _cmp_topk_kernel(q_ref, k_ref, v_ref, ov_ref, o_ref, sel_ref, *, q_blk, n_cmp_pad, n_slc, n_sel):
    i = pl.program_id(1)
    pos = i * q_blk + lax.broadcasted_iota(jnp.int32, (q_blk, 1), 0)
    kend = lax.broadcasted_iota(jnp.int32, (1, n_cmp_pad), 1) * CMP_STRIDE + (CMP_LEN - 1)
    mask_c = kend <= pos
    k = k_ref[...]
    v = v_ref[...]
    scale = NB_DH ** -0.5
    psum = jnp.zeros((q_blk, n_cmp_pad), jnp.float32)
    for j in range(HEADS_PER_GROUP):
        qj = q_ref[:, j * NB_DH:(j + 1) * NB_DH].astype(jnp.bfloat16)
        s = lax.dot_general(qj, k, _NT, preferred_element_type=jnp.float32) * scale
        s = jnp.where(mask_c, s, NEG)
        m = jnp.max(s, axis=-1, keepdims=True)
        e = jnp.where(mask_c, jnp.exp(s - m), 0.0)
        p = e / jnp.maximum(jnp.sum(e, axis=-1, keepdims=True), 1e-30)
        o_ref[:, j * NB_DH:(j + 1) * NB_DH] = jnp.dot(p.astype(jnp.bfloat16), v, preferred_element_type=jnp.float32)
        psum = psum + p
    hi = psum.astype(jnp.bfloat16)
    lo = (psum - hi.astype(jnp.float32)).astype(jnp.bfloat16)
    ov = ov_ref[...]
    imp = (jnp.dot(hi, ov, preferred_element_type=jnp.float32)
           + jnp.dot(lo, ov, preferred_element_type=jnp.float32))
    sidx = lax.broadcasted_iota(jnp.int32, (1, n_slc), 1)
    cur = lax.shift_right_logical(pos, 6)
    forced = (sidx == 0) | (sidx == cur) | (sidx == cur - 1)
    future = sidx * SLC_LEN > pos
    score = jnp.where(future, NEG, jnp.where(forced, BIG, imp))
    lane = lax.broadcasted_iota(jnp.int32, (q_blk, n_slc), 1).astype(jnp.float32)
    sel = jnp.zeros((q_blk, n_slc), jnp.float32)
    for _ in range(n_sel):
        m = jnp.max(score, axis=-1, keepdims=True)
        idx = jnp.min(jnp.where(score == m, lane, float(n_slc)), axis=-1, keepdims=True)
        hit = lane == idx
        sel = jnp.where(hit, 1.0, sel)
        score = jnp.where(hit, REMOVED, score)
    sel_ref[...] = sel.astype(sel_ref.dtype)


def _cmp_topk_prompt(q2d, kcb, vcb):
    t = q2d.shape[0]
    n_cmp = kcb.shape[0]
    n_cmp_pad = -(-n_cmp // LANES) * LANES
    n_slc = t // SLC_LEN
    assert SLC_LEN == 64 and n_slc == LANES and t % Q_BLOCK == 0
    n_sel = min(N_SELECT, n_slc)
    padc = ((0, 0), (0, n_cmp_pad - n_cmp), (0, 0))
    k = jnp.pad(jnp.transpose(kcb, (1, 0, 2)), padc).astype(jnp.bfloat16)
    v = jnp.pad(jnp.transpose(vcb, (1, 0, 2)), padc).astype(jnp.bfloat16)
    cstart = jnp.arange(n_cmp_pad) * CMP_STRIDE
    sstart = jnp.arange(n_slc) * SLC_LEN
    overlap = ((cstart[:, None] <= sstart[None, :] + (SLC_LEN - 1))
               & (cstart[:, None] + (CMP_LEN - 1) >= sstart[None, :])
               & (jnp.arange(n_cmp_pad)[:, None] < n_cmp)).astype(jnp.bfloat16)
    gd = HEADS_PER_GROUP * NB_DH
    kern = functools.partial(_cmp_topk_kernel, q_blk=Q_BLOCK, n_cmp_pad=n_cmp_pad, n_slc=n_slc, n_sel=n_sel)
    return pl.pallas_call(
        kern,
        grid=(NB_KV, t // Q_BLOCK),
        in_specs=[
            pl.BlockSpec((Q_BLOCK, gd), lambda g, i: (i, g)),
            pl.BlockSpec((None, n_cmp_pad, NB_DH), lambda g, i: (g, 0, 0)),
            pl.BlockSpec((None, n_cmp_pad, NB_DH), lambda g, i: (g, 0, 0)),
            pl.BlockSpec((n_cmp_pad, n_slc), lambda g, i: (0, 0)),
        ],
        out_specs=[
            pl.BlockSpec((Q_BLOCK, gd), lambda g, i: (i, g)),
            pl.BlockSpec((Q_BLOCK, n_slc), lambda g, i: (i, g)),
        ],
        out_shape=[
            jax.ShapeDtypeStruct((t, NB_HEADS * NB_DH), jnp.float32),
            jax.ShapeDtypeStruct((t, NB_KV * n_slc), jnp.bfloat16),
        ],
        compiler_params=pltpu.CompilerParams(
            dimension_semantics=("arbitrary", "arbitrary"), vmem_limit_bytes=VMEM_LIMIT),
        name="cmp_topk_prompt",
    )(q2d, k, v, overlap)


def _slc_win_kernel(q_ref, ks_ref, vs_ref, kw_ref, vw_ref, sel_ref, os_ref, ow_ref, m_ref, l_ref, acc_ref,
                    *, q_blk, n_slc, kv_tile):
    i = pl.program_id(1)
    rows = HEADS_PER_GROUP * q_blk
    blocks_per_tile = kv_tile // SLC_LEN
    reps = kv_tile // LANES
    scale = NB_DH ** -0.5
    q4 = jnp.concatenate([q_ref[:, j * NB_DH:(j + 1) * NB_DH] for j in range(HEADS_PER_GROUP)], axis=0)
    sel = sel_ref[...]
    pos = i * q_blk + lax.broadcasted_iota(jnp.int32, (q_blk, 1), 0)
    m_ref[...] = jnp.full(m_ref.shape, NEG, jnp.float32)
    l_ref[...] = jnp.zeros(l_ref.shape, jnp.float32)
    acc_ref[...] = jnp.zeros(acc_ref.shape, jnp.float32)
    blk_row = lax.broadcasted_iota(jnp.int32, (n_slc, kv_tile), 0)
    blk_col = lax.shift_right_logical(lax.broadcasted_iota(jnp.int32, (n_slc, kv_tile), 1), 6)
    key_off = lax.broadcasted_iota(jnp.int32, (1, kv_tile), 1)
    n_tiles = lax.div(i * q_blk + (q_blk - 1), kv_tile) + 1

    def body(j, carry):
        start = pl.multiple_of(j * kv_tile, kv_tile)
        kt = ks_ref[pl.ds(start, kv_tile), :]
        vt = vs_ref[pl.ds(start, kv_tile), :]
        s = lax.dot_general(q4, kt, _NT, preferred_element_type=jnp.float32) * scale
        expand = jnp.where(blk_row == blk_col + j * blocks_per_tile, 1.0, 0.0).astype(jnp.bfloat16)
        picked = jnp.dot(sel, expand, preferred_element_type=jnp.float32) > 0.5
        mask = picked & (key_off + start <= pos)
        s = jnp.where(mask[None], s.reshape(HEADS_PER_GROUP, q_blk, kv_tile), NEG).reshape(rows, kv_tile)
        m_prev = m_ref[...]
        m_new = jnp.maximum(m_prev, jnp.max(s, axis=-1, keepdims=True))
        alpha = jnp.exp(m_prev - m_new)
        p = jnp.exp(s - jnp.concatenate([m_new] * reps, axis=1))
        l_ref[...] = alpha * l_ref[...] + jnp.sum(p, axis=-1, keepdims=True)
        acc_ref[...] = alpha * acc_ref[...] + jnp.dot(p.astype(jnp.bfloat16), vt, preferred_element_type=jnp.float32)
        m_ref[...] = m_new
        return carry

    lax.fori_loop(0, n_tiles, body, 0)
    out = acc_ref[...] / jnp.maximum(l_ref[...], 1e-30)
    for j in range(HEADS_PER_GROUP):
        os_ref[:, j * NB_DH:(j + 1) * NB_DH] = out[j * q_blk:(j + 1) * q_blk, :]

    wstart = pl.multiple_of(jnp.maximum(i - WINDOW // q_blk, 0) * q_blk, q_blk)
    kt = kw_ref[pl.ds(wstart, WIN_SPAN), :]
    vt = vw_ref[pl.ds(wstart, WIN_SPAN), :]
    s = lax.dot_general(q4, kt, _NT, preferred_element_type=jnp.float32) * scale
    diff = pos - (wstart + lax.broadcasted_iota(jnp.int32, (1, WIN_SPAN), 1))
    mask = ((diff >= 0) & (diff <= WINDOW))[None]
    s = jnp.where(mask, s.reshape(HEADS_PER_GROUP, q_blk, WIN_SPAN), NEG)
    n_ghost = jnp.maximum(WINDOW - pos, 0).astype(jnp.float32)[None]
    m = jnp.max(s, axis=-1, keepdims=True)
    m = jnp.where(n_ghost > 0, jnp.maximum(m, 0.0), m)
    e = jnp.where(mask, jnp.exp(s - m), 0.0)
    p = e / jnp.maximum(jnp.sum(e, axis=-1, keepdims=True) + n_ghost * jnp.exp(-m), 1e-30)
    ow = jnp.dot(p.reshape(rows, WIN_SPAN).astype(jnp.bfloat16), vt, preferred_element_type=jnp.float32)
    for j in range(HEADS_PER_GROUP):
        ow_ref[:, j * NB_DH:(j + 1) * NB_DH] = ow[j * q_blk:(j + 1) * q_blk, :]


def _slc_win_prompt(qr2d, kpad, vpad, kwin, vwin, sel):
    t = qr2d.shape[0]
    n_slc = t // SLC_LEN
    assert t % SLC_KV_TILE == 0 and SLC_KV_TILE % SLC_LEN == 0 and t >= WIN_SPAN

    def group_major(a):
        return jnp.transpose(a, (1, 0, 2)).astype(jnp.bfloat16)

    gd = HEADS_PER_GROUP * NB_DH
    rows = HEADS_PER_GROUP * Q_BLOCK
    kern = functools.partial(_slc_win_kernel, q_blk=Q_BLOCK, n_slc=n_slc, kv_tile=SLC_KV_TILE)
    kv_spec = pl.BlockSpec((None, t, NB_DH), lambda g, i: (g, 0, 0))
    o_spec = pl.BlockSpec((Q_BLOCK, gd), lambda g, i: (i, g))
    return pl.pallas_call(
        kern,
        grid=(NB_KV, t // Q_BLOCK),
        in_specs=[o_spec, kv_spec, kv_spec, kv_spec, kv_spec, pl.BlockSpec((Q_BLOCK, n_slc), lambda g, i: (i, g))],
        out_specs=[o_spec, o_spec],
        out_shape=[jax.ShapeDtypeStruct((t, NB_HEADS * NB_DH), jnp.float32)] * 2,
        scratch_shapes=[
            pltpu.VMEM((rows, LANES), jnp.float32),
            pltpu.VMEM((rows, LANES), jnp.float32),
            pltpu.VMEM((rows, NB_DH), jnp.float32),
        ],
        compiler_params=pltpu.CompilerParams(
            dimension_semantics=("arbitrary", "arbitrary"), vmem_limit_bytes=VMEM_LIMIT),
        name="slc_win_prompt",
    )(qr2d.astype(jnp.bfloat16), group_major(kpad), group_major(vpad), group_major(kwin), group_major(vwin), sel)


def _split3(x):
    hi = x.astype(jnp.bfloat16)
    r = x - hi.astype(jnp.float32)
    mid = r.astype(jnp.bfloat16)
    lo = (r - mid.astype(jnp.float32)).astype(jnp.bfloat16)
    return hi, mid, lo


def _gla_kernel(q_ref, k_ref, v_ref, g_ref, o_ref, sfin_ref, st_ref, *, rows, heads, chunk):
    i = pl.program_id(1)

    @pl.when(i == 0)
    def _():
        st_ref[...] = jnp.zeros(st_ref.shape, jnp.float32)

    r_idx = lax.broadcasted_iota(jnp.int32, (rows, rows), 0)
    c_idx = lax.broadcasted_iota(jnp.int32, (rows, rows), 1)
    same = lax.shift_right_logical(r_idx, 5) == lax.shift_right_logical(c_idx, 5)
    causal = same & (c_idx <= r_idx)
    tri = jnp.where(causal, 1.0, 0.0).astype(jnp.bfloat16)
    blk = jnp.where(same, 1.0, 0.0).astype(jnp.bfloat16)
    f32 = jnp.float32
    for hh in range(heads):
        sl = slice(hh * HA_DK, (hh + 1) * HA_DK)
        qh, kh, vh, gh = q_ref[:, sl], k_ref[:, sl], v_ref[:, sl], g_ref[:, sl]
        parts = _split3(gh)
        bcum = sum(jnp.dot(tri, p, preferred_element_type=f32) for p in parts)
        tot = sum(jnp.dot(blk, p, preferred_element_type=f32) for p in parts)
        q_in = (qh * jnp.exp(bcum)).astype(jnp.bfloat16)
        k_in = (kh * jnp.exp(-bcum)).astype(jnp.bfloat16)
        k_dec = (kh * jnp.exp(tot - bcum)).astype(jnp.bfloat16)
        vb = vh.astype(jnp.bfloat16)
        att = lax.dot_general(q_in, k_in, _NT, preferred_element_type=f32)
        att = jnp.where(causal, att, 0.0).astype(jnp.bfloat16)
        o_intra = jnp.dot(att, vb, preferred_element_type=f32)
        st = st_ref[hh]
        for c in range(rows // chunk):
            rs = slice(c * chunk, (c + 1) * chunk)
            o_inter = lax.dot_general(q_in[rs], st.astype(jnp.bfloat16), _NT, preferred_element_type=f32)
            o_ref[rs, sl] = o_intra[rs] + o_inter
            upd = lax.dot_general(vb[rs], k_dec[rs], _TN, preferred_element_type=f32)
            st = jnp.exp(tot[c * chunk:c * chunk + 1, :]) * st + upd
        st_ref[hh] = st

    @pl.when(i == pl.num_programs(1) - 1)
    def _():
        for hh in range(heads):
            sfin_ref[hh] = st_ref[hh].T


def _gla_prompt(q, k, v, logf):
    t = q.shape[0]
    assert t % GLA_ROWS == 0 and GLA_ROWS % HA_CHUNK == 0 and HA_CHUNK == 32
    assert HA_HEADS % GLA_HEADS == 0 and HA_DK == HA_DV
    w = GLA_HEADS * HA_DK
    spec = pl.BlockSpec((GLA_ROWS, w), lambda h, i: (i, h))
    kern = functools.partial(_gla_kernel, rows=GLA_ROWS, heads=GLA_HEADS, chunk=HA_CHUNK)
    return pl.pallas_call(
        kern,
        grid=(HA_HEADS // GLA_HEADS, t // GLA_ROWS),
        in_specs=[spec] * 4,
        out_specs=[spec, pl.BlockSpec((GLA_HEADS, HA_DK, HA_DV), lambda h, i: (h, 0, 0))],
        out_shape=[jax.ShapeDtypeStruct((t, HA_HEADS * HA_DV), jnp.float32),
                   jax.ShapeDtypeStruct((HA_HEADS, HA_DK, HA_DV), jnp.float32)],
        scratch_shapes=[pltpu.VMEM((GLA_HEADS, HA_DV, HA_DK), jnp.float32)],
        compiler_params=pltpu.CompilerParams(
            dimension_semantics=("arbitrary", "arbitrary"), vmem_limit_bytes=VMEM_LIMIT),
        name="gla_prompt",
    )(q, k, v, logf)


def _mm_kernel(x_ref, w_ref, o_ref, wb_ref):
    @pl.when(pl.program_id(1) == 0)
    def _():
        wb_ref[...] = w_ref[...].astype(jnp.bfloat16)

    o_ref[...] = jnp.dot(x_ref[...], wb_ref[...], preferred_element_type=jnp.float32)


def _mm_tiles(m, k, n):
    tm = min(m, 1024 if k <= 2048 else 512)
    tn = min(n, 512 if k <= 2048 else 256)
    return tm, tn


def _mm(x, w):
    lead = x.shape[:-1]
    k = x.shape[-1]
    n = w.shape[1]
    x2 = x.reshape(-1, k).astype(jnp.bfloat16)
    m = x2.shape[0]
    tm, tn = _mm_tiles(m, k, n)
    assert m % tm == 0
    out = pl.pallas_call(
        _mm_kernel,
        grid=(pl.cdiv(n, tn), m // tm),
        in_specs=[pl.BlockSpec((tm, k), lambda j, i: (i, 0)),
                  pl.BlockSpec((k, tn), lambda j, i: (0, j))],
        out_specs=pl.BlockSpec((tm, tn), lambda j, i: (i, j)),
        out_shape=jax.ShapeDtypeStruct((m, n), jnp.float32),
        scratch_shapes=[pltpu.VMEM((k, tn), jnp.bfloat16)],
        compiler_params=pltpu.CompilerParams(
            dimension_semantics=("arbitrary", "arbitrary"), vmem_limit_bytes=VMEM_LIMIT),
        name="mm",
    )(x2, w)
    return out.reshape(*lead, n)


def _rmsnorm(x, g):
    xf = x.astype(jnp.float32)
    y = xf * lax.rsqrt(jnp.mean(xf * xf, axis=-1, keepdims=True) + EPS)
    return (y * g.astype(jnp.float32)).astype(x.dtype)


def _rope(x, pos):
    half = x.shape[-1] // 2
    inv = ROPE_THETA ** (-jnp.arange(half, dtype=jnp.float32) / half)
    ang = pos.astype(jnp.float32)[:, None] * inv[None, :]
    cos, sin = jnp.cos(ang)[:, None, :], jnp.sin(ang)[:, None, :]
    xf = x.astype(jnp.float32)
    x1, x2 = xf[..., :half], xf[..., half:]
    return jnp.concatenate([x1 * cos - x2 * sin, x2 * cos + x1 * sin], axis=-1).astype(x.dtype)


def _masked_softmax(s, mask):
    s = jnp.where(mask, s, NEG)
    m = jnp.max(s, axis=-1, keepdims=True)
    e = jnp.where(mask, jnp.exp(s - m), 0.0)
    return e / jnp.maximum(jnp.sum(e, axis=-1, keepdims=True), 1e-30)


def _gqa_attend(q, k, v, mask):
    bq, t, h, d = q.shape
    g = k.shape[2]
    qg = q.reshape(bq, t, g, h // g, d).astype(jnp.float32)
    s = jnp.einsum('btgjd,bkgd->btgjk', qg, k.astype(jnp.float32)) * (d ** -0.5)
    p = _masked_softmax(s, mask[:, :, None, None, :])
    o = jnp.einsum('btgjk,bkgd->btgjd', p, v.astype(jnp.float32))
    return o.reshape(bq, t, h, d).astype(q.dtype), p


def _gla_chunked(q, k, v, logf, s0):
    b, t, h, dk = q.shape
    dv = v.shape[-1]
    c = math.gcd(t, HA_CHUNK)
    n = t // c

    def to_chunks(a):
        return jnp.transpose(a.reshape(b, n, c, h, a.shape[-1]), (1, 0, 3, 2, 4))

    qc, kc, vc, gc = to_chunks(q), to_chunks(k), to_chunks(v), to_chunks(logf)
    bcum = jnp.cumsum(gc, axis=3)
    blast = bcum[:, :, :, -1, :]
    q_in = qc * jnp.exp(bcum)
    k_in = kc * jnp.exp(-bcum)
    k_dec = kc * jnp.exp(blast[:, :, :, None, :] - bcum)
    causal = jnp.tril(jnp.ones((c, c), dtype=bool))
    att = jnp.where(causal, jnp.einsum('nbhik,nbhjk->nbhij', q_in, k_in), 0.0)
    o_intra = jnp.einsum('nbhij,nbhjv->nbhiv', att, vc)

    def step(s, inp):
        qi, kd, vi, dl = inp
        o_inter = jnp.einsum('bhik,bhkv->bhiv', qi, s)
        s = jnp.exp(dl)[..., None] * s + jnp.einsum('bhik,bhiv->bhkv', kd, vi)
        return s, o_inter

    s_final, o_inter = lax.scan(step, s0.astype(jnp.float32), (q_in, k_dec, vc, blast))
    o = o_intra + o_inter
    return jnp.transpose(o, (1, 0, 3, 2, 4)).reshape(b, t, h, dv), s_final


def _hgrn2_layer(xn, s0, lb, w_in, out_gain, w_out):
    b, t, _ = xn.shape
    hk = HA_HEADS * HA_DK
    hv = HA_HEADS * HA_DV
    proj = _mm(xn, w_in)
    q = jax.nn.silu(proj[..., :hk])
    f = lb + (1.0 - lb) * jax.nn.sigmoid(proj[..., hk:2 * hk])
    i_in = proj[..., 2 * hk:2 * hk + hv]
    gate = proj[..., 2 * hk + hv:]
    shp_k = (b, t, HA_HEADS, HA_DK)
    shp_v = (b, t, HA_HEADS, HA_DV)
    if s0 is None:
        assert b == 1
        o, s_new = _gla_prompt(q.reshape(t, hk), (1.0 - f).reshape(t, hk), i_in.reshape(t, hv),
                               jnp.log(f).reshape(t, hk))
        o, s_new = o.reshape(shp_v), s_new[None]
    else:
        o, s_new = _gla_chunked(q.reshape(shp_k), (1.0 - f).reshape(shp_k), i_in.reshape(shp_v),
                                jnp.log(f).reshape(shp_k), s0)
    o = _rmsnorm(o, out_gain.reshape(HA_HEADS, HA_DV)) * jax.nn.silu(gate).reshape(shp_v)
    return _mm(o.reshape(b, t, hv), w_out), s_new


def _conv_ffn(xn, prev, w_up, conv_w, conv_b, w_down):
    t = xn.shape[1]
    up = _mm(xn, w_up)
    u, gt = up[..., :D_FF], up[..., D_FF:]
    ext = jnp.concatenate([prev.astype(gt.dtype), gt], axis=1)
    conv = conv_b
    for j in range(CONV_W):
        conv = conv + conv_w[j] * ext[:, j:j + t]
    return _mm(jax.nn.silu(conv) * u, w_down), ext[:, t:]


def _ple(h, p, gain, w_p, w_g):
    gate = jax.nn.sigmoid(_mm(_rmsnorm(h, gain), w_g))
    return h + _mm(p, w_p) * gate


def _gather_pages(cache, page_table):
    rows = cache[page_table]
    return rows.reshape(page_table.shape[0], page_table.shape[1] * cache.shape[1], cache.shape[2], cache.shape[3])


def _compress(rows, pe, w1, w2):
    b, l, g, d = rows.shape
    n_sub = l // CMP_STRIDE
    r = CMP_LEN // CMP_STRIDE
    n_cmp = n_sub - r + 1
    sub = rows[:, :n_sub * CMP_STRIDE].reshape(b, n_sub, CMP_STRIDE, g, d)
    pre = None
    for j in range(r):
        sl = slice(j * CMP_STRIDE, (j + 1) * CMP_STRIDE)
        part = jnp.einsum('bnsgd,sde->bnge', sub + pe[sl][None, None, :, None, :], w1[sl])[:, j:j + n_cmp]
        pre = part if pre is None else pre + part
    return jnp.einsum('bnge,ef->bngf', jax.nn.gelu(pre), w2)


def _cmp_slc_attend_sample(q, qr, qpos, kcb, vcb, kpad, vpad, q_blk):
    b, t, h, d = q.shape
    g = kcb.shape[2]
    n_cmp = kcb.shape[1]
    n_slc = kpad.shape[1] // SLC_LEN
    n_sel = min(N_SELECT, n_slc)
    cstart = jnp.arange(n_cmp) * CMP_STRIDE
    kend = cstart + (CMP_LEN - 1)
    sstart = jnp.arange(n_slc) * SLC_LEN
    overlap = ((cstart[:, None] <= sstart[None, :] + (SLC_LEN - 1)) & (kend[:, None] >= sstart[None, :])).astype(jnp.float32)
    sidx = jnp.arange(n_slc)
    offs = jnp.arange(SLC_LEN)
    bi = jnp.arange(b)[:, None, None, None]
    gi = jnp.arange(g)[None, None, :, None]

    def block(args):
        qc_, qr_, pb = args
        mask_c = (kend[None, :] <= pb[:, None])[None]
        o_c, p_c = _gqa_attend(qc_, kcb, vcb, mask_c)
        imp = jnp.einsum('bqgjn,ns->bqgs', p_c, overlap)
        cur = pb // SLC_LEN
        forced = (sidx[None, :] == 0) | (sidx[None, :] == cur[:, None]) | (sidx[None, :] == cur[:, None] - 1)
        future = sstart[None, :] > pb[:, None]
        score = jnp.where(future[None, :, None, :], NEG, jnp.where(forced[None, :, None, :], BIG, imp))
        _, sel = lax.top_k(score, n_sel)
        tok = (sel[..., None] * SLC_LEN + offs).reshape(b, q_blk, g, n_sel * SLC_LEN)
        ks = kpad[bi, tok, gi].astype(jnp.float32)
        vs = vpad[bi, tok, gi].astype(jnp.float32)
        qg = qr_.reshape(b, q_blk, g, h // g, d).astype(jnp.float32)
        s = jnp.einsum('bqgjd,bqgnd->bqgjn', qg, ks) * (d ** -0.5)
        p_s = _masked_softmax(s, (tok <= pb[None, :, None, None])[:, :, :, None, :])
        o_s = jnp.einsum('bqgjn,bqgnd->bqgjd', p_s, vs).reshape(b, q_blk, h, d)
        return o_c, o_s.astype(q.dtype)

    nq = t // q_blk
    qs = jnp.moveaxis(q.reshape(b, nq, q_blk, h, d), 1, 0)
    qrs = jnp.moveaxis(qr.reshape(b, nq, q_blk, h, d), 1, 0)
    ps = qpos.reshape(nq, q_blk)
    o_c, o_s = lax.map(block, (qs, qrs, ps))
    return (jnp.moveaxis(o_c, 0, 1).reshape(b, t, h, d), jnp.moveaxis(o_s, 0, 1).reshape(b, t, h, d))


def _nsa_layer(xn, pos, shared, w_in, q_gain, w_out, banded, q_blk):
    kcb, vcb, kpad, vpad, win_k, win_v, win_pos = shared
    b, t, _ = xn.shape
    hd = NB_HEADS * NB_DH
    proj = _mm(xn, w_in)
    q = _rmsnorm(proj[..., :hd].reshape(b, t, NB_HEADS, NB_DH), q_gain)
    gates = jax.nn.sigmoid(proj[..., hd:].astype(jnp.float32)).reshape(b, t, 3, NB_HEADS)
    qr = _rope(q, pos)
    if banded:
        assert b == 1
        o_c2, sel = _cmp_topk_prompt(q.reshape(t, hd), kcb[0], vcb[0])
        o_s2, o_w2 = _slc_win_prompt(qr.reshape(t, hd), kpad[0], vpad[0], win_k[0], win_v[0], sel)
        o_c = o_c2.reshape(b, t, NB_HEADS, NB_DH)
        o_s = o_s2.reshape(b, t, NB_HEADS, NB_DH)
        o_w = o_w2.reshape(b, t, NB_HEADS, NB_DH)
    else:
        o_c, o_s = _cmp_slc_attend_sample(q, qr, pos, kcb, vcb, kpad, vpad, q_blk)
        diff = pos[:, None] - win_pos[None, :]
        o_w, _ = _gqa_attend(qr, win_k, win_v, ((diff >= 0) & (diff <= WINDOW))[None])
    o = (gates[:, :, 0, :, None] * o_c + gates[:, :, 1, :, None] * o_s + gates[:, :, 2, :, None] * o_w)
    return _mm(o.reshape(b, t, hd), w_out)


def _shared_kv(h, pos, start, past, wt):
    b, t, _ = h.shape
    kv = _mm(_rmsnorm(h, wt['kv_norm']), wt['kv_w']).reshape(b, t, 6, NB_KV, NB_DH)
    k_cmp, v_cmp = kv[:, :, 0], kv[:, :, 1]
    k_slc = _rope(_rmsnorm(kv[:, :, 2], wt['k_norm_slc']), pos)
    v_slc = kv[:, :, 3]
    k_win = _rope(_rmsnorm(kv[:, :, 4], wt['k_norm_win']), pos)
    v_win = kv[:, :, 5]
    if past is None:
        kc_full, vc_full, ks_full, vs_full = k_cmp, v_cmp, k_slc, v_slc
        win_k, win_v, win_pos = k_win, v_win, None
        keep = min(WINDOW, t)
    else:
        page_table, ckc, cvc, cks, cvs, cwk, cwv = past
        kc_full = jnp.concatenate([_gather_pages(ckc, page_table).astype(k_cmp.dtype), k_cmp], axis=1)
        vc_full = jnp.concatenate([_gather_pages(cvc, page_table).astype(v_cmp.dtype), v_cmp], axis=1)
        ks_full = jnp.concatenate([_gather_pages(cks, page_table).astype(k_slc.dtype), k_slc], axis=1)
        vs_full = jnp.concatenate([_gather_pages(cvs, page_table).astype(v_slc.dtype), v_slc], axis=1)
        keep = cwk.shape[1]
        win_k = jnp.concatenate([cwk.astype(k_win.dtype), k_win], axis=1)
        win_v = jnp.concatenate([cwv.astype(v_win.dtype), v_win], axis=1)
        win_pos = start - keep + jnp.arange(keep + t)
    kcb = _rmsnorm(_compress(kc_full, wt['cmp_pe_k'], wt['cmp_w1_k'], wt['cmp_w2_k']), wt['k_norm_cmp'])
    vcb = _compress(vc_full, wt['cmp_pe_v'], wt['cmp_w1_v'], wt['cmp_w2_v'])
    l = ks_full.shape[1]
    n_slc = -(-l // SLC_LEN)
    padw = ((0, 0), (0, n_slc * SLC_LEN - l), (0, 0), (0, 0))
    kpad, vpad = jnp.pad(ks_full, padw), jnp.pad(vs_full, padw)
    shared = (kcb, vcb, kpad, vpad, win_k, win_v, win_pos)
    rows = (k_cmp, v_cmp, k_slc, v_slc, win_k[:, -keep:], win_v[:, -keep:])
    return shared, rows


def _trunk(x, p, start, s_hgrn, s_conv, past, wt):
    b, t, _ = x.shape
    pos = start + jnp.arange(t)
    banded = past is None
    q_blk = math.gcd(t, Q_BLOCK) if banded else 1
    lb = jnp.cumsum(jax.nn.softmax(wt['a_lb_logits'].astype(jnp.float32), axis=0), axis=0)
    h = x
    hgrn_out, conv_out = [], []
    shared, rows = None, None
    for i in range(DEPTH):
        hn = _rmsnorm(h, wt['norm_mix'][i])
        if i < N_A:
            y, s = _hgrn2_layer(hn, None if s_hgrn is None else s_hgrn[i], lb[i], wt['a_w_in'][i],
                                wt['a_out_norm'][i], wt['a_w_out'][i])
            hgrn_out.append(s)
        else:
            j = i - N_A
            y = _nsa_layer(hn, pos, shared, wt['b_w_in'][j], wt['b_q_norm'][j], wt['b_w_out'][j], banded, q_blk)
        h = h + y
        f, cs = _conv_ffn(_rmsnorm(h, wt['norm_ffn'][i]), s_conv[i], wt['ffn_w_up'][i], wt['ffn_conv_w'][i],
                          wt['ffn_conv_b'][i], wt['ffn_w_down'][i])
        conv_out.append(cs)
        h = h + f
        h = _ple(h, p[i], wt['norm_ple'][i], wt['ple_w_p'][i], wt['ple_w_g'][i])
        if i == N_A - 1:
            shared, rows = _shared_kv(h, pos, start, past, wt)
    return h, jnp.stack(hgrn_out), jnp.stack(conv_out), rows


def kernel(x_prompt, x_sample, state_hgrn, state_conv, cache_cmp_k, cache_cmp_v, cache_slc_k, cache_slc_v,
           cache_win_k, cache_win_v, page_table, p_prompt, p_sample, norm_mix, norm_ffn, norm_ple,
           a_w_in, a_lb_logits, a_out_norm, a_w_out, kv_norm, kv_w, k_norm_cmp, k_norm_slc, k_norm_win,
           cmp_pe_k, cmp_pe_v, cmp_w1_k, cmp_w2_k, cmp_w1_v, cmp_w2_v, b_w_in, b_q_norm, b_w_out,
           ffn_w_up, ffn_conv_w, ffn_conv_b, ffn_w_down, ple_w_p, ple_w_g):
    wt = {
        'norm_mix': norm_mix, 'norm_ffn': norm_ffn, 'norm_ple': norm_ple,
        'a_w_in': a_w_in, 'a_lb_logits': a_lb_logits, 'a_out_norm': a_out_norm, 'a_w_out': a_w_out,
        'kv_norm': kv_norm, 'kv_w': kv_w, 'k_norm_cmp': k_norm_cmp, 'k_norm_slc': k_norm_slc,
        'k_norm_win': k_norm_win, 'cmp_pe_k': cmp_pe_k, 'cmp_pe_v': cmp_pe_v, 'cmp_w1_k': cmp_w1_k,
        'cmp_w2_k': cmp_w2_k, 'cmp_w1_v': cmp_w1_v, 'cmp_w2_v': cmp_w2_v,
        'b_w_in': b_w_in, 'b_q_norm': b_q_norm, 'b_w_out': b_w_out,
        'ffn_w_up': ffn_w_up, 'ffn_conv_w': ffn_conv_w, 'ffn_conv_b': ffn_conv_b, 'ffn_w_down': ffn_w_down,
        'ple_w_p': ple_w_p, 'ple_w_g': ple_w_g,
    }
    bp, past_len = x_prompt.shape[0], page_table.shape[1] * cache_cmp_k.shape[1]
    s_conv0 = jnp.zeros((DEPTH, bp, CONV_W - 1, D_FF), x_prompt.dtype)
    y_prompt, hgrn_p, conv_p, rows_p = _trunk(x_prompt, p_prompt, 0, None, s_conv0, None, wt)
    past = (page_table, cache_cmp_k, cache_cmp_v, cache_slc_k, cache_slc_v, cache_win_k, cache_win_v)
    y_sample, hgrn_s, conv_s, rows_s = _trunk(x_sample, p_sample, past_len, state_hgrn, state_conv, past, wt)
    kc_p, vc_p, ks_p, vs_p, wk_p, wv_p = rows_p
    kc_s, vc_s, ks_s, vs_s, wk_s, wv_s = rows_s
    return (y_prompt, y_sample, hgrn_p, hgrn_s, conv_p, conv_s, kc_p, vc_p, ks_p, vs_p,
            kc_s, vc_s, ks_s, vs_s, wk_p, wv_p, wk_s, wv_s)
```

```python
import functools
import math

import jax
import jax.numpy as jnp
from jax import lax
from jax.experimental import pallas as pl
from jax.experimental.pallas import tpu as pltpu

D_MODEL = 2048
DEPTH = 2
N_A = DEPTH // 2
EPS = 1e-6
HA_DK = 128
HA_HEADS = D_MODEL // HA_DK
HA_DV = D_MODEL // HA_HEADS
HA_CHUNK = 32
NB_DH = 128
NB_HEADS = D_MODEL // NB_DH
NB_KV = 4
HEADS_PER_GROUP = NB_HEADS // NB_KV
CMP_LEN = 32
CMP_STRIDE = 16
SLC_LEN = 64
N_SELECT = 16
WINDOW = 512
Q_BLOCK = 128
ROPE_THETA = 10000.0
D_FF = ((8 * D_MODEL // 3 + 127) // 128) * 128
CONV_W = 3
NEG = -1e30
BIG = 1e30
REMOVED = -3.0e38
LANES = 128
SLC_KV_TILE = 512
WIN_SPAN = WINDOW + Q_BLOCK
GLA_ROWS = 256
GLA_HEADS = 4
PAGE = 128
SUBS_PER_PAGE = PAGE // CMP_STRIDE
GW = NB_KV * NB_DH
VMEM_LIMIT = 48 * 1024 * 1024
_NT =(((1,), (1,)), ((), ()))
_TN = (((0,), (0,)), ((), ()))


def _cmp_topk_kernel(q_ref, k_ref, v_ref, ov_ref, o_ref, sel_ref, *, q_blk, n_cmp_pad, n_slc, n_sel):
    i = pl.program_id(1)
    pos = i * q_blk + lax.broadcasted_iota(jnp.int32, (q_blk, 1), 0)
    kend = lax.broadcasted_iota(jnp.int32, (1, n_cmp_pad), 1) * CMP_STRIDE + (CMP_LEN - 1)
    mask_c = kend <= pos
    k = k_ref[...]
    v = v_ref[...]
    scale = NB_DH ** -0.5
    psum = jnp.zeros((q_blk, n_cmp_pad), jnp.float32)
    for j in range(HEADS_PER_GROUP):
        qj = q_ref[:, j * NB_DH:(j + 1) * NB_DH].astype(jnp.bfloat16)
        s = lax.dot_general(qj, k, _NT, preferred_element_type=jnp.float32) * scale
        s = jnp.where(mask_c, s, NEG)
        m = jnp.max(s, axis=-1, keepdims=True)
        e = jnp.where(mask_c, jnp.exp(s - m), 0.0)
        p = e / jnp.maximum(jnp.sum(e, axis=-1, keepdims=True), 1e-30)
        o_ref[:, j * NB_DH:(j + 1) * NB_DH] = jnp.dot(p.astype(jnp.bfloat16), v, preferred_element_type=jnp.float32)
        psum = psum + p
    hi = psum.astype(jnp.bfloat16)
    lo = (psum - hi.astype(jnp.float32)).astype(jnp.bfloat16)
    ov = ov_ref[...]
    imp = (jnp.dot(hi, ov, preferred_element_type=jnp.float32)
           + jnp.dot(lo, ov, preferred_element_type=jnp.float32))
    sidx = lax.broadcasted_iota(jnp.int32, (1, n_slc), 1)
    cur = lax.shift_right_logical(pos, 6)
    forced = (sidx == 0) | (sidx == cur) | (sidx == cur - 1)
    future = sidx * SLC_LEN > pos
    score = jnp.where(future, NEG, jnp.where(forced, BIG, imp))
    lane = lax.broadcasted_iota(jnp.int32, (q_blk, n_slc), 1).astype(jnp.float32)
    sel = jnp.zeros((q_blk, n_slc), jnp.float32)
    for _ in range(n_sel):
        m = jnp.max(score, axis=-1, keepdims=True)
        idx = jnp.min(jnp.where(score == m, lane, float(n_slc)), axis=-1, keepdims=True)
        hit = lane == idx
        sel = jnp.where(hit, 1.0, sel)
        score = jnp.where(hit, REMOVED, score)
    sel_ref[...] = sel.astype(sel_ref.dtype)


def _cmp_topk_prompt(q2d, kcb, vcb):
    t = q2d.shape[0]
    n_cmp = kcb.shape[0]
    n_cmp_pad = -(-n_cmp // LANES) * LANES
    n_slc = t // SLC_LEN
    assert SLC_LEN == 64 and n_slc == LANES and t % Q_BLOCK == 0
    n_sel = min(N_SELECT, n_slc)
    padc = ((0, 0), (0, n_cmp_pad - n_cmp), (0, 0))
    k = jnp.pad(jnp.transpose(kcb, (1, 0, 2)), padc).astype(jnp.bfloat16)
    v = jnp.pad(jnp.transpose(vcb, (1, 0, 2)), padc).astype(jnp.bfloat16)
    cstart = jnp.arange(n_cmp_pad) * CMP_STRIDE
    sstart = jnp.arange(n_slc) * SLC_LEN
    overlap = ((cstart[:, None] <= sstart[None, :] + (SLC_LEN - 1))
               & (cstart[:, None] + (CMP_LEN - 1) >= sstart[None, :])
               & (jnp.arange(n_cmp_pad)[:, None] < n_cmp)).astype(jnp.bfloat16)
    gd = HEADS_PER_GROUP * NB_DH
    kern = functools.partial(_cmp_topk_kernel, q_blk=Q_BLOCK, n_cmp_pad=n_cmp_pad, n_slc=n_slc, n_sel=n_sel)
    return pl.pallas_call(
        kern,
        grid=(NB_KV, t // Q_BLOCK),
        in_specs=[
            pl.BlockSpec((Q_BLOCK, gd), lambda g, i: (i, g)),
            pl.BlockSpec((None, n_cmp_pad, NB_DH), lambda g, i: (g, 0, 0)),
            pl.BlockSpec((None, n_cmp_pad, NB_DH), lambda g, i: (g, 0, 0)),
            pl.BlockSpec((n_cmp_pad, n_slc), lambda g, i: (0, 0)),
        ],
        out_specs=[
            pl.BlockSpec((Q_BLOCK, gd), lambda g, i: (i, g)),
            pl.BlockSpec((Q_BLOCK, n_slc), lambda g, i: (i, g)),
        ],
        out_shape=[
            jax.ShapeDtypeStruct((t, NB_HEADS * NB_DH), jnp.float32),
            jax.ShapeDtypeStruct((t, NB_KV * n_slc), jnp.bfloat16),
        ],
        compiler_params=pltpu.CompilerParams(
            dimension_semantics=("arbitrary", "arbitrary"), vmem_limit_bytes=VMEM_LIMIT),
        name="cmp_topk_prompt",
    )(q2d, k, v, overlap)


def _slc_win_kernel(q_ref, ks_ref, vs_ref, kw_ref, vw_ref, sel_ref, os_ref, ow_ref, m_ref, l_ref, acc_ref,
                    *, q_blk, n_slc, kv_tile):
    i = pl.program_id(1)
    rows = HEADS_PER_GROUP * q_blk
    blocks_per_tile = kv_tile // SLC_LEN
    reps = kv_tile // LANES
    scale = NB_DH ** -0.5
    q4 = jnp.concatenate([q_ref[:, j * NB_DH:(j + 1) * NB_DH] for j in range(HEADS_PER_GROUP)], axis=0)
    sel = sel_ref[...]
    pos = i * q_blk + lax.broadcasted_iota(jnp.int32, (q_blk, 1), 0)
    m_ref[...] = jnp.full(m_ref.shape, NEG, jnp.float32)
    l_ref[...] = jnp.zeros(l_ref.shape, jnp.float32)
    acc_ref[...] = jnp.zeros(acc_ref.shape, jnp.float32)
    blk_row = lax.broadcasted_iota(jnp.int32, (n_slc, kv_tile), 0)
    blk_col = lax.shift_right_logical(lax.broadcasted_iota(jnp.int32, (n_slc, kv_tile), 1), 6)
    key_off = lax.broadcasted_iota(jnp.int32, (1, kv_tile), 1)
    n_tiles = lax.div(i * q_blk + (q_blk - 1), kv_tile) + 1

    def body(j, carry):
        start = pl.multiple_of(j * kv_tile, kv_tile)
        kt = ks_ref[pl.ds(start, kv_tile), :]
        vt = vs_ref[pl.ds(start, kv_tile), :]
        s = lax.dot_general(q4, kt, _NT, preferred_element_type=jnp.float32) * scale
        expand = jnp.where(blk_row == blk_col + j * blocks_per_tile, 1.0, 0.0).astype(jnp.bfloat16)
        picked = jnp.dot(sel, expand, preferred_element_type=jnp.float32) > 0.5
        mask = picked & (key_off + start <= pos)
        s = jnp.where(mask[None], s.reshape(HEADS_PER_GROUP, q_blk, kv_tile), NEG).reshape(rows, kv_tile)
        m_prev = m_ref[...]
        m_new = jnp.maximum(m_prev, jnp.max(s, axis=-1, keepdims=True))
        alpha = jnp.exp(m_prev - m_new)
        p = jnp.exp(s - jnp.concatenate([m_new] * reps, axis=1))
        l_ref[...] = alpha * l_ref[...] + jnp.sum(p, axis=-1, keepdims=True)
        acc_ref[...] = alpha * acc_ref[...] + jnp.dot(p.astype(jnp.bfloat16), vt, preferred_element_type=jnp.float32)
        m_ref[...] = m_new
        return carry

    lax.fori_loop(0, n_tiles, body, 0)
    out = acc_ref[...] / jnp.maximum(l_ref[...], 1e-30)
    for j in range(HEADS_PER_GROUP):
        os_ref[:, j * NB_DH:(j + 1) * NB_DH] = out[j * q_blk:(j + 1) * q_blk, :]

    wstart = pl.multiple_of(jnp.maximum(i - WINDOW // q_blk, 0) * q_blk, q_blk)
    kt = kw_ref[pl.ds(wstart, WIN_SPAN), :]
    vt = vw_ref[pl.ds(wstart, WIN_SPAN), :]
    s = lax.dot_general(q4, kt, _NT, preferred_element_type=jnp.float32) * scale
    diff = pos - (wstart + lax.broadcasted_iota(jnp.int32, (1, WIN_SPAN), 1))
    mask = ((diff >= 0) & (diff <= WINDOW))[None]
    s = jnp.where(mask, s.reshape(HEADS_PER_GROUP, q_blk, WIN_SPAN), NEG)
    n_ghost = jnp.maximum(WINDOW - pos, 0).astype(jnp.float32)[None]
    m = jnp.max(s, axis=-1, keepdims=True)
    m = jnp.where(n_ghost > 0, jnp.maximum(m, 0.0), m)
    e = jnp.where(mask, jnp.exp(s - m), 0.0)
    p = e / jnp.maximum(jnp.sum(e, axis=-1, keepdims=True) + n_ghost * jnp.exp(-m), 1e-30)
    ow = jnp.dot(p.reshape(rows, WIN_SPAN).astype(jnp.bfloat16), vt, preferred_element_type=jnp.float32)
    for j in range(HEADS_PER_GROUP):
        ow_ref[:, j * NB_DH:(j + 1) * NB_DH] = ow[j * q_blk:(j + 1) * q_blk, :]


def _slc_win_prompt(qr2d, kpad, vpad, kwin, vwin, sel):
    t = qr2d.shape[0]
    n_slc = t // SLC_LEN
    assert t % SLC_KV_TILE == 0 and SLC_KV_TILE % SLC_LEN == 0 and t >= WIN_SPAN

    def group_major(a):
        return jnp.transpose(a, (1, 0, 2)).astype(jnp.bfloat16)

    gd = HEADS_PER_GROUP * NB_DH
    rows = HEADS_PER_GROUP * Q_BLOCK
    kern = functools.partial(_slc_win_kernel, q_blk=Q_BLOCK, n_slc=n_slc, kv_tile=SLC_KV_TILE)
    kv_spec = pl.BlockSpec((None, t, NB_DH), lambda g, i: (g, 0, 0))
    o_spec = pl.BlockSpec((Q_BLOCK, gd), lambda g, i: (i, g))
    return pl.pallas_call(
        kern,
        grid=(NB_KV, t // Q_BLOCK),
        in_specs=[o_spec, kv_spec, kv_spec, kv_spec, kv_spec, pl.BlockSpec((Q_BLOCK, n_slc), lambda g, i: (i, g))],
        out_specs=[o_spec, o_spec],
        out_shape=[jax.ShapeDtypeStruct((t, NB_HEADS * NB_DH), jnp.float32)] * 2,
        scratch_shapes=[
            pltpu.VMEM((rows, LANES), jnp.float32),
            pltpu.VMEM((rows, LANES), jnp.float32),
            pltpu.VMEM((rows, NB_DH), jnp.float32),
        ],
        compiler_params=pltpu.CompilerParams(
            dimension_semantics=("arbitrary", "arbitrary"), vmem_limit_bytes=VMEM_LIMIT),
        name="slc_win_prompt",
    )(qr2d.astype(jnp.bfloat16), group_major(kpad), group_major(vpad), group_major(kwin), group_major(vwin), sel)


def _split3(x):
    hi = x.astype(jnp.bfloat16)
    r = x - hi.astype(jnp.float32)
    mid = r.astype(jnp.bfloat16)
    lo = (r - mid.astype(jnp.float32)).astype(jnp.bfloat16)
    return hi, mid, lo


def _gla_kernel(q_ref, k_ref, v_ref, g_ref, o_ref, sfin_ref, st_ref, *, rows, heads, chunk):
    i = pl.program_id(1)

    @pl.when(i == 0)
    def _():
        st_ref[...] = jnp.zeros(st_ref.shape, jnp.float32)

    r_idx = lax.broadcasted_iota(jnp.int32, (rows, rows), 0)
    c_idx = lax.broadcasted_iota(jnp.int32, (rows, rows), 1)
    same = lax.shift_right_logical(r_idx, 5) == lax.shift_right_logical(c_idx, 5)
    causal = same & (c_idx <= r_idx)
    tri = jnp.where(causal, 1.0, 0.0).astype(jnp.bfloat16)
    blk = jnp.where(same, 1.0, 0.0).astype(jnp.bfloat16)
    f32 = jnp.float32
    for hh in range(heads):
        sl = slice(hh * HA_DK, (hh + 1) * HA_DK)
        qh, kh, vh, gh = q_ref[:, sl], k_ref[:, sl], v_ref[:, sl], g_ref[:, sl]
        parts = _split3(gh)
        bcum = sum(jnp.dot(tri, p, preferred_element_type=f32) for p in parts)
        tot = sum(jnp.dot(blk, p, preferred_element_type=f32) for p in parts)
        q_in = (qh * jnp.exp(bcum)).astype(jnp.bfloat16)
        k_in = (kh * jnp.exp(-bcum)).astype(jnp.bfloat16)
        k_dec = (kh * jnp.exp(tot - bcum)).astype(jnp.bfloat16)
        vb = vh.astype(jnp.bfloat16)
        att = lax.dot_general(q_in, k_in, _NT, preferred_element_type=f32)
        att = jnp.where(causal, att, 0.0).astype(jnp.bfloat16)
        o_intra = jnp.dot(att, vb, preferred_element_type=f32)
        st = st_ref[hh]
        for c in range(rows // chunk):
            rs = slice(c * chunk, (c + 1) * chunk)
            o_inter = lax.dot_general(q_in[rs], st.astype(jnp.bfloat16), _NT, preferred_element_type=f32)
            o_ref[rs, sl] = o_intra[rs] + o_inter
            upd = lax.dot_general(vb[rs], k_dec[rs], _TN, preferred_element_type=f32)
            st = jnp.exp(tot[c * chunk:c * chunk + 1, :]) * st + upd
        st_ref[hh] = st

    @pl.when(i == pl.num_programs(1) - 1)
    def _():
        for hh in range(heads):
            sfin_ref[hh] = st_ref[hh].T


def _gla_prompt(q, k, v, logf):
    t = q.shape[0]
    assert t % GLA_ROWS == 0 and GLA_ROWS % HA_CHUNK == 0 and HA_CHUNK == 32
    assert HA_HEADS % GLA_HEADS == 0 and HA_DK == HA_DV
    w = GLA_HEADS * HA_DK
    spec = pl.BlockSpec((GLA_ROWS, w), lambda h, i: (i, h))
    kern = functools.partial(_gla_kernel, rows=GLA_ROWS, heads=GLA_HEADS, chunk=HA_CHUNK)
    return pl.pallas_call(
        kern,
        grid=(HA_HEADS // GLA_HEADS, t // GLA_ROWS),
        in_specs=[spec] * 4,
        out_specs=[spec, pl.BlockSpec((GLA_HEADS, HA_DK, HA_DV), lambda h, i: (h, 0, 0))],
        out_shape=[jax.ShapeDtypeStruct((t, HA_HEADS * HA_DV), jnp.float32),
                   jax.ShapeDtypeStruct((HA_HEADS, HA_DK, HA_DV), jnp.float32)],
        scratch_shapes=[pltpu.VMEM((GLA_HEADS, HA_DV, HA_DK), jnp.float32)],
        compiler_params=pltpu.CompilerParams(
            dimension_semantics=("arbitrary", "arbitrary"), vmem_limit_bytes=VMEM_LIMIT),
        name="gla_prompt",
    )(q, k, v, logf)


def _cmp_pages_kernel(pt_ref, *refs, n_pages):
    kp = refs[:n_pages]
    vp = refs[n_pages:2 * n_pages]
    (w1k_ref, w1v_ref, pek_ref, pev_ref, w2k_ref, w2v_ref, gain_ref, ko_ref, vo_ref, x_ref) = refs[2 * n_pages:]
    n_sub = n_pages * SUBS_PER_PAGE
    f32 = jnp.float32
    row = lax.broadcasted_iota(jnp.int32, (NB_KV * n_sub, 1), 0)
    last = lax.rem(row, n_sub) == n_sub - 1
    for pages, w1_ref, pe_ref, w2_ref, o_ref, norm in ((kp, w1k_ref, pek_ref, w2k_ref, ko_ref, True),
                                                       (vp, w1v_ref, pev_ref, w2v_ref, vo_ref, False)):
        for g in range(NB_KV):
            for p in range(0, n_pages, 2):
                for s in range(CMP_STRIDE):
                    tok = pl.ds(s * NB_KV + g, SUBS_PER_PAGE, stride=CMP_STRIDE * NB_KV)
                    a = pages[p][tok, :]
                    b = pages[p + 1][tok, :]
                    x_ref[pl.ds(g * n_sub + p * SUBS_PER_PAGE, 2 * SUBS_PER_PAGE), s * NB_DH:(s + 1) * NB_DH] = (
                        jnp.concatenate([a, b], axis=0).astype(jnp.bfloat16))
        w1 = w1_ref[...]
        a01 = jnp.dot(x_ref[...], w1, preferred_element_type=f32)
        c = jnp.dot(pe_ref[...].astype(jnp.bfloat16), w1, preferred_element_type=f32)
        a0 = a01[:, :NB_DH] + c[0:1, :NB_DH]
        a1 = a01[:, NB_DH:] + c[1:2, NB_DH:]
        pre = a0 + pltpu.roll(a1, NB_KV * n_sub - 1, 0)
        y = jnp.dot(jax.nn.gelu(pre).astype(jnp.bfloat16), w2_ref[...], preferred_element_type=f32)
        if norm:
            y = y * lax.rsqrt(jnp.mean(y * y, axis=-1, keepdims=True) + EPS) * gain_ref[...]
        o_ref[...] = jnp.where(last, 0.0, y).astype(o_ref.dtype)


def _cmp_pages(page_table, cache_k, cache_v, pe_k, pe_v, w1_k, w1_v, w2_k, w2_v, gain):
    nb, n_pages = page_table.shape
    n_phys = cache_k.shape[0]
    assert cache_k.shape[1:] == (PAGE, NB_KV, NB_DH) and n_pages % 2 == 0 and CMP_LEN == 2 * CMP_STRIDE
    n_sub = n_pages * SUBS_PER_PAGE
    kd = CMP_STRIDE * NB_DH

    def w1cat(w1):
        return jnp.concatenate([w1[:CMP_STRIDE].reshape(kd, NB_DH), w1[CMP_STRIDE:].reshape(kd, NB_DH)],
                               axis=1).astype(jnp.bfloat16)

    def pecat(pe):
        z = jnp.zeros((8, kd), jnp.float32)
        return z.at[0].set(pe[:CMP_STRIDE].reshape(kd)).at[1].set(pe[CMP_STRIDE:].reshape(kd))

    page_specs = [pl.BlockSpec((PAGE * NB_KV, NB_DH), functools.partial(lambda b, pt, p: (pt[b, p], 0), p=p))
                  for p in range(n_pages)]

    def full(*shape):
        return pl.BlockSpec(shape, lambda b, pt: (0,) * len(shape))

    out_spec = pl.BlockSpec((None, NB_KV * n_sub, NB_DH), lambda b, pt: (b, 0, 0))
    ck = cache_k.reshape(n_phys * PAGE * NB_KV, NB_DH)
    cv = cache_v.reshape(n_phys * PAGE * NB_KV, NB_DH)
    grid_spec = pltpu.PrefetchScalarGridSpec(
        num_scalar_prefetch=1,
        grid=(nb,),
        in_specs=page_specs + page_specs + [full(kd, 2 * NB_DH), full(kd, 2 * NB_DH), full(8, kd), full(8, kd),
                                           full(NB_DH, NB_DH), full(NB_DH, NB_DH), full(1, NB_DH)],
        out_specs=[out_spec, out_spec],
        scratch_shapes=[pltpu.VMEM((NB_KV * n_sub, kd), jnp.bfloat16)],
    )
    return pl.pallas_call(
        functools.partial(_cmp_pages_kernel, n_pages=n_pages),
        grid_spec=grid_spec,
        out_shape=[jax.ShapeDtypeStruct((nb, NB_KV * n_sub, NB_DH), jnp.bfloat16)] * 2,
        compiler_params=pltpu.CompilerParams(dimension_semantics=("arbitrary",), vmem_limit_bytes=VMEM_LIMIT),
        name="cmp_pages",
    )(page_table, *([ck] * n_pages), *([cv] * n_pages), w1cat(w1_k), w1cat(w1_v), pecat(pe_k), pecat(pe_v),
      w2_k.astype(jnp.bfloat16), w2_v.astype(jnp.bfloat16), gain.reshape(1, NB_DH))


def _bf(x):
    return x.astype(jnp.bfloat16)


def _tail_scores(qf, knew, n_new):
    kb = _bf(knew).astype(jnp.float32)
    return [jnp.sum(qf * kb[i:i + 1, :], axis=-1, keepdims=True) for i in range(n_new)]


def _nsa_sample_kernel(pt_ref, *refs, n_pages, n_new, win_keep):
    ksp = refs[:n_pages]
    vsp = refs[n_pages:2 * n_pages]
    (q_ref, qr_ref, kcb_ref, vcb_ref, ksn_ref, vsn_ref, kwc_ref, vwc_ref, kwn_ref, vwn_ref, ov_ref,
     oc_ref, os_ref, ow_ref) = refs[2 * n_pages:]
    f32 = jnp.float32
    past = n_pages * PAGE
    n_sub = past // CMP_STRIDE
    n_cmp = n_sub - CMP_LEN // CMP_STRIDE + 1
    n_slc = -(-(past + n_new) // SLC_LEN)
    new_blk = past // SLC_LEN
    assert (past % SLC_LEN) + n_new <= SLC_LEN and n_sub == LANES and n_slc <= LANES
    rows = HEADS_PER_GROUP * n_new
    scale = NB_DH ** -0.5
    t_row = lax.rem(lax.broadcasted_iota(jnp.int32, (rows, 1), 0), n_new)
    pos = past + t_row
    pos_t = past + lax.broadcasted_iota(jnp.int32, (n_new, 1), 0)
    lane = lax.broadcasted_iota(jnp.int32, (1, LANES), 1)
    mask_c = (lane * CMP_STRIDE + (CMP_LEN - 1) <= pos) & (lane < n_cmp)
    cur = lax.shift_right_logical(pos_t, 6)
    forced = (lane == 0) | (lane == cur) | (lane == cur - 1)
    future = lane * SLC_LEN > pos_t
    lane_f = lax.broadcasted_iota(jnp.int32, (n_new, LANES), 1).astype(f32)
    expand = jnp.where(lax.broadcasted_iota(jnp.int32, (LANES, past), 0)
                       == lax.shift_right_logical(lax.broadcasted_iota(jnp.int32, (LANES, past), 1), 6),
                       1.0, 0.0).astype(jnp.bfloat16)
    key_w = lax.broadcasted_iota(jnp.int32, (1, win_keep), 1)
    mask_w = key_w >= t_row
    ov = ov_ref[...]
    for g in range(NB_KV):
        gs = slice(g * NB_DH, (g + 1) * NB_DH)
        qg = q_ref[g]
        qrg = qr_ref[g]
        qrf = qrg.astype(f32)
        kc = kcb_ref[g * n_sub:(g + 1) * n_sub, :]
        vc = vcb_ref[g * n_sub:(g + 1) * n_sub, :]
        s = lax.dot_general(qg, kc, _NT, preferred_element_type=f32) * scale
        s = jnp.where(mask_c, s, NEG)
        m = jnp.max(s, axis=-1, keepdims=True)
        e = jnp.where(mask_c, jnp.exp(s - m), 0.0)
        p = e / jnp.maximum(jnp.sum(e, axis=-1, keepdims=True), 1e-30)
        oc_ref[g] = jnp.dot(_bf(p), vc, preferred_element_type=f32)
        psum = p[0:n_new]
        for j in range(1, HEADS_PER_GROUP):
            psum = psum + p[j * n_new:(j + 1) * n_new]
        hi = _bf(psum)
        lo = _bf(psum - hi.astype(f32))
        imp = jnp.dot(hi, ov, preferred_element_type=f32) + jnp.dot(lo, ov, preferred_element_type=f32)
        score = jnp.where(future, NEG, jnp.where(forced, BIG, imp))
        score = jnp.where(lane >= n_slc, REMOVED, score)
        sel = jnp.zeros((n_new, LANES), f32)
        for _ in range(min(N_SELECT, n_slc)):
            mx = jnp.max(score, axis=-1, keepdims=True)
            idx = jnp.min(jnp.where(score == mx, lane_f, float(LANES)), axis=-1, keepdims=True)
            hit = lane_f == idx
            sel = jnp.where(hit, 1.0, sel)
            score = jnp.where(hit, REMOVED, score)
        sel_r = jnp.concatenate([sel] * HEADS_PER_GROUP, axis=0)
        picked = jnp.dot(_bf(sel_r), expand, preferred_element_type=f32) > 0.5
        s_all = jnp.concatenate(
            [lax.dot_general(qrg, _bf(ksp[pg][:, gs]), _NT, preferred_element_type=f32) for pg in range(n_pages)],
            axis=1) * scale
        s_all = jnp.where(picked, s_all, NEG)
        new_ok = [(sel_r[:, new_blk:new_blk + 1] > 0.5) & (t_row >= i) for i in range(n_new)]
        s_new = [jnp.where(ok, sc * scale, NEG) for ok, sc in zip(new_ok, _tail_scores(qrf, ksn_ref[:, gs], n_new))]
        m = jnp.max(s_all, axis=-1, keepdims=True)
        for sc in s_new:
            m = jnp.maximum(m, sc)
        e_all = jnp.where(picked, jnp.exp(s_all - m), 0.0)
        e_new = [jnp.where(ok, jnp.exp(sc - m), 0.0) for ok, sc in zip(new_ok, s_new)]
        den = jnp.sum(e_all, axis=-1, keepdims=True)
        for en in e_new:
            den = den + en
        inv = 1.0 / jnp.maximum(den, 1e-30)
        p_all = _bf(e_all * inv)
        acc = jnp.dot(p_all[:, 0:PAGE], _bf(vsp[0][:, gs]), preferred_element_type=f32)
        for pg in range(1, n_pages):
            acc = acc + jnp.dot(p_all[:, pg * PAGE:(pg + 1) * PAGE], _bf(vsp[pg][:, gs]), preferred_element_type=f32)
        vn = _bf(vsn_ref[:, gs]).astype(f32)
        for i in range(n_new):
            acc = acc + _bf(e_new[i] * inv).astype(f32) * vn[i:i + 1, :]
        os_ref[g] = acc
        s_w = lax.dot_general(qrg, _bf(kwc_ref[:, gs]), _NT, preferred_element_type=f32) * scale
        s_w = jnp.where(mask_w, s_w, NEG)
        w_ok = [t_row >= i for i in range(n_new)]
        s_wn = [jnp.where(ok, sc * scale, NEG) for ok, sc in zip(w_ok, _tail_scores(qrf, kwn_ref[:, gs], n_new))]
        m = jnp.max(s_w, axis=-1, keepdims=True)
        for sc in s_wn:
            m = jnp.maximum(m, sc)
        e_w = jnp.where(mask_w, jnp.exp(s_w - m), 0.0)
        e_wn = [jnp.where(ok, jnp.exp(sc - m), 0.0) for ok, sc in zip(w_ok, s_wn)]
        den = jnp.sum(e_w, axis=-1, keepdims=True)
        for en in e_wn:
            den = den + en
        inv = 1.0 / jnp.maximum(den, 1e-30)
        acc = jnp.dot(_bf(e_w * inv), _bf(vwc_ref[:, gs]), preferred_element_type=f32)
        vn = _bf(vwn_ref[:, gs]).astype(f32)
        for i in range(n_new):
            acc = acc + _bf(e_wn[i] * inv).astype(f32) * vn[i:i + 1, :]
        ow_ref[g] = acc


def _nsa_sample(page_table, q, qr, kcb, vcb, cache_sk, cache_sv, ks_new, vs_new, cache_wk, cache_wv, kw_new, vw_new):
    nb, n_pages = page_table.shape
    t = q.shape[1]
    n_phys = cache_sk.shape[0]
    keep = cache_wk.shape[1]
    assert keep == WINDOW and t <= 8
    rows = HEADS_PER_GROUP * t
    n_sub = n_pages * SUBS_PER_PAGE

    def q_rows(a):
        a = a.reshape(nb, t, NB_KV, HEADS_PER_GROUP, NB_DH)
        return jnp.transpose(a, (0, 2, 3, 1, 4)).reshape(nb, NB_KV, rows, NB_DH).astype(jnp.bfloat16)

    def q_unrows(a):
        a = a.reshape(nb, NB_KV, HEADS_PER_GROUP, t, NB_DH)
        return jnp.transpose(a, (0, 3, 1, 2, 4)).reshape(nb, t, NB_HEADS, NB_DH)

    cstart = jnp.arange(LANES) * CMP_STRIDE
    sstart = jnp.arange(LANES) * SLC_LEN
    n_cmp = n_sub - CMP_LEN // CMP_STRIDE + 1
    overlap = ((cstart[:, None] <= sstart[None, :] + (SLC_LEN - 1))
               & (cstart[:, None] + (CMP_LEN - 1) >= sstart[None, :])
               & (jnp.arange(LANES)[:, None] < n_cmp)).astype(jnp.bfloat16)
    page_specs = [pl.BlockSpec((None, PAGE, GW), functools.partial(lambda b, pt, p: (pt[b, p], 0, 0), p=p))
                  for p in range(n_pages)]

    def per_b(*shape):
        return pl.BlockSpec((None,) + shape, lambda b, pt: (b,) + (0,) * len(shape))

    q_spec = per_b(NB_KV, rows, NB_DH)
    in_specs = (page_specs + page_specs
                + [q_spec, q_spec, per_b(NB_KV * n_sub, NB_DH), per_b(NB_KV * n_sub, NB_DH),
                   per_b(t, GW), per_b(t, GW), per_b(keep, GW), per_b(keep, GW), per_b(t, GW), per_b(t, GW),
                   pl.BlockSpec((LANES, LANES), lambda b, pt: (0, 0))])
    grid_spec = pltpu.PrefetchScalarGridSpec(
        num_scalar_prefetch=1, grid=(nb,), in_specs=in_specs, out_specs=[q_spec] * 3)
    sk = cache_sk.reshape(n_phys, PAGE, GW)
    sv = cache_sv.reshape(n_phys, PAGE, GW)

    def flat(a):
        return a.reshape(a.shape[0], a.shape[1], GW)

    o_c, o_s, o_w = pl.pallas_call(
        functools.partial(_nsa_sample_kernel, n_pages=n_pages, n_new=t, win_keep=keep),
        grid_spec=grid_spec,
        out_shape=[jax.ShapeDtypeStruct((nb, NB_KV, rows, NB_DH), jnp.float32)] * 3,
        compiler_params=pltpu.CompilerParams(dimension_semantics=("arbitrary",), vmem_limit_bytes=VMEM_LIMIT),
        name="nsa_sample",
    )(page_table, *([sk] * n_pages), *([sv] * n_pages), q_rows(q), q_rows(qr), kcb, vcb,
      flat(ks_new), flat(vs_new), flat(cache_wk), flat(cache_wv), flat(kw_new), flat(vw_new), overlap)
    return q_unrows(o_c), q_unrows(o_s), q_unrows(o_w)


def _mm_kernel(x_ref, w_ref, o_ref, wb_ref):
    @pl.when(pl.program_id(1) == 0)
    def _():
        wb_ref[...] = w_ref[...].astype(jnp.bfloat16)

    o_ref[...] = jnp.dot(x_ref[...], wb_ref[...], preferred_element_type=jnp.float32)


def _mm_tiles(m, k, n):
    tm = min(m, 1024 if k <= 2048 else 512)
    tn = min(n, 512 if k <= 2048 else 256)
    return tm, tn


def _mm(x, w):
    lead = x.shape[:-1]
    k = x.shape[-1]
    n = w.shape[1]
    x2 = x.reshape(-1, k).astype(jnp.bfloat16)
    m = x2.shape[0]
    tm, tn = _mm_tiles(m, k, n)
    assert m % tm == 0
    out = pl.pallas_call(
        _mm_kernel,
        grid=(pl.cdiv(n, tn), m // tm),
        in_specs=[pl.BlockSpec((tm, k), lambda j, i: (i, 0)),
                  pl.BlockSpec((k, tn), lambda j, i: (0, j))],
        out_specs=pl.BlockSpec((tm, tn), lambda j, i: (i, j)),
        out_shape=jax.ShapeDtypeStruct((m, n), jnp.float32),
        scratch_shapes=[pltpu.VMEM((k, tn), jnp.bfloat16)],
        compiler_params=pltpu.CompilerParams(
            dimension_semantics=("arbitrary", "arbitrary"), vmem_limit_bytes=VMEM_LIMIT),
        name="mm",
    )(x2, w)
    return out.reshape(*lead, n)


def _rmsnorm(x, g):
    xf = x.astype(jnp.float32)
    y = xf * lax.rsqrt(jnp.mean(xf * xf, axis=-1, keepdims=True) + EPS)
    return (y * g.astype(jnp.float32)).astype(x.dtype)


def _rope(x, pos):
    half = x.shape[-1] // 2
    inv = ROPE_THETA ** (-jnp.arange(half, dtype=jnp.float32) / half)
    ang = pos.astype(jnp.float32)[:, None] * inv[None, :]
    cos, sin = jnp.cos(ang)[:, None, :], jnp.sin(ang)[:, None, :]
    xf = x.astype(jnp.float32)
    x1, x2 = xf[..., :half], xf[..., half:]
    return jnp.concatenate([x1 * cos - x2 * sin, x2 * cos + x1 * sin], axis=-1).astype(x.dtype)


def _masked_softmax(s, mask):
    s = jnp.where(mask, s, NEG)
    m = jnp.max(s, axis=-1, keepdims=True)
    e = jnp.where(mask, jnp.exp(s - m), 0.0)
    return e / jnp.maximum(jnp.sum(e, axis=-1, keepdims=True), 1e-30)


def _gqa_attend(q, k, v, mask):
    bq, t, h, d = q.shape
    g = k.shape[2]
    qg = q.reshape(bq, t, g, h // g, d).astype(jnp.float32)
    s = jnp.einsum('btgjd,bkgd->btgjk', qg, k.astype(jnp.float32)) * (d ** -0.5)
    p = _masked_softmax(s, mask[:, :, None, None, :])
    o = jnp.einsum('btgjk,bkgd->btgjd', p, v.astype(jnp.float32))
    return o.reshape(bq, t, h, d).astype(q.dtype), p


def _gla_chunked(q, k, v, logf, s0):
    b, t, h, dk = q.shape
    dv = v.shape[-1]
    c = math.gcd(t, HA_CHUNK)
    n = t // c

    def to_chunks(a):
        return jnp.transpose(a.reshape(b, n, c, h, a.shape[-1]), (1, 0, 3, 2, 4))

    qc, kc, vc, gc = to_chunks(q), to_chunks(k), to_chunks(v), to_chunks(logf)
    bcum = jnp.cumsum(gc, axis=3)
    blast = bcum[:, :, :, -1, :]
    q_in = qc * jnp.exp(bcum)
    k_in = kc * jnp.exp(-bcum)
    k_dec = kc * jnp.exp(blast[:, :, :, None, :] - bcum)
    causal = jnp.tril(jnp.ones((c, c), dtype=bool))
    att = jnp.where(causal, jnp.einsum('nbhik,nbhjk->nbhij', q_in, k_in), 0.0)
    o_intra = jnp.einsum('nbhij,nbhjv->nbhiv', att, vc)

    def step(s, inp):
        qi, kd, vi, dl = inp
        o_inter = jnp.einsum('bhik,bhkv->bhiv', qi, s)
        s = jnp.exp(dl)[..., None] * s + jnp.einsum('bhik,bhiv->bhkv', kd, vi)
        return s, o_inter

    s_final, o_inter = lax.scan(step, s0.astype(jnp.float32), (q_in, k_dec, vc, blast))
    o = o_intra + o_inter
    return jnp.transpose(o, (1, 0, 3, 2, 4)).reshape(b, t, h, dv), s_final


def _hgrn2_layer(xn, s0, lb, w_in, out_gain, w_out):
    b, t, _ = xn.shape
    hk = HA_HEADS * HA_DK
    hv = HA_HEADS * HA_DV
    proj = _mm(xn, w_in)
    q = jax.nn.silu(proj[..., :hk])
    f = lb + (1.0 - lb) * jax.nn.sigmoid(proj[..., hk:2 * hk])
    i_in = proj[..., 2 * hk:2 * hk + hv]
    gate = proj[..., 2 * hk + hv:]
    shp_k = (b, t, HA_HEADS, HA_DK)
    shp_v = (b, t, HA_HEADS, HA_DV)
    if s0 is None:
        assert b == 1
        o, s_new = _gla_prompt(q.reshape(t, hk), (1.0 - f).reshape(t, hk), i_in.reshape(t, hv),
                               jnp.log(f).reshape(t, hk))
        o, s_new = o.reshape(shp_v), s_new[None]
    else:
        o, s_new = _gla_chunked(q.reshape(shp_k), (1.0 - f).reshape(shp_k), i_in.reshape(shp_v),
                                jnp.log(f).reshape(shp_k), s0)
    o = _rmsnorm(o, out_gain.reshape(HA_HEADS, HA_DV)) * jax.nn.silu(gate).reshape(shp_v)
    return _mm(o.reshape(b, t, hv), w_out), s_new


def _conv_ffn(xn, prev, w_up, conv_w, conv_b, w_down):
    t = xn.shape[1]
    up = _mm(xn, w_up)
    u, gt = up[..., :D_FF], up[..., D_FF:]
    ext = jnp.concatenate([prev.astype(gt.dtype), gt], axis=1)
    conv = conv_b
    for j in range(CONV_W):
        conv = conv + conv_w[j] * ext[:, j:j + t]
    return _mm(jax.nn.silu(conv) * u, w_down), ext[:, t:]


def _ple(h, p, gain, w_p, w_g):
    gate = jax.nn.sigmoid(_mm(_rmsnorm(h, gain), w_g))
    return h + _mm(p, w_p) * gate


def _gather_pages(cache, page_table):
    rows = cache[page_table]
    return rows.reshape(page_table.shape[0], page_table.shape[1] * cache.shape[1], cache.shape[2], cache.shape[3])


def _compress(rows, pe, w1, w2):
    b, l, g, d = rows.shape
    n_sub = l // CMP_STRIDE
    r = CMP_LEN // CMP_STRIDE
    n_cmp = n_sub - r + 1
    sub = rows[:, :n_sub * CMP_STRIDE].reshape(b, n_sub, CMP_STRIDE, g, d)
    pre = None
    for j in range(r):
        sl = slice(j * CMP_STRIDE, (j + 1) * CMP_STRIDE)
        part = jnp.einsum('bnsgd,sde->bnge', sub + pe[sl][None, None, :, None, :], w1[sl])[:, j:j + n_cmp]
        pre = part if pre is None else pre + part
    return jnp.einsum('bnge,ef->bngf', jax.nn.gelu(pre), w2)


def _cmp_slc_attend_sample(q, qr, qpos, kcb, vcb, kpad, vpad, q_blk):
    b, t, h, d = q.shape
    g = kcb.shape[2]
    n_cmp = kcb.shape[1]
    n_slc = kpad.shape[1] // SLC_LEN
    n_sel = min(N_SELECT, n_slc)
    cstart = jnp.arange(n_cmp) * CMP_STRIDE
    kend = cstart + (CMP_LEN - 1)
    sstart = jnp.arange(n_slc) * SLC_LEN
    overlap = ((cstart[:, None] <= sstart[None, :] + (SLC_LEN - 1)) & (kend[:, None] >= sstart[None, :])).astype(jnp.float32)
    sidx = jnp.arange(n_slc)
    offs = jnp.arange(SLC_LEN)
    bi = jnp.arange(b)[:, None, None, None]
    gi = jnp.arange(g)[None, None, :, None]

    def block(args):
        qc_, qr_, pb = args
        mask_c = (kend[None, :] <= pb[:, None])[None]
        o_c, p_c = _gqa_attend(qc_, kcb, vcb, mask_c)
        imp = jnp.einsum('bqgjn,ns->bqgs', p_c, overlap)
        cur = pb // SLC_LEN
        forced = (sidx[None, :] == 0) | (sidx[None, :] == cur[:, None]) | (sidx[None, :] == cur[:, None] - 1)
        future = sstart[None, :] > pb[:, None]
        score = jnp.where(future[None, :, None, :], NEG, jnp.where(forced[None, :, None, :], BIG, imp))
        _, sel = lax.top_k(score, n_sel)
        tok = (sel[..., None] * SLC_LEN + offs).reshape(b, q_blk, g, n_sel * SLC_LEN)
        ks = kpad[bi, tok, gi].astype(jnp.float32)
        vs = vpad[bi, tok, gi].astype(jnp.float32)
        qg = qr_.reshape(b, q_blk, g, h // g, d).astype(jnp.float32)
        s = jnp.einsum('bqgjd,bqgnd->bqgjn', qg, ks) * (d ** -0.5)
        p_s = _masked_softmax(s, (tok <= pb[None, :, None, None])[:, :, :, None, :])
        o_s = jnp.einsum('bqgjn,bqgnd->bqgjd', p_s, vs).reshape(b, q_blk, h, d)
        return o_c, o_s.astype(q.dtype)

    nq = t // q_blk
    qs = jnp.moveaxis(q.reshape(b, nq, q_blk, h, d), 1, 0)
    qrs = jnp.moveaxis(qr.reshape(b, nq, q_blk, h, d), 1, 0)
    ps = qpos.reshape(nq, q_blk)
    o_c, o_s = lax.map(block, (qs, qrs, ps))
    return (jnp.moveaxis(o_c, 0, 1).reshape(b, t, h, d), jnp.moveaxis(o_s, 0, 1).reshape(b, t, h, d))


def _nsa_layer(xn, pos, shared, w_in, q_gain, w_out, banded, q_blk):
    b, t, _ = xn.shape
    hd = NB_HEADS * NB_DH
    proj = _mm(xn, w_in)
    q = _rmsnorm(proj[..., :hd].reshape(b, t, NB_HEADS, NB_DH), q_gain)
    gates = jax.nn.sigmoid(proj[..., hd:].astype(jnp.float32)).reshape(b, t, 3, NB_HEADS)
    qr = _rope(q, pos)
    if banded:
        assert b == 1
        kcb, vcb, kpad, vpad, win_k, win_v = shared
        o_c2, sel = _cmp_topk_prompt(q.reshape(t, hd), kcb[0], vcb[0])
        o_s2, o_w2 = _slc_win_prompt(qr.reshape(t, hd), kpad[0], vpad[0], win_k[0], win_v[0], sel)
        o_c = o_c2.reshape(b, t, NB_HEADS, NB_DH)
        o_s = o_s2.reshape(b, t, NB_HEADS, NB_DH)
        o_w = o_w2.reshape(b, t, NB_HEADS, NB_DH)
    else:
        o_c, o_s, o_w = _nsa_sample(shared[0], q, qr, *shared[1:])
    o = (gates[:, :, 0, :, None] * o_c + gates[:, :, 1, :, None] * o_s + gates[:, :, 2, :, None] * o_w)
    return _mm(o.reshape(b, t, hd), w_out)


def _shared_kv(h, pos, start, past, wt):
    b, t, _ = h.shape
    kv = _mm(_rmsnorm(h, wt['kv_norm']), wt['kv_w']).reshape(b, t, 6, NB_KV, NB_DH)
    k_cmp, v_cmp = kv[:, :, 0], kv[:, :, 1]
    k_slc = _rope(_rmsnorm(kv[:, :, 2], wt['k_norm_slc']), pos)
    v_slc = kv[:, :, 3]
    k_win = _rope(_rmsnorm(kv[:, :, 4], wt['k_norm_win']), pos)
    v_win = kv[:, :, 5]
    if past is None:
        keep = min(WINDOW, t)
        kcb = _rmsnorm(_compress(k_cmp, wt['cmp_pe_k'], wt['cmp_w1_k'], wt['cmp_w2_k']), wt['k_norm_cmp'])
        vcb = _compress(v_cmp, wt['cmp_pe_v'], wt['cmp_w1_v'], wt['cmp_w2_v'])
        assert t % SLC_LEN == 0
        shared = (kcb, vcb, k_slc, v_slc, k_win, v_win)
        rows = (k_cmp, v_cmp, k_slc, v_slc, k_win[:, -keep:], v_win[:, -keep:])
    else:
        page_table, ckc, cvc, cks, cvs, cwk, cwv = past
        assert t < CMP_STRIDE and (page_table.shape[1] * PAGE) % CMP_STRIDE == 0
        kcb, vcb = _cmp_pages(page_table, ckc, cvc, wt['cmp_pe_k'], wt['cmp_pe_v'], wt['cmp_w1_k'], wt['cmp_w1_v'],
                              wt['cmp_w2_k'], wt['cmp_w2_v'], wt['k_norm_cmp'])
        keep = cwk.shape[1]
        shared = (page_table, kcb, vcb, cks, cvs, k_slc, v_slc, cwk, cwv, k_win, v_win)
        rows = (k_cmp, v_cmp, k_slc, v_slc,
                jnp.concatenate([cwk.astype(k_win.dtype), k_win], axis=1)[:, -keep:],
                jnp.concatenate([cwv.astype(v_win.dtype), v_win], axis=1)[:, -keep:])
    return shared, rows


def _trunk(x, p, start, s_hgrn, s_conv, past, wt):
    b, t, _ = x.shape
    pos = start + jnp.arange(t)
    banded = past is None
    q_blk = math.gcd(t, Q_BLOCK) if banded else 1
    lb = jnp.cumsum(jax.nn.softmax(wt['a_lb_logits'].astype(jnp.float32), axis=0), axis=0)
    h = x
    hgrn_out, conv_out = [], []
    shared, rows = None, None
    for i in range(DEPTH):
        hn = _rmsnorm(h, wt['norm_mix'][i])
        if i < N_A:
            y, s = _hgrn2_layer(hn, None if s_hgrn is None else s_hgrn[i], lb[i], wt['a_w_in'][i],
                                wt['a_out_norm'][i], wt['a_w_out'][i])
            hgrn_out.append(s)
        else:
            j = i - N_A
            y = _nsa_layer(hn, pos, shared, wt['b_w_in'][j], wt['b_q_norm'][j], wt['b_w_out'][j], banded, q_blk)
        h = h + y
        f, cs = _conv_ffn(_rmsnorm(h, wt['norm_ffn'][i]), s_conv[i], wt['ffn_w_up'][i], wt['ffn_conv_w'][i],
                          wt['ffn_conv_b'][i], wt['ffn_w_down'][i])
        conv_out.append(cs)
        h = h + f
        h = _ple(h, p[i], wt['norm_ple'][i], wt['ple_w_p'][i], wt['ple_w_g'][i])
        if i == N_A - 1:
            shared, rows = _shared_kv(h, pos, start, past, wt)
    return h, jnp.stack(hgrn_out), jnp.stack(conv_out), rows


def kernel(x_prompt, x_sample, state_hgrn, state_conv, cache_cmp_k, cache_cmp_v, cache_slc_k, cache_slc_v,
           cache_win_k, cache_win_v, page_table, p_prompt, p_sample, norm_mix, norm_ffn, norm_ple,
           a_w_in, a_lb_logits, a_out_norm, a_w_out, kv_norm, kv_w, k_norm_cmp, k_norm_slc, k_norm_win,
           cmp_pe_k, cmp_pe_v, cmp_w1_k, cmp_w2_k, cmp_w1_v, cmp_w2_v, b_w_in, b_q_norm, b_w_out,
           ffn_w_up, ffn_conv_w, ffn_conv_b, ffn_w_down, ple_w_p, ple_w_g):
    wt = {
        'norm_mix': norm_mix, 'norm_ffn': norm_ffn, 'norm_ple': norm_ple,
        'a_w_in': a_w_in, 'a_lb_logits': a_lb_logits, 'a_out_norm': a_out_norm, 'a_w_out': a_w_out,
        'kv_norm': kv_norm, 'kv_w': kv_w, 'k_norm_cmp': k_norm_cmp, 'k_norm_slc': k_norm_slc,
        'k_norm_win': k_norm_win, 'cmp_pe_k': cmp_pe_k, 'cmp_pe_v': cmp_pe_v, 'cmp_w1_k': cmp_w1_k,
        'cmp_w2_k': cmp_w2_k, 'cmp_w1_v': cmp_w1_v, 'cmp_w2_v': cmp_w2_v,
        'b_w_in': b_w_in, 'b_q_norm': b_q_norm, 'b_w_out': b_w_out,
        'ffn_w_up': ffn_w_up, 'ffn_conv_w': ffn_conv_w, 'ffn_conv_b': ffn_conv_b, 'ffn_w_down': ffn_w_down,
        'ple_w_p': ple_w_p, 'ple_w_g': ple_w_g,
    }
    bp, past_len = x_prompt.shape[0], page_table.shape[1] * cache_cmp_k.shape[1]
    s_conv0 = jnp.zeros((DEPTH, bp, CONV_W - 1, D_FF), x_prompt.dtype)
    y_prompt, hgrn_p, conv_p, rows_p = _trunk(x_prompt, p_prompt, 0, None, s_conv0, None, wt)
    past = (page_table, cache_cmp_k, cache_cmp_v, cache_slc_k, cache_slc_v, cache_win_k, cache_win_v)
    y_sample, hgrn_s, conv_s, rows_s = _trunk(x_sample, p_sample, past_len, state_hgrn, state_conv, past, wt)
    kc_p, vc_p, ks_p, vs_p, wk_p, wv_p = rows_p
    kc_s, vc_s, ks_s, vs_s, wk_s, wv_s = rows_s
    return (y_prompt, y_sample, hgrn_p, hgrn_s, conv_p, conv_s, kc_p, vc_p, ks_p, vs_p,
            kc_s, vc_s, ks_s, vs_s, wk_p, wv_p, wk_s, wv_s)
```

```python
import functools
import math

import jax
import jax.numpy as jnp
from jax import lax
from jax.experimental import pallas as pl
from jax.experimental.pallas import tpu as pltpu

D_MODEL = 2048
DEPTH = 2
N_A = DEPTH // 2
N_B = DEPTH - N_A
EPS = 1e-6
HA_DK = 128
HA_HEADS = D_MODEL // HA_DK
HA_DV = D_MODEL // HA_HEADS
HA_CHUNK = 32
NB_DH = 128
NB_HEADS = D_MODEL // NB_DH
NB_KV = 4
HEADS_PER_GROUP = NB_HEADS // NB_KV
CMP_LEN = 32
CMP_STRIDE = 16
SLC_LEN = 64
N_SELECT = 16
WINDOW = 512
Q_BLOCK = 128
ROPE_THETA = 10000.0
D_FF = ((8 * D_MODEL // 3 + 127) // 128) * 128
CONV_W = 3
NEG = -1e30
BIG = 1e30
REMOVED = -3.0e38
LANES = 128
SLC_KV_TILE = 512
WIN_SPAN = WINDOW + Q_BLOCK
GLA_ROWS = 256
GLA_HEADS = 4
PAGE = 128
SUBS_PER_PAGE = PAGE // CMP_STRIDE
SUB_ROWS = CMP_STRIDE * NB_KV
SUB_PITCH = SUB_ROWS + 8
VMEM_LIMIT = 48 * 1024 * 1024
_NT = (((1,), (1,)), ((), ()))
_TN = (((0,), (0,)), ((), ()))


def _bf(x):
    return x.astype(jnp.bfloat16)


def _cmp_topk_kernel(q_ref, k_ref, v_ref, ov_ref, o_ref, sel_ref, *, q_blk, n_cmp_pad, n_slc, n_sel):
    i = pl.program_id(1)
    pos = i * q_blk + lax.broadcasted_iota(jnp.int32, (q_blk, 1), 0)
    kend = lax.broadcasted_iota(jnp.int32, (1, n_cmp_pad), 1) * CMP_STRIDE + (CMP_LEN - 1)
    mask_c = kend <= pos
    k = k_ref[...]
    v = v_ref[...]
    scale = NB_DH ** -0.5
    psum = jnp.zeros((q_blk, n_cmp_pad), jnp.float32)
    for j in range(HEADS_PER_GROUP):
        qj = q_ref[:, j * NB_DH:(j + 1) * NB_DH].astype(jnp.bfloat16)
        s = lax.dot_general(qj, k, _NT, preferred_element_type=jnp.float32) * scale
        s = jnp.where(mask_c, s, NEG)
        m = jnp.max(s, axis=-1, keepdims=True)
        e = jnp.where(mask_c, jnp.exp(s - m), 0.0)
        p = e / jnp.maximum(jnp.sum(e, axis=-1, keepdims=True), 1e-30)
        o_ref[:, j * NB_DH:(j + 1) * NB_DH] = jnp.dot(p.astype(jnp.bfloat16), v, preferred_element_type=jnp.float32)
        psum = psum + p
    hi = psum.astype(jnp.bfloat16)
    lo = (psum - hi.astype(jnp.float32)).astype(jnp.bfloat16)
    ov = ov_ref[...]
    imp = (jnp.dot(hi, ov, preferred_element_type=jnp.float32)
           + jnp.dot(lo, ov, preferred_element_type=jnp.float32))
    sidx = lax.broadcasted_iota(jnp.int32, (1, n_slc), 1)
    cur = lax.shift_right_logical(pos, 6)
    forced = (sidx == 0) | (sidx == cur) | (sidx == cur - 1)
    future = sidx * SLC_LEN > pos
    score = jnp.where(future, NEG, jnp.where(forced, BIG, imp))
    lane = lax.broadcasted_iota(jnp.int32, (q_blk, n_slc), 1).astype(jnp.float32)
    sel = jnp.zeros((q_blk, n_slc), jnp.float32)
    for _ in range(n_sel):
        m = jnp.max(score, axis=-1, keepdims=True)
        idx = jnp.min(jnp.where(score == m, lane, float(n_slc)), axis=-1, keepdims=True)
        hit = lane == idx
        sel = jnp.where(hit, 1.0, sel)
        score = jnp.where(hit, REMOVED, score)
    sel_ref[...] = sel.astype(sel_ref.dtype)


def _overlap_matrix(n_cmp_rows, n_cmp, n_slc_cols):
    cstart = jnp.arange(n_cmp_rows) * CMP_STRIDE
    sstart = jnp.arange(n_slc_cols) * SLC_LEN
    return _bf((cstart[:, None] <= sstart[None, :] + (SLC_LEN - 1))
               & (cstart[:, None] + (CMP_LEN - 1) >= sstart[None, :])
               & (jnp.arange(n_cmp_rows)[:, None] < n_cmp))


def _cmp_topk_prompt(q2d, kcb, vcb):
    t = q2d.shape[0]
    n_cmp = kcb.shape[0]
    n_cmp_pad = -(-n_cmp // LANES) * LANES
    n_slc = t // SLC_LEN
    assert SLC_LEN == 64 and n_slc == LANES and t % Q_BLOCK == 0
    n_sel = min(N_SELECT, n_slc)
    padc = ((0, 0), (0, n_cmp_pad - n_cmp), (0, 0))
    k = _bf(jnp.pad(jnp.transpose(kcb, (1, 0, 2)), padc))
    v = _bf(jnp.pad(jnp.transpose(vcb, (1, 0, 2)), padc))
    gd = HEADS_PER_GROUP * NB_DH
    kern = functools.partial(_cmp_topk_kernel, q_blk=Q_BLOCK, n_cmp_pad=n_cmp_pad, n_slc=n_slc, n_sel=n_sel)
    return pl.pallas_call(
        kern,
        grid=(NB_KV, t // Q_BLOCK),
        in_specs=[
            pl.BlockSpec((Q_BLOCK, gd), lambda g, i: (i, g)),
            pl.BlockSpec((None, n_cmp_pad, NB_DH), lambda g, i: (g, 0, 0)),
            pl.BlockSpec((None, n_cmp_pad, NB_DH), lambda g, i: (g, 0, 0)),
            pl.BlockSpec((n_cmp_pad, n_slc), lambda g, i: (0, 0)),
        ],
        out_specs=[
            pl.BlockSpec((Q_BLOCK, gd), lambda g, i: (i, g)),
            pl.BlockSpec((Q_BLOCK, n_slc), lambda g, i: (i, g)),
        ],
        out_shape=[
            jax.ShapeDtypeStruct((t, NB_HEADS * NB_DH), jnp.float32),
            jax.ShapeDtypeStruct((t, NB_KV * n_slc), jnp.bfloat16),
        ],
        compiler_params=pltpu.CompilerParams(
            dimension_semantics=("arbitrary", "arbitrary"), vmem_limit_bytes=VMEM_LIMIT),
        name="cmp_topk_prompt",
    )(q2d, k, v, _overlap_matrix(n_cmp_pad, n_cmp, n_slc))


def _slc_win_kernel(q_ref, ks_ref, vs_ref, kw_ref, vw_ref, sel_ref, os_ref, ow_ref, m_ref, l_ref, acc_ref,
                    *, q_blk, n_slc, kv_tile):
    i = pl.program_id(1)
    rows = HEADS_PER_GROUP * q_blk
    blocks_per_tile = kv_tile // SLC_LEN
    reps = kv_tile // LANES
    scale = NB_DH ** -0.5
    q4 = jnp.concatenate([q_ref[:, j * NB_DH:(j + 1) * NB_DH] for j in range(HEADS_PER_GROUP)], axis=0)
    sel = sel_ref[...]
    pos = i * q_blk + lax.broadcasted_iota(jnp.int32, (q_blk, 1), 0)
    m_ref[...] = jnp.full(m_ref.shape, NEG, jnp.float32)
    l_ref[...] = jnp.zeros(l_ref.shape, jnp.float32)
    acc_ref[...] = jnp.zeros(acc_ref.shape, jnp.float32)
    blk_row = lax.broadcasted_iota(jnp.int32, (n_slc, kv_tile), 0)
    blk_col = lax.shift_right_logical(lax.broadcasted_iota(jnp.int32, (n_slc, kv_tile), 1), 6)
    key_off = lax.broadcasted_iota(jnp.int32, (1, kv_tile), 1)
    n_tiles = lax.div(i * q_blk + (q_blk - 1), kv_tile) + 1

    def body(j, carry):
        start = pl.multiple_of(j * kv_tile, kv_tile)
        kt = ks_ref[pl.ds(start, kv_tile), :]
        vt = vs_ref[pl.ds(start, kv_tile), :]
        s = lax.dot_general(q4, kt, _NT, preferred_element_type=jnp.float32) * scale
        expand = jnp.where(blk_row == blk_col + j * blocks_per_tile, 1.0, 0.0).astype(jnp.bfloat16)
        picked = jnp.dot(sel, expand, preferred_element_type=jnp.float32) > 0.5
        mask = picked & (key_off + start <= pos)
        s = jnp.where(mask[None], s.reshape(HEADS_PER_GROUP, q_blk, kv_tile), NEG).reshape(rows, kv_tile)
        m_prev = m_ref[...]
        m_new = jnp.maximum(m_prev, jnp.max(s, axis=-1, keepdims=True))
        alpha = jnp.exp(m_prev - m_new)
        p = jnp.exp(s - jnp.concatenate([m_new] * reps, axis=1))
        l_ref[...] = alpha * l_ref[...] + jnp.sum(p, axis=-1, keepdims=True)
        acc_ref[...] = alpha * acc_ref[...] + jnp.dot(p.astype(jnp.bfloat16), vt, preferred_element_type=jnp.float32)
        m_ref[...] = m_new
        return carry

    lax.fori_loop(0, n_tiles, body, 0)
    out = acc_ref[...] / jnp.maximum(l_ref[...], 1e-30)
    for j in range(HEADS_PER_GROUP):
        os_ref[:, j * NB_DH:(j + 1) * NB_DH] = out[j * q_blk:(j + 1) * q_blk, :]

    wstart = pl.multiple_of(jnp.maximum(i - WINDOW // q_blk, 0) * q_blk, q_blk)
    kt = kw_ref[pl.ds(wstart, WIN_SPAN), :]
    vt = vw_ref[pl.ds(wstart, WIN_SPAN), :]
    s = lax.dot_general(q4, kt, _NT, preferred_element_type=jnp.float32) * scale
    diff = pos - (wstart + lax.broadcasted_iota(jnp.int32, (1, WIN_SPAN), 1))
    mask = ((diff >= 0) & (diff <= WINDOW))[None]
    s = jnp.where(mask, s.reshape(HEADS_PER_GROUP, q_blk, WIN_SPAN), NEG)
    n_ghost = jnp.maximum(WINDOW - pos, 0).astype(jnp.float32)[None]
    m = jnp.max(s, axis=-1, keepdims=True)
    m = jnp.where(n_ghost > 0, jnp.maximum(m, 0.0), m)
    e = jnp.where(mask, jnp.exp(s - m), 0.0)
    p = e / jnp.maximum(jnp.sum(e, axis=-1, keepdims=True) + n_ghost * jnp.exp(-m), 1e-30)
    ow = jnp.dot(p.reshape(rows, WIN_SPAN).astype(jnp.bfloat16), vt, preferred_element_type=jnp.float32)
    for j in range(HEADS_PER_GROUP):
        ow_ref[:, j * NB_DH:(j + 1) * NB_DH] = ow[j * q_blk:(j + 1) * q_blk, :]


def _slc_win_prompt(qr2d, kpad, vpad, kwin, vwin, sel):
    t = qr2d.shape[0]
    n_slc = t // SLC_LEN
    assert t % SLC_KV_TILE == 0 and SLC_KV_TILE % SLC_LEN == 0 and t >= WIN_SPAN

    def group_major(a):
        return _bf(jnp.transpose(a, (1, 0, 2)))

    gd = HEADS_PER_GROUP * NB_DH
    rows = HEADS_PER_GROUP * Q_BLOCK
    kern = functools.partial(_slc_win_kernel, q_blk=Q_BLOCK, n_slc=n_slc, kv_tile=SLC_KV_TILE)
    kv_spec = pl.BlockSpec((None, t, NB_DH), lambda g, i: (g, 0, 0))
    o_spec = pl.BlockSpec((Q_BLOCK, gd), lambda g, i: (i, g))
    return pl.pallas_call(
        kern,
        grid=(NB_KV, t // Q_BLOCK),
        in_specs=[o_spec, kv_spec, kv_spec, kv_spec, kv_spec, pl.BlockSpec((Q_BLOCK, n_slc), lambda g, i: (i, g))],
        out_specs=[o_spec, o_spec],
        out_shape=[jax.ShapeDtypeStruct((t, NB_HEADS * NB_DH), jnp.float32)] * 2,
        scratch_shapes=[
            pltpu.VMEM((rows, LANES), jnp.float32),
            pltpu.VMEM((rows, LANES), jnp.float32),
            pltpu.VMEM((rows, NB_DH), jnp.float32),
        ],
        compiler_params=pltpu.CompilerParams(
            dimension_semantics=("arbitrary", "arbitrary"), vmem_limit_bytes=VMEM_LIMIT),
        name="slc_win_prompt",
    )(_bf(qr2d), group_major(kpad), group_major(vpad), group_major(kwin), group_major(vwin), sel)


def _split3(x):
    hi = x.astype(jnp.bfloat16)
    r = x - hi.astype(jnp.float32)
    mid = r.astype(jnp.bfloat16)
    lo = (r - mid.astype(jnp.float32)).astype(jnp.bfloat16)
    return hi, mid, lo


def _gla_kernel(q_ref, k_ref, v_ref, g_ref, o_ref, sfin_ref, st_ref, *, rows, heads, chunk):
    i = pl.program_id(1)

    @pl.when(i == 0)
    def _():
        st_ref[...] = jnp.zeros(st_ref.shape, jnp.float32)

    r_idx = lax.broadcasted_iota(jnp.int32, (rows, rows), 0)
    c_idx = lax.broadcasted_iota(jnp.int32, (rows, rows), 1)
    same = lax.shift_right_logical(r_idx, 5) == lax.shift_right_logical(c_idx, 5)
    causal = same & (c_idx <= r_idx)
    tri = jnp.where(causal, 1.0, 0.0).astype(jnp.bfloat16)
    blk = jnp.where(same, 1.0, 0.0).astype(jnp.bfloat16)
    f32 = jnp.float32
    for hh in range(heads):
        sl = slice(hh * HA_DK, (hh + 1) * HA_DK)
        qh, kh, vh, gh = q_ref[:, sl], k_ref[:, sl], v_ref[:, sl], g_ref[:, sl]
        parts = _split3(gh)
        bcum = sum(jnp.dot(tri, p, preferred_element_type=f32) for p in parts)
        tot = sum(jnp.dot(blk, p, preferred_element_type=f32) for p in parts)
        q_in = (qh * jnp.exp(bcum)).astype(jnp.bfloat16)
        k_in = (kh * jnp.exp(-bcum)).astype(jnp.bfloat16)
        k_dec = (kh * jnp.exp(tot - bcum)).astype(jnp.bfloat16)
        vb = vh.astype(jnp.bfloat16)
        att = lax.dot_general(q_in, k_in, _NT, preferred_element_type=f32)
        att = jnp.where(causal, att, 0.0).astype(jnp.bfloat16)
        o_intra = jnp.dot(att, vb, preferred_element_type=f32)
        st = st_ref[hh]
        for c in range(rows // chunk):
            rs = slice(c * chunk, (c + 1) * chunk)
            o_inter = lax.dot_general(q_in[rs], st.astype(jnp.bfloat16), _NT, preferred_element_type=f32)
            o_ref[rs, sl] = o_intra[rs] + o_inter
            upd = lax.dot_general(vb[rs], k_dec[rs], _TN, preferred_element_type=f32)
            st = jnp.exp(tot[c * chunk:c * chunk + 1, :]) * st + upd
        st_ref[hh] = st

    @pl.when(i == pl.num_programs(1) - 1)
    def _():
        for hh in range(heads):
            sfin_ref[hh] = st_ref[hh].T


def _gla_prompt(q, k, v, logf):
    t = q.shape[0]
    assert t % GLA_ROWS == 0 and GLA_ROWS % HA_CHUNK == 0 and HA_CHUNK == 32
    assert HA_HEADS % GLA_HEADS == 0 and HA_DK == HA_DV
    w = GLA_HEADS * HA_DK
    spec = pl.BlockSpec((GLA_ROWS, w), lambda h, i: (i, h))
    kern = functools.partial(_gla_kernel, rows=GLA_ROWS, heads=GLA_HEADS, chunk=HA_CHUNK)
    return pl.pallas_call(
        kern,
        grid=(HA_HEADS // GLA_HEADS, t // GLA_ROWS),
        in_specs=[spec] * 4,
        out_specs=[spec, pl.BlockSpec((GLA_HEADS, HA_DK, HA_DV), lambda h, i: (h, 0, 0))],
        out_shape=[jax.ShapeDtypeStruct((t, HA_HEADS * HA_DV), jnp.float32),
                   jax.ShapeDtypeStruct((HA_HEADS, HA_DK, HA_DV), jnp.float32)],
        scratch_shapes=[pltpu.VMEM((GLA_HEADS, HA_DV, HA_DK), jnp.float32)],
        compiler_params=pltpu.CompilerParams(
            dimension_semantics=("arbitrary", "arbitrary"), vmem_limit_bytes=VMEM_LIMIT),
        name="gla_prompt",
    )(q, k, v, logf)


def _page_specs(n_pages):
    return [pl.BlockSpec((PAGE * NB_KV, NB_DH), functools.partial(lambda b, pt, p: (pt[b, p], 0), p=p))
            for p in range(n_pages)]


def _cmp_pages_kernel(pt_ref, *refs, n_pages):
    kp = refs[:n_pages]
    vp = refs[n_pages:2 * n_pages]
    (w1k_ref, w1v_ref, pek_ref, pev_ref, w2k_ref, w2v_ref, gain_ref, ko_ref, vo_ref, x_ref, st_ref) = refs[2 * n_pages:]
    n_sub = n_pages * SUBS_PER_PAGE
    f32 = jnp.float32
    row = lax.broadcasted_iota(jnp.int32, (NB_KV * n_sub, 1), 0)
    last = lax.rem(row, n_sub) == n_sub - 1
    for pages, w1_ref, pe_ref, w2_ref, o_ref, norm in ((kp, w1k_ref, pek_ref, w2k_ref, ko_ref, True),
                                                       (vp, w1v_ref, pev_ref, w2v_ref, vo_ref, False)):
        for p in range(n_pages):
            for n in range(SUBS_PER_PAGE):
                st_ref[pl.ds((p * SUBS_PER_PAGE + n) * SUB_PITCH, SUB_ROWS), :] = pages[p][pl.ds(n * SUB_ROWS, SUB_ROWS), :]
        for g in range(NB_KV):
            for s in range(CMP_STRIDE):
                x_ref[g * n_sub:(g + 1) * n_sub, s * NB_DH:(s + 1) * NB_DH] = _bf(
                    st_ref[pl.ds(s * NB_KV + g, n_sub, stride=SUB_PITCH), :])
        w1 = w1_ref[...]
        a01 = jnp.dot(x_ref[...], w1, preferred_element_type=f32)
        c = jnp.dot(_bf(pe_ref[...]), w1, preferred_element_type=f32)
        a0 = a01[:, :NB_DH] + c[0:1, :NB_DH]
        a1 = a01[:, NB_DH:] + c[1:2, NB_DH:]
        pre = a0 + pltpu.roll(a1, NB_KV * n_sub - 1, 0)
        y = jnp.dot(_bf(jax.nn.gelu(pre)), w2_ref[...], preferred_element_type=f32)
        if norm:
            y = y * lax.rsqrt(jnp.mean(y * y, axis=-1, keepdims=True) + EPS) * gain_ref[...]
        o_ref[...] = jnp.where(last, 0.0, y).astype(o_ref.dtype)


def _cmp_pages(page_table, cache_k, cache_v, pe_k, pe_v, w1_k, w1_v, w2_k, w2_v, gain):
    nb, n_pages = page_table.shape
    n_phys = cache_k.shape[0]
    assert cache_k.shape[1:] == (PAGE, NB_KV, NB_DH) and CMP_LEN == 2 * CMP_STRIDE
    n_sub = n_pages * SUBS_PER_PAGE
    kd = CMP_STRIDE * NB_DH

    def w1cat(w1):
        return _bf(jnp.concatenate([w1[:CMP_STRIDE].reshape(kd, NB_DH), w1[CMP_STRIDE:].reshape(kd, NB_DH)], axis=1))

    def pecat(pe):
        z = jnp.zeros((8, kd), jnp.float32)
        return z.at[0].set(pe[:CMP_STRIDE].reshape(kd)).at[1].set(pe[CMP_STRIDE:].reshape(kd))

    def full(*shape):
        return pl.BlockSpec(shape, lambda b, pt: (0,) * len(shape))

    out_spec = pl.BlockSpec((None, NB_KV * n_sub, NB_DH), lambda b, pt: (b, 0, 0))
    ck = cache_k.reshape(n_phys * PAGE * NB_KV, NB_DH)
    cv = cache_v.reshape(n_phys * PAGE * NB_KV, NB_DH)
    grid_spec = pltpu.PrefetchScalarGridSpec(
        num_scalar_prefetch=1,
        grid=(nb,),
        in_specs=_page_specs(n_pages) * 2 + [full(kd, 2 * NB_DH), full(kd, 2 * NB_DH), full(8, kd), full(8, kd),
                                            full(NB_DH, NB_DH), full(NB_DH, NB_DH), full(1, NB_DH)],
        out_specs=[out_spec, out_spec],
        scratch_shapes=[pltpu.VMEM((NB_KV * n_sub, kd), jnp.bfloat16),
                        pltpu.VMEM((n_sub * SUB_PITCH, NB_DH), jnp.float32)],
    )
    return pl.pallas_call(
        functools.partial(_cmp_pages_kernel, n_pages=n_pages),
        grid_spec=grid_spec,
        out_shape=[jax.ShapeDtypeStruct((nb, NB_KV * n_sub, NB_DH), jnp.bfloat16)] * 2,
        compiler_params=pltpu.CompilerParams(dimension_semantics=("arbitrary",), vmem_limit_bytes=VMEM_LIMIT),
        name="cmp_pages",
    )(page_table, *([ck] * n_pages), *([cv] * n_pages), w1cat(w1_k), w1cat(w1_v), pecat(pe_k), pecat(pe_v),
      _bf(w2_k), _bf(w2_v), gain.reshape(1, NB_DH))


def _softmax_with_tail(s, mask, s_tail, ok_tail):
    m = jnp.max(s, axis=-1, keepdims=True)
    for sc in s_tail:
        m = jnp.maximum(m, sc)
    e = jnp.where(mask, jnp.exp(s - m), 0.0)
    e_tail = [jnp.where(ok, jnp.exp(sc - m), 0.0) for ok, sc in zip(ok_tail, s_tail)]
    den = jnp.sum(e, axis=-1, keepdims=True)
    for et in e_tail:
        den = den + et
    inv = 1.0 / jnp.maximum(den, 1e-30)
    return e * inv, [et * inv for et in e_tail]


def _nsa_sample_kernel(pt_ref, *refs, n_pages, n_new, win_keep):
    ksp = refs[:n_pages]
    vsp = refs[n_pages:2 * n_pages]
    (q_ref, qr_ref, kcb_ref, vcb_ref, ksn_ref, vsn_ref, kwc_ref, vwc_ref, kwn_ref, vwn_ref, ov_ref,
     oc_ref, os_ref, ow_ref, kwo_ref, vwo_ref) = refs[2 * n_pages:]
    f32 = jnp.float32
    past = n_pages * PAGE
    n_sub = past // CMP_STRIDE
    n_cmp = n_sub - CMP_LEN // CMP_STRIDE + 1
    n_slc = -(-(past + n_new) // SLC_LEN)
    new_blk = past // SLC_LEN
    assert (past % SLC_LEN) + n_new <= SLC_LEN and n_sub == LANES and n_slc <= LANES
    rows = HEADS_PER_GROUP * n_new
    all_rows = NB_KV * rows
    sel_rows = NB_KV * n_new
    scale = NB_DH ** -0.5
    groups = range(NB_KV)

    def grp_rows(g, n_tok):
        return pl.ds(g, n_tok, stride=NB_KV)

    def stack(parts):
        return jnp.concatenate(parts, axis=0)

    t_row = lax.rem(lax.broadcasted_iota(jnp.int32, (all_rows, 1), 0), n_new)
    pos = past + t_row
    pos_t = past + lax.rem(lax.broadcasted_iota(jnp.int32, (sel_rows, 1), 0), n_new)
    lane = lax.broadcasted_iota(jnp.int32, (1, LANES), 1)
    q = [q_ref[g] for g in groups]
    qr = [qr_ref[g] for g in groups]
    qrf = stack(qr).astype(f32)

    def tail_scores(new_ref):
        kb = _bf(new_ref[...]).astype(f32)
        out = []
        for i in range(n_new):
            ki = stack([jnp.broadcast_to(kb[i * NB_KV + g:i * NB_KV + g + 1, :], (rows, NB_DH)) for g in groups])
            out.append(jnp.sum(qrf * ki, axis=-1, keepdims=True) * scale)
        return out

    def tail_values(p_tail, new_ref, g):
        vb = _bf(new_ref[...]).astype(f32)
        acc = _bf(p_tail[0][g * rows:(g + 1) * rows]).astype(f32) * vb[g:g + 1, :]
        for i in range(1, n_new):
            acc = acc + _bf(p_tail[i][g * rows:(g + 1) * rows]).astype(f32) * vb[i * NB_KV + g:i * NB_KV + g + 1, :]
        return acc

    mask_c = (lane * CMP_STRIDE + (CMP_LEN - 1) <= pos) & (lane < n_cmp)
    s = stack([lax.dot_general(q[g], kcb_ref[g * n_sub:(g + 1) * n_sub, :], _NT, preferred_element_type=f32)
               for g in groups]) * scale
    s = jnp.where(mask_c, s, NEG)
    m = jnp.max(s, axis=-1, keepdims=True)
    e = jnp.where(mask_c, jnp.exp(s - m), 0.0)
    p = e / jnp.maximum(jnp.sum(e, axis=-1, keepdims=True), 1e-30)
    pb = _bf(p)
    for g in groups:
        oc_ref[g] = jnp.dot(pb[g * rows:(g + 1) * rows], vcb_ref[g * n_sub:(g + 1) * n_sub, :],
                            preferred_element_type=f32)
    psum = []
    for g in groups:
        acc = p[g * rows:g * rows + n_new]
        for j in range(1, HEADS_PER_GROUP):
            acc = acc + p[g * rows + j * n_new:g * rows + (j + 1) * n_new]
        psum.append(acc)
    psum = stack(psum)
    hi = _bf(psum)
    lo = _bf(psum - hi.astype(f32))
    ov = ov_ref[...]
    imp = jnp.dot(hi, ov, preferred_element_type=f32) + jnp.dot(lo, ov, preferred_element_type=f32)
    cur = lax.shift_right_logical(pos_t, 6)
    forced = (lane == 0) | (lane == cur) | (lane == cur - 1)
    future = lane * SLC_LEN > pos_t
    score = jnp.where(future, NEG, jnp.where(forced, BIG, imp))
    score = jnp.where(lane >= n_slc, REMOVED, score)
    lane_f = lax.broadcasted_iota(jnp.int32, (sel_rows, LANES), 1).astype(f32)
    sel = jnp.zeros((sel_rows, LANES), f32)
    for _ in range(min(N_SELECT, n_slc)):
        mx = jnp.max(score, axis=-1, keepdims=True)
        idx = jnp.min(jnp.where(score == mx, lane_f, float(LANES)), axis=-1, keepdims=True)
        hit = lane_f == idx
        sel = jnp.where(hit, 1.0, sel)
        score = jnp.where(hit, REMOVED, score)
    sel_r = stack([sel[g * n_new:(g + 1) * n_new] for g in groups for _ in range(HEADS_PER_GROUP)])
    expand = jnp.where(lax.broadcasted_iota(jnp.int32, (LANES, past), 0)
                       == lax.shift_right_logical(lax.broadcasted_iota(jnp.int32, (LANES, past), 1), 6),
                       1.0, 0.0).astype(jnp.bfloat16)
    picked = jnp.dot(_bf(sel_r), expand, preferred_element_type=f32) > 0.5
    s_all = stack([jnp.concatenate([lax.dot_general(qr[g], _bf(ksp[pg][grp_rows(g, PAGE), :]), _NT,
                                                    preferred_element_type=f32)
                                    for pg in range(n_pages)], axis=1) for g in groups]) * scale
    s_all = jnp.where(picked, s_all, NEG)
    new_sel = sel_r[:, new_blk:new_blk + 1] > 0.5
    ok_new = [new_sel & (t_row >= i) for i in range(n_new)]
    s_new = [jnp.where(ok, sc, NEG) for ok, sc in zip(ok_new, tail_scores(ksn_ref))]
    p_all, p_new = _softmax_with_tail(s_all, picked, s_new, ok_new)
    p_all = _bf(p_all)
    for g in groups:
        acc = tail_values(p_new, vsn_ref, g)
        for pg in range(n_pages):
            acc = acc + jnp.dot(p_all[g * rows:(g + 1) * rows, pg * PAGE:(pg + 1) * PAGE],
                                _bf(vsp[pg][grp_rows(g, PAGE), :]), preferred_element_type=f32)
        os_ref[g] = acc
    mask_w = lax.broadcasted_iota(jnp.int32, (1, win_keep), 1) >= t_row
    s_w = stack([lax.dot_general(qr[g], _bf(kwc_ref[grp_rows(g, win_keep), :]), _NT, preferred_element_type=f32)
                 for g in groups]) * scale
    s_w = jnp.where(mask_w, s_w, NEG)
    ok_wn = [t_row >= i for i in range(n_new)]
    s_wn = [jnp.where(ok, sc, NEG) for ok, sc in zip(ok_wn, tail_scores(kwn_ref))]
    p_w, p_wn = _softmax_with_tail(s_w, mask_w, s_wn, ok_wn)
    p_w = _bf(p_w)
    for g in groups:
        ow_ref[g] = (tail_values(p_wn, vwn_ref, g)
                     + jnp.dot(p_w[g * rows:(g + 1) * rows], _bf(vwc_ref[grp_rows(g, win_keep), :]),
                               preferred_element_type=f32))
    new_rows = n_new * NB_KV
    kept_rows = win_keep * NB_KV - new_rows
    for src_ref, add_ref, dst_ref in ((kwc_ref, kwn_ref, kwo_ref), (vwc_ref, vwn_ref, vwo_ref)):
        dst_ref[0:kept_rows, :] = src_ref[new_rows:new_rows + kept_rows, :]
        dst_ref[kept_rows:kept_rows + new_rows, :] = add_ref[...]


def _nsa_sample(page_table, q, qr, kcb, vcb, cache_sk, cache_sv, ks_new, vs_new, cache_wk, cache_wv, kw_new, vw_new):
    nb, n_pages = page_table.shape
    t = q.shape[1]
    n_phys = cache_sk.shape[0]
    keep = cache_wk.shape[1]
    assert keep == WINDOW and (t * NB_KV) % 8 == 0 and t < keep
    rows = HEADS_PER_GROUP * t
    n_sub = n_pages * SUBS_PER_PAGE

    def q_rows(a):
        a = a.reshape(nb, t, NB_KV, HEADS_PER_GROUP, NB_DH)
        return _bf(jnp.transpose(a, (0, 2, 3, 1, 4)).reshape(nb, NB_KV, rows, NB_DH))

    def q_unrows(a):
        a = a.reshape(nb, NB_KV, HEADS_PER_GROUP, t, NB_DH)
        return jnp.transpose(a, (0, 3, 1, 2, 4)).reshape(nb, t, NB_HEADS, NB_DH)

    def per_b(*shape):
        return pl.BlockSpec((None,) + shape, lambda b, pt: (b,) + (0,) * len(shape))

    def rows_view(a):
        return a.reshape(a.shape[0], a.shape[1] * NB_KV, NB_DH)

    n_cmp = n_sub - CMP_LEN // CMP_STRIDE + 1
    q_spec = per_b(NB_KV, rows, NB_DH)
    win_spec = per_b(keep * NB_KV, NB_DH)
    new_spec = per_b(t * NB_KV, NB_DH)
    cmp_spec = per_b(NB_KV * n_sub, NB_DH)
    in_specs = (_page_specs(n_pages) * 2
                + [q_spec, q_spec, cmp_spec, cmp_spec, new_spec, new_spec, win_spec, win_spec, new_spec, new_spec,
                   pl.BlockSpec((LANES, LANES), lambda b, pt: (0, 0))])
    grid_spec = pltpu.PrefetchScalarGridSpec(
        num_scalar_prefetch=1, grid=(nb,), in_specs=in_specs, out_specs=[q_spec] * 3 + [win_spec] * 2)
    sk = cache_sk.reshape(n_phys * PAGE * NB_KV, NB_DH)
    sv = cache_sv.reshape(n_phys * PAGE * NB_KV, NB_DH)
    o_c, o_s, o_w, wk_next, wv_next = pl.pallas_call(
        functools.partial(_nsa_sample_kernel, n_pages=n_pages, n_new=t, win_keep=keep),
        grid_spec=grid_spec,
        out_shape=([jax.ShapeDtypeStruct((nb, NB_KV, rows, NB_DH), jnp.float32)] * 3
                   + [jax.ShapeDtypeStruct((nb, keep * NB_KV, NB_DH), jnp.float32)] * 2),
        compiler_params=pltpu.CompilerParams(dimension_semantics=("arbitrary",), vmem_limit_bytes=VMEM_LIMIT),
        name="nsa_sample",
    )(page_table, *([sk] * n_pages), *([sv] * n_pages), q_rows(q), q_rows(qr), kcb, vcb,
      rows_view(ks_new), rows_view(vs_new), rows_view(cache_wk), rows_view(cache_wv), rows_view(kw_new),
      rows_view(vw_new), _overlap_matrix(LANES, n_cmp, LANES))
    return (q_unrows(o_c), q_unrows(o_s), q_unrows(o_w),
            wk_next.reshape(cache_wk.shape), wv_next.reshape(cache_wv.shape))


def _mm_kernel(x_ref, w_ref, o_ref, wb_ref):
    @pl.when(pl.program_id(1) == 0)
    def _():
        wb_ref[...] = w_ref[...].astype(jnp.bfloat16)

    o_ref[...] = jnp.dot(x_ref[...], wb_ref[...], preferred_element_type=jnp.float32)


def _mm_tiles(m, k, n):
    tm = min(m, 1024 if k <= 2048 else 512)
    tn = min(n, 512 if k <= 2048 else 256)
    return tm, tn


def _mm(x, w):
    lead = x.shape[:-1]
    k = x.shape[-1]
    n = w.shape[1]
    x2 = _bf(x.reshape(-1, k))
    m = x2.shape[0]
    tm, tn = _mm_tiles(m, k, n)
    assert m % tm == 0
    out = pl.pallas_call(
        _mm_kernel,
        grid=(pl.cdiv(n, tn), m // tm),
        in_specs=[pl.BlockSpec((tm, k), lambda j, i: (i, 0)),
                  pl.BlockSpec((k, tn), lambda j, i: (0, j))],
        out_specs=pl.BlockSpec((tm, tn), lambda j, i: (i, j)),
        out_shape=jax.ShapeDtypeStruct((m, n), jnp.float32),
        scratch_shapes=[pltpu.VMEM((k, tn), jnp.bfloat16)],
        compiler_params=pltpu.CompilerParams(
            dimension_semantics=("arbitrary", "arbitrary"), vmem_limit_bytes=VMEM_LIMIT),
        name="mm",
    )(x2, w)
    return out.reshape(*lead, n)


def _rmsnorm(x, g):
    xf = x.astype(jnp.float32)
    y = xf * lax.rsqrt(jnp.mean(xf * xf, axis=-1, keepdims=True) + EPS)
    return (y * g.astype(jnp.float32)).astype(x.dtype)


def _rope(x, pos):
    half = x.shape[-1] // 2
    inv = ROPE_THETA ** (-jnp.arange(half, dtype=jnp.float32) / half)
    ang = pos.astype(jnp.float32)[:, None] * inv[None, :]
    cos, sin = jnp.cos(ang)[:, None, :], jnp.sin(ang)[:, None, :]
    xf = x.astype(jnp.float32)
    x1, x2 = xf[..., :half], xf[..., half:]
    return jnp.concatenate([x1 * cos - x2 * sin, x2 * cos + x1 * sin], axis=-1).astype(x.dtype)


def _gla_chunked(q, k, v, logf, s0):
    b, t, h, dk = q.shape
    dv = v.shape[-1]
    c = math.gcd(t, HA_CHUNK)
    n = t // c

    def to_chunks(a):
        return jnp.transpose(a.reshape(b, n, c, h, a.shape[-1]), (1, 0, 3, 2, 4))

    qc, kc, vc, gc = to_chunks(q), to_chunks(k), to_chunks(v), to_chunks(logf)
    bcum = jnp.cumsum(gc, axis=3)
    blast = bcum[:, :, :, -1, :]
    q_in = qc * jnp.exp(bcum)
    k_in = kc * jnp.exp(-bcum)
    k_dec = kc * jnp.exp(blast[:, :, :, None, :] - bcum)
    causal = jnp.tril(jnp.ones((c, c), dtype=bool))
    att = jnp.where(causal, jnp.einsum('nbhik,nbhjk->nbhij', q_in, k_in), 0.0)
    o_intra = jnp.einsum('nbhij,nbhjv->nbhiv', att, vc)

    def step(s, inp):
        qi, kd, vi, dl = inp
        o_inter = jnp.einsum('bhik,bhkv->bhiv', qi, s)
        s = jnp.exp(dl)[..., None] * s + jnp.einsum('bhik,bhiv->bhkv', kd, vi)
        return s, o_inter

    s_final, o_inter = lax.scan(step, s0.astype(jnp.float32), (q_in, k_dec, vc, blast))
    o = o_intra + o_inter
    return jnp.transpose(o, (1, 0, 3, 2, 4)).reshape(b, t, h, dv), s_final


def _hgrn2_layer(xn, s0, lb, w_in, out_gain, w_out):
    b, t, _ = xn.shape
    hk = HA_HEADS * HA_DK
    hv = HA_HEADS * HA_DV
    proj = _mm(xn, w_in)
    q = jax.nn.silu(proj[..., :hk])
    f = lb + (1.0 - lb) * jax.nn.sigmoid(proj[..., hk:2 * hk])
    i_in = proj[..., 2 * hk:2 * hk + hv]
    gate = proj[..., 2 * hk + hv:]
    shp_k = (b, t, HA_HEADS, HA_DK)
    shp_v = (b, t, HA_HEADS, HA_DV)
    if s0 is None:
        assert b == 1
        o, s_new = _gla_prompt(q.reshape(t, hk), (1.0 - f).reshape(t, hk), i_in.reshape(t, hv),
                               jnp.log(f).reshape(t, hk))
        o, s_new = o.reshape(shp_v), s_new[None]
    else:
        o, s_new = _gla_chunked(q.reshape(shp_k), (1.0 - f).reshape(shp_k), i_in.reshape(shp_v),
                                jnp.log(f).reshape(shp_k), s0)
    o = _rmsnorm(o, out_gain.reshape(HA_HEADS, HA_DV)) * jax.nn.silu(gate).reshape(shp_v)
    return _mm(o.reshape(b, t, hv), w_out), s_new


def _conv_ffn(xn, prev, w_up, conv_w, conv_b, w_down):
    t = xn.shape[1]
    up = _mm(xn, w_up)
    u, gt = up[..., :D_FF], up[..., D_FF:]
    ext = jnp.concatenate([prev.astype(gt.dtype), gt], axis=1)
    conv = conv_b
    for j in range(CONV_W):
        conv = conv + conv_w[j] * ext[:, j:j + t]
    return _mm(jax.nn.silu(conv) * u, w_down), ext[:, t:]


def _ple(h, p, gain, w_p, w_g):
    gate = jax.nn.sigmoid(_mm(_rmsnorm(h, gain), w_g))
    return h + _mm(p, w_p) * gate


def _compress(rows, pe, w1, w2):
    b, l, g, d = rows.shape
    n_sub = l // CMP_STRIDE
    r = CMP_LEN // CMP_STRIDE
    n_cmp = n_sub - r + 1
    sub = rows[:, :n_sub * CMP_STRIDE].reshape(b, n_sub, CMP_STRIDE, g, d)
    pre = None
    for j in range(r):
        sl = slice(j * CMP_STRIDE, (j + 1) * CMP_STRIDE)
        part = jnp.einsum('bnsgd,sde->bnge', sub + pe[sl][None, None, :, None, :], w1[sl])[:, j:j + n_cmp]
        pre = part if pre is None else pre + part
    return jnp.einsum('bnge,ef->bngf', jax.nn.gelu(pre), w2)


def _nsa_layer(xn, pos, shared, w_in, q_gain, w_out, banded):
    b, t, _ = xn.shape
    hd = NB_HEADS * NB_DH
    proj = _mm(xn, w_in)
    q = _rmsnorm(proj[..., :hd].reshape(b, t, NB_HEADS, NB_DH), q_gain)
    gates = jax.nn.sigmoid(proj[..., hd:].astype(jnp.float32)).reshape(b, t, 3, NB_HEADS)
    qr = _rope(q, pos)
    win_next = None
    if banded:
        assert b == 1
        kcb, vcb, kpad, vpad, win_k, win_v = shared
        o_c2, sel = _cmp_topk_prompt(q.reshape(t, hd), kcb[0], vcb[0])
        o_s2, o_w2 = _slc_win_prompt(qr.reshape(t, hd), kpad[0], vpad[0], win_k[0], win_v[0], sel)
        o_c = o_c2.reshape(b, t, NB_HEADS, NB_DH)
        o_s = o_s2.reshape(b, t, NB_HEADS, NB_DH)
        o_w = o_w2.reshape(b, t, NB_HEADS, NB_DH)
    else:
        o_c, o_s, o_w, wk_next, wv_next = _nsa_sample(shared[0], q, qr, *shared[1:])
        win_next = (wk_next, wv_next)
    o = (gates[:, :, 0, :, None] * o_c + gates[:, :, 1, :, None] * o_s + gates[:, :, 2, :, None] * o_w)
    return _mm(o.reshape(b, t, hd), w_out), win_next


def _shared_kv(h, pos, past, wt):
    b, t, _ = h.shape
    kv = _mm(_rmsnorm(h, wt['kv_norm']), wt['kv_w']).reshape(b, t, 6, NB_KV, NB_DH)
    k_cmp, v_cmp = kv[:, :, 0], kv[:, :, 1]
    k_slc = _rope(_rmsnorm(kv[:, :, 2], wt['k_norm_slc']), pos)
    v_slc = kv[:, :, 3]
    k_win = _rope(_rmsnorm(kv[:, :, 4], wt['k_norm_win']), pos)
    v_win = kv[:, :, 5]
    if past is None:
        keep = min(WINDOW, t)
        kcb = _rmsnorm(_compress(k_cmp, wt['cmp_pe_k'], wt['cmp_w1_k'], wt['cmp_w2_k']), wt['k_norm_cmp'])
        vcb = _compress(v_cmp, wt['cmp_pe_v'], wt['cmp_w1_v'], wt['cmp_w2_v'])
        assert t % SLC_LEN == 0
        shared = (kcb, vcb, k_slc, v_slc, k_win, v_win)
        rows = (k_cmp, v_cmp, k_slc, v_slc, k_win[:, -keep:], v_win[:, -keep:])
    else:
        page_table, ckc, cvc, cks, cvs, cwk, cwv = past
        assert t < CMP_STRIDE and (page_table.shape[1] * PAGE) % CMP_STRIDE == 0
        kcb, vcb = _cmp_pages(page_table, ckc, cvc, wt['cmp_pe_k'], wt['cmp_pe_v'], wt['cmp_w1_k'], wt['cmp_w1_v'],
                              wt['cmp_w2_k'], wt['cmp_w2_v'], wt['k_norm_cmp'])
        shared = (page_table, kcb, vcb, cks, cvs, k_slc, v_slc, cwk, cwv, k_win, v_win)
        rows = (k_cmp, v_cmp, k_slc, v_slc, None, None)
    return shared, rows


def _trunk(x, p, start, s_hgrn, s_conv, past, wt):
    b, t, _ = x.shape
    pos = start + jnp.arange(t)
    banded = past is None
    lb = jnp.cumsum(jax.nn.softmax(wt['a_lb_logits'].astype(jnp.float32), axis=0), axis=0)
    h = x
    hgrn_out, conv_out = [], []
    shared, rows = None, None
    for i in range(DEPTH):
        hn = _rmsnorm(h, wt['norm_mix'][i])
        if i < N_A:
            y, s = _hgrn2_layer(hn, None if s_hgrn is None else s_hgrn[i], lb[i], wt['a_w_in'][i],
                                wt['a_out_norm'][i], wt['a_w_out'][i])
            hgrn_out.append(s)
        else:
            j = i - N_A
            y, win_next = _nsa_layer(hn, pos, shared, wt['b_w_in'][j], wt['b_q_norm'][j], wt['b_w_out'][j], banded)
            if win_next is not None:
                assert N_B == 1
                rows = rows[:4] + win_next
        h = h + y
        f, cs = _conv_ffn(_rmsnorm(h, wt['norm_ffn'][i]), s_conv[i], wt['ffn_w_up'][i], wt['ffn_conv_w'][i],
                          wt['ffn_conv_b'][i], wt['ffn_w_down'][i])
        conv_out.append(cs)
        h = h + f
        h = _ple(h, p[i], wt['norm_ple'][i], wt['ple_w_p'][i], wt['ple_w_g'][i])
        if i == N_A - 1:
            shared, rows = _shared_kv(h, pos, past, wt)
    return h, jnp.stack(hgrn_out), jnp.stack(conv_out), rows


def kernel(x_prompt, x_sample, state_hgrn, state_conv, cache_cmp_k, cache_cmp_v, cache_slc_k, cache_slc_v,
           cache_win_k, cache_win_v, page_table, p_prompt, p_sample, norm_mix, norm_ffn, norm_ple,
           a_w_in, a_lb_logits, a_out_norm, a_w_out, kv_norm, kv_w, k_norm_cmp, k_norm_slc, k_norm_win,
           cmp_pe_k, cmp_pe_v, cmp_w1_k, cmp_w2_k, cmp_w1_v, cmp_w2_v, b_w_in, b_q_norm, b_w_out,
           ffn_w_up, ffn_conv_w, ffn_conv_b, ffn_w_down, ple_w_p, ple_w_g):
    wt = {
        'norm_mix': norm_mix, 'norm_ffn': norm_ffn, 'norm_ple': norm_ple,
        'a_w_in': a_w_in, 'a_lb_logits': a_lb_logits, 'a_out_norm': a_out_norm, 'a_w_out': a_w_out,
        'kv_norm': kv_norm, 'kv_w': kv_w, 'k_norm_cmp': k_norm_cmp, 'k_norm_slc': k_norm_slc,
        'k_norm_win': k_norm_win, 'cmp_pe_k': cmp_pe_k, 'cmp_pe_v': cmp_pe_v, 'cmp_w1_k': cmp_w1_k,
        'cmp_w2_k': cmp_w2_k, 'cmp_w1_v': cmp_w1_v, 'cmp_w2_v': cmp_w2_v,
        'b_w_in': b_w_in, 'b_q_norm': b_q_norm, 'b_w_out': b_w_out,
        'ffn_w_up': ffn_w_up, 'ffn_conv_w': ffn_conv_w, 'ffn_conv_b': ffn_conv_b, 'ffn_w_down': ffn_w_down,
        'ple_w_p': ple_w_p, 'ple_w_g': ple_w_g,
    }
    bp, past_len = x_prompt.shape[0], page_table.shape[1] * cache_cmp_k.shape[1]
    s_conv0 = jnp.zeros((DEPTH, bp, CONV_W - 1, D_FF), x_prompt.dtype)
    y_prompt, hgrn_p, conv_p, rows_p = _trunk(x_prompt, p_prompt, 0, None, s_conv0, None, wt)
    past = (page_table, cache_cmp_k, cache_cmp_v, cache_slc_k, cache_slc_v, cache_win_k, cache_win_v)
    y_sample, hgrn_s, conv_s, rows_s = _trunk(x_sample, p_sample, past_len, state_hgrn, state_conv, past, wt)
    kc_p, vc_p, ks_p, vs_p, wk_p, wv_p = rows_p
    kc_s, vc_s, ks_s, vs_s, wk_s, wv_s = rows_s
    return (y_prompt, y_sample, hgrn_p, hgrn_s, conv_p, conv_s, kc_p, vc_p, ks_p, vs_p,
            kc_s, vc_s, ks_s, vs_s, wk_p, wv_p, wk_s, wv_s)
```

```python
import functools
import math

import jax
import jax.numpy as jnp
from jax import lax
from jax.experimental import pallas as pl
from jax.experimental.pallas import tpu as pltpu

D_MODEL = 2048
DEPTH = 2
N_A = DEPTH // 2
N_B = DEPTH - N_A
EPS = 1e-6
HA_DK = 128
HA_HEADS = D_MODEL // HA_DK
HA_DV = D_MODEL // HA_HEADS
HA_CHUNK = 32
NB_DH = 128
NB_HEADS = D_MODEL // NB_DH
NB_KV = 4
HEADS_PER_GROUP = NB_HEADS // NB_KV
CMP_LEN = 32
CMP_STRIDE = 16
SLC_LEN = 64
N_SELECT = 16
WINDOW = 512
Q_BLOCK = 128
ROPE_THETA = 10000.0
D_FF = ((8 * D_MODEL // 3 + 127) // 128) * 128
CONV_W = 3
NEG = -1e30
BIG = 1e30
REMOVED = -3.0e38
LANES = 128
SLC_KV_TILE = 512
WIN_SPAN = WINDOW + Q_BLOCK
GLA_ROWS = 256
GLA_HEADS = 4
PAGE = 128
SUBS_PER_PAGE = PAGE // CMP_STRIDE
SUB_ROWS = CMP_STRIDE * NB_KV
SUB_PITCH = SUB_ROWS + 8
VMEM_LIMIT = 48 * 1024 * 1024
_NT = (((1,), (1,)), ((), ()))
_TN = (((0,), (0,)), ((), ()))


def _bf(x):
    return x.astype(jnp.bfloat16)


def _cmp_topk_kernel(q_ref, k_ref, v_ref, ov_ref, o_ref, sel_ref, *, q_blk, n_cmp_pad, n_slc, n_sel):
    i = pl.program_id(0)
    f32 = jnp.float32
    pos = i * q_blk + lax.broadcasted_iota(jnp.int32, (q_blk, 1), 0)
    kend = lax.broadcasted_iota(jnp.int32, (1, n_cmp_pad), 1) * CMP_STRIDE + (CMP_LEN - 1)
    mask_c = kend <= pos
    scale = NB_DH ** -0.5
    ov = ov_ref[...]
    imp = []
    for g in range(NB_KV):
        k = k_ref[g]
        v = v_ref[g]
        psum = jnp.zeros((q_blk, n_cmp_pad), f32)
        for j in range(HEADS_PER_GROUP):
            cs = slice((g * HEADS_PER_GROUP + j) * NB_DH, (g * HEADS_PER_GROUP + j + 1) * NB_DH)
            s = lax.dot_general(_bf(q_ref[:, cs]), k, _NT, preferred_element_type=f32) * scale
            s = jnp.where(mask_c, s, NEG)
            m = jnp.max(s, axis=-1, keepdims=True)
            e = jnp.where(mask_c, jnp.exp(s - m), 0.0)
            p = e / jnp.maximum(jnp.sum(e, axis=-1, keepdims=True), 1e-30)
            o_ref[:, cs] = jnp.dot(_bf(p), v, preferred_element_type=f32)
            psum = psum + p
        hi = _bf(psum)
        lo = _bf(psum - hi.astype(f32))
        imp.append(jnp.dot(hi, ov, preferred_element_type=f32) + jnp.dot(lo, ov, preferred_element_type=f32))
    imp = jnp.concatenate(imp, axis=0)
    pos_r = jnp.concatenate([pos] * NB_KV, axis=0)
    sidx = lax.broadcasted_iota(jnp.int32, (1, n_slc), 1)
    cur = lax.shift_right_logical(pos_r, 6)
    forced = (sidx == 0) | (sidx == cur) | (sidx == cur - 1)
    future = sidx * SLC_LEN > pos_r
    score = jnp.where(future, NEG, jnp.where(forced, BIG, imp))
    lane = lax.broadcasted_iota(jnp.int32, (NB_KV * q_blk, n_slc), 1).astype(f32)
    sel = jnp.zeros((NB_KV * q_blk, n_slc), f32)
    for _ in range(n_sel):
        m = jnp.max(score, axis=-1, keepdims=True)
        idx = jnp.min(jnp.where(score == m, lane, float(n_slc)), axis=-1, keepdims=True)
        hit = lane == idx
        sel = jnp.where(hit, 1.0, sel)
        score = jnp.where(hit, REMOVED, score)
    for g in range(NB_KV):
        sel_ref[:, g * n_slc:(g + 1) * n_slc] = sel[g * q_blk:(g + 1) * q_blk].astype(sel_ref.dtype)


def _overlap_matrix(n_cmp_rows, n_cmp, n_slc_cols):
    cstart = jnp.arange(n_cmp_rows) * CMP_STRIDE
    sstart = jnp.arange(n_slc_cols) * SLC_LEN
    return _bf((cstart[:, None] <= sstart[None, :] + (SLC_LEN - 1))
               & (cstart[:, None] + (CMP_LEN - 1) >= sstart[None, :])
               & (jnp.arange(n_cmp_rows)[:, None] < n_cmp))


def _cmp_topk_prompt(q2d, kcb, vcb):
    t = q2d.shape[0]
    n_cmp = kcb.shape[0]
    n_cmp_pad = -(-n_cmp // LANES) * LANES
    n_slc = t // SLC_LEN
    assert SLC_LEN == 64 and n_slc == LANES and t % Q_BLOCK == 0
    n_sel = min(N_SELECT, n_slc)
    padc = ((0, 0), (0, n_cmp_pad - n_cmp), (0, 0))
    k = _bf(jnp.pad(jnp.transpose(kcb, (1, 0, 2)), padc))
    v = _bf(jnp.pad(jnp.transpose(vcb, (1, 0, 2)), padc))
    hd = NB_HEADS * NB_DH
    kern = functools.partial(_cmp_topk_kernel, q_blk=Q_BLOCK, n_cmp_pad=n_cmp_pad, n_slc=n_slc, n_sel=n_sel)
    kv_spec = pl.BlockSpec((NB_KV, n_cmp_pad, NB_DH), lambda i: (0, 0, 0))
    return pl.pallas_call(
        kern,
        grid=(t // Q_BLOCK,),
        in_specs=[pl.BlockSpec((Q_BLOCK, hd), lambda i: (i, 0)), kv_spec, kv_spec,
                  pl.BlockSpec((n_cmp_pad, n_slc), lambda i: (0, 0))],
        out_specs=[pl.BlockSpec((Q_BLOCK, hd), lambda i: (i, 0)),
                   pl.BlockSpec((Q_BLOCK, NB_KV * n_slc), lambda i: (i, 0))],
        out_shape=[jax.ShapeDtypeStruct((t, hd), jnp.float32),
                   jax.ShapeDtypeStruct((t, NB_KV * n_slc), jnp.bfloat16)],
        compiler_params=pltpu.CompilerParams(dimension_semantics=("arbitrary",), vmem_limit_bytes=VMEM_LIMIT),
        name="cmp_topk_prompt",
    )(q2d, k, v, _overlap_matrix(n_cmp_pad, n_cmp, n_slc))


def _slc_win_kernel(q_ref, ks_ref, vs_ref, kw_ref, vw_ref, sel_ref, os_ref, ow_ref, m_ref, l_ref, acc_ref,
                    *, q_blk, n_slc, kv_tile):
    i = pl.program_id(1)
    rows = HEADS_PER_GROUP * q_blk
    blocks_per_tile = kv_tile // SLC_LEN
    reps = kv_tile // LANES
    scale = NB_DH ** -0.5
    q4 = jnp.concatenate([q_ref[:, j * NB_DH:(j + 1) * NB_DH] for j in range(HEADS_PER_GROUP)], axis=0)
    sel = sel_ref[...]
    pos = i * q_blk + lax.broadcasted_iota(jnp.int32, (q_blk, 1), 0)
    m_ref[...] = jnp.full(m_ref.shape, NEG, jnp.float32)
    l_ref[...] = jnp.zeros(l_ref.shape, jnp.float32)
    acc_ref[...] = jnp.zeros(acc_ref.shape, jnp.float32)
    blk_row = lax.broadcasted_iota(jnp.int32, (n_slc, kv_tile), 0)
    blk_col = lax.shift_right_logical(lax.broadcasted_iota(jnp.int32, (n_slc, kv_tile), 1), 6)
    key_off = lax.broadcasted_iota(jnp.int32, (1, kv_tile), 1)
    n_tiles = lax.div(i * q_blk + (q_blk - 1), kv_tile) + 1

    def body(j, carry):
        start = pl.multiple_of(j * kv_tile, kv_tile)
        kt = ks_ref[pl.ds(start, kv_tile), :]
        vt = vs_ref[pl.ds(start, kv_tile), :]
        s = lax.dot_general(q4, kt, _NT, preferred_element_type=jnp.float32) * scale
        expand = jnp.where(blk_row == blk_col + j * blocks_per_tile, 1.0, 0.0).astype(jnp.bfloat16)
        picked = jnp.dot(sel, expand, preferred_element_type=jnp.float32) > 0.5
        mask = picked & (key_off + start <= pos)
        s = jnp.where(mask[None], s.reshape(HEADS_PER_GROUP, q_blk, kv_tile), NEG).reshape(rows, kv_tile)
        m_prev = m_ref[...]
        m_new = jnp.maximum(m_prev, jnp.max(s, axis=-1, keepdims=True))
        alpha = jnp.exp(m_prev - m_new)
        p = jnp.exp(s - jnp.concatenate([m_new] * reps, axis=1))
        l_ref[...] = alpha * l_ref[...] + jnp.sum(p, axis=-1, keepdims=True)
        acc_ref[...] = alpha * acc_ref[...] + jnp.dot(p.astype(jnp.bfloat16), vt, preferred_element_type=jnp.float32)
        m_ref[...] = m_new
        return carry

    lax.fori_loop(0, n_tiles, body, 0)
    out = acc_ref[...] / jnp.maximum(l_ref[...], 1e-30)
    for j in range(HEADS_PER_GROUP):
        os_ref[:, j * NB_DH:(j + 1) * NB_DH] = out[j * q_blk:(j + 1) * q_blk, :]

    wstart = pl.multiple_of(jnp.maximum(i - WINDOW // q_blk, 0) * q_blk, q_blk)
    kt = kw_ref[pl.ds(wstart, WIN_SPAN), :]
    vt = vw_ref[pl.ds(wstart, WIN_SPAN), :]
    s = lax.dot_general(q4, kt, _NT, preferred_element_type=jnp.float32) * scale
    diff = pos - (wstart + lax.broadcasted_iota(jnp.int32, (1, WIN_SPAN), 1))
    mask = ((diff >= 0) & (diff <= WINDOW))[None]
    s = jnp.where(mask, s.reshape(HEADS_PER_GROUP, q_blk, WIN_SPAN), NEG)
    n_ghost = jnp.maximum(WINDOW - pos, 0).astype(jnp.float32)[None]
    m = jnp.max(s, axis=-1, keepdims=True)
    m = jnp.where(n_ghost > 0, jnp.maximum(m, 0.0), m)
    e = jnp.where(mask, jnp.exp(s - m), 0.0)
    p = e / jnp.maximum(jnp.sum(e, axis=-1, keepdims=True) + n_ghost * jnp.exp(-m), 1e-30)
    ow = jnp.dot(p.reshape(rows, WIN_SPAN).astype(jnp.bfloat16), vt, preferred_element_type=jnp.float32)
    for j in range(HEADS_PER_GROUP):
        ow_ref[:, j * NB_DH:(j + 1) * NB_DH] = ow[j * q_blk:(j + 1) * q_blk, :]


def _slc_win_prompt(qr2d, kpad, vpad, kwin, vwin, sel):
    t = qr2d.shape[0]
    n_slc = t // SLC_LEN
    assert t % SLC_KV_TILE == 0 and SLC_KV_TILE % SLC_LEN == 0 and t >= WIN_SPAN

    def group_major(a):
        return _bf(jnp.transpose(a, (1, 0, 2)))

    gd = HEADS_PER_GROUP * NB_DH
    rows = HEADS_PER_GROUP * Q_BLOCK
    kern = functools.partial(_slc_win_kernel, q_blk=Q_BLOCK, n_slc=n_slc, kv_tile=SLC_KV_TILE)
    kv_spec = pl.BlockSpec((None, t, NB_DH), lambda g, i: (g, 0, 0))
    o_spec = pl.BlockSpec((Q_BLOCK, gd), lambda g, i: (i, g))
    return pl.pallas_call(
        kern,
        grid=(NB_KV, t // Q_BLOCK),
        in_specs=[o_spec, kv_spec, kv_spec, kv_spec, kv_spec, pl.BlockSpec((Q_BLOCK, n_slc), lambda g, i: (i, g))],
        out_specs=[o_spec, o_spec],
        out_shape=[jax.ShapeDtypeStruct((t, NB_HEADS * NB_DH), jnp.float32)] * 2,
        scratch_shapes=[
            pltpu.VMEM((rows, LANES), jnp.float32),
            pltpu.VMEM((rows, LANES), jnp.float32),
            pltpu.VMEM((rows, NB_DH), jnp.float32),
        ],
        compiler_params=pltpu.CompilerParams(
            dimension_semantics=("arbitrary", "arbitrary"), vmem_limit_bytes=VMEM_LIMIT),
        name="slc_win_prompt",
    )(_bf(qr2d), group_major(kpad), group_major(vpad), group_major(kwin), group_major(vwin), sel)


def _split3(x):
    hi = x.astype(jnp.bfloat16)
    r = x - hi.astype(jnp.float32)
    mid = r.astype(jnp.bfloat16)
    lo = (r - mid.astype(jnp.float32)).astype(jnp.bfloat16)
    return hi, mid, lo


def _gla_kernel(xq_ref, xf_ref, xv_ref, xg_ref, lb_ref, gain_ref, o_ref, sfin_ref, st_ref, *, rows, heads, chunk):
    i = pl.program_id(1)

    @pl.when(i == 0)
    def _():
        st_ref[...] = jnp.zeros(st_ref.shape, jnp.float32)

    r_idx = lax.broadcasted_iota(jnp.int32, (rows, rows), 0)
    c_idx = lax.broadcasted_iota(jnp.int32, (rows, rows), 1)
    same = lax.shift_right_logical(r_idx, 5) == lax.shift_right_logical(c_idx, 5)
    causal = same & (c_idx <= r_idx)
    tri = jnp.where(causal, 1.0, 0.0).astype(jnp.bfloat16)
    blk = jnp.where(same, 1.0, 0.0).astype(jnp.bfloat16)
    f32 = jnp.float32
    for hh in range(heads):
        sl = slice(hh * HA_DK, (hh + 1) * HA_DK)
        lb = lb_ref[:, sl]
        qh = jax.nn.silu(xq_ref[:, sl])
        f = lb + (1.0 - lb) * jax.nn.sigmoid(xf_ref[:, sl])
        kh = 1.0 - f
        gh = jnp.log(f)
        parts = _split3(gh)
        bcum = sum(jnp.dot(tri, p, preferred_element_type=f32) for p in parts)
        tot = sum(jnp.dot(blk, p, preferred_element_type=f32) for p in parts)
        q_in = (qh * jnp.exp(bcum)).astype(jnp.bfloat16)
        k_in = (kh * jnp.exp(-bcum)).astype(jnp.bfloat16)
        k_dec = (kh * jnp.exp(tot - bcum)).astype(jnp.bfloat16)
        vb = _bf(xv_ref[:, sl])
        att = lax.dot_general(q_in, k_in, _NT, preferred_element_type=f32)
        att = jnp.where(causal, att, 0.0).astype(jnp.bfloat16)
        o_intra = jnp.dot(att, vb, preferred_element_type=f32)
        st = st_ref[hh]
        o_inter = []
        for c in range(rows // chunk):
            rs = slice(c * chunk, (c + 1) * chunk)
            o_inter.append(lax.dot_general(q_in[rs], st.astype(jnp.bfloat16), _NT, preferred_element_type=f32))
            upd = lax.dot_general(vb[rs], k_dec[rs], _TN, preferred_element_type=f32)
            st = jnp.exp(tot[c * chunk:c * chunk + 1, :]) * st + upd
        st_ref[hh] = st
        o = o_intra + jnp.concatenate(o_inter, axis=0)
        o = o * lax.rsqrt(jnp.mean(o * o, axis=-1, keepdims=True) + EPS) * gain_ref[:, sl]
        o_ref[:, sl] = (o * jax.nn.silu(xg_ref[:, sl])).astype(o_ref.dtype)

    @pl.when(i == pl.num_programs(1) - 1)
    def _():
        for hh in range(heads):
            sfin_ref[hh] = st_ref[hh].T


def _gla_prompt(proj, lb, out_gain):
    t = proj.shape[0]
    hk = HA_HEADS * HA_DK
    assert proj.shape[1] == 4 * hk and HA_DK == HA_DV
    assert t % GLA_ROWS == 0 and GLA_ROWS % HA_CHUNK == 0 and HA_CHUNK == 32 and HA_HEADS % GLA_HEADS == 0
    w = GLA_HEADS * HA_DK
    n_hb = HA_HEADS // GLA_HEADS

    def section(c):
        return pl.BlockSpec((GLA_ROWS, w), functools.partial(lambda h, i, c: (i, c * n_hb + h), c=c))

    vec_spec = pl.BlockSpec((1, w), lambda h, i: (0, h))
    kern = functools.partial(_gla_kernel, rows=GLA_ROWS, heads=GLA_HEADS, chunk=HA_CHUNK)
    return pl.pallas_call(
        kern,
        grid=(n_hb, t // GLA_ROWS),
        in_specs=[section(0), section(1), section(2), section(3), vec_spec, vec_spec],
        out_specs=[pl.BlockSpec((GLA_ROWS, w), lambda h, i: (i, h)),
                   pl.BlockSpec((GLA_HEADS, HA_DK, HA_DV), lambda h, i: (h, 0, 0))],
        out_shape=[jax.ShapeDtypeStruct((t, hk), jnp.bfloat16),
                   jax.ShapeDtypeStruct((HA_HEADS, HA_DK, HA_DV), jnp.float32)],
        scratch_shapes=[pltpu.VMEM((GLA_HEADS, HA_DV, HA_DK), jnp.float32)],
        compiler_params=pltpu.CompilerParams(
            dimension_semantics=("arbitrary", "arbitrary"), vmem_limit_bytes=VMEM_LIMIT),
        name="gla_prompt",
    )(proj, proj, proj, proj, lb.reshape(1, hk), out_gain.reshape(1, hk))


def _page_specs(n_pages):
    return [pl.BlockSpec((PAGE * NB_KV, NB_DH), functools.partial(lambda b, pt, p: (pt[b, p], 0), p=p))
            for p in range(n_pages)]


def _cmp_pages_kernel(pt_ref, *refs, n_pages):
    kp = refs[:n_pages]
    vp = refs[n_pages:2 * n_pages]
    (w1k_ref, w1v_ref, pek_ref, pev_ref, w2k_ref, w2v_ref, gain_ref, ko_ref, vo_ref, x_ref, st_ref) = refs[2 * n_pages:]
    n_sub = n_pages * SUBS_PER_PAGE
    f32 = jnp.float32
    row = lax.broadcasted_iota(jnp.int32, (NB_KV * n_sub, 1), 0)
    last = lax.rem(row, n_sub) == n_sub - 1
    for pages, w1_ref, pe_ref, w2_ref, o_ref, norm in ((kp, w1k_ref, pek_ref, w2k_ref, ko_ref, True),
                                                       (vp, w1v_ref, pev_ref, w2v_ref, vo_ref, False)):
        for p in range(n_pages):
            for n in range(SUBS_PER_PAGE):
                st_ref[pl.ds((p * SUBS_PER_PAGE + n) * SUB_PITCH, SUB_ROWS), :] = pages[p][pl.ds(n * SUB_ROWS, SUB_ROWS), :]
        for g in range(NB_KV):
            for s in range(CMP_STRIDE):
                x_ref[g * n_sub:(g + 1) * n_sub, s * NB_DH:(s + 1) * NB_DH] = _bf(
                    st_ref[pl.ds(s * NB_KV + g, n_sub, stride=SUB_PITCH), :])
        w1 = w1_ref[...]
        a01 = jnp.dot(x_ref[...], w1, preferred_element_type=f32)
        c = jnp.dot(_bf(pe_ref[...]), w1, preferred_element_type=f32)
        a0 = a01[:, :NB_DH] + c[0:1, :NB_DH]
        a1 = a01[:, NB_DH:] + c[1:2, NB_DH:]
        pre = a0 + pltpu.roll(a1, NB_KV * n_sub - 1, 0)
        y = jnp.dot(_bf(jax.nn.gelu(pre)), w2_ref[...], preferred_element_type=f32)
        if norm:
            y = y * lax.rsqrt(jnp.mean(y * y, axis=-1, keepdims=True) + EPS) * gain_ref[...]
        o_ref[...] = jnp.where(last, 0.0, y).astype(o_ref.dtype)


def _cmp_pages(page_table, cache_k, cache_v, pe_k, pe_v, w1_k, w1_v, w2_k, w2_v, gain):
    nb, n_pages = page_table.shape
    n_phys = cache_k.shape[0]
    assert cache_k.shape[1:] == (PAGE, NB_KV, NB_DH) and CMP_LEN == 2 * CMP_STRIDE
    n_sub = n_pages * SUBS_PER_PAGE
    kd = CMP_STRIDE * NB_DH

    def w1cat(w1):
        return _bf(jnp.concatenate([w1[:CMP_STRIDE].reshape(kd, NB_DH), w1[CMP_STRIDE:].reshape(kd, NB_DH)], axis=1))

    def pecat(pe):
        z = jnp.zeros((8, kd), jnp.float32)
        return z.at[0].set(pe[:CMP_STRIDE].reshape(kd)).at[1].set(pe[CMP_STRIDE:].reshape(kd))

    def full(*shape):
        return pl.BlockSpec(shape, lambda b, pt: (0,) * len(shape))

    out_spec = pl.BlockSpec((None, NB_KV * n_sub, NB_DH), lambda b, pt: (b, 0, 0))
    ck = cache_k.reshape(n_phys * PAGE * NB_KV, NB_DH)
    cv = cache_v.reshape(n_phys * PAGE * NB_KV, NB_DH)
    grid_spec = pltpu.PrefetchScalarGridSpec(
        num_scalar_prefetch=1,
        grid=(nb,),
        in_specs=_page_specs(n_pages) * 2 + [full(kd, 2 * NB_DH), full(kd, 2 * NB_DH), full(8, kd), full(8, kd),
                                            full(NB_DH, NB_DH), full(NB_DH, NB_DH), full(1, NB_DH)],
        out_specs=[out_spec, out_spec],
        scratch_shapes=[pltpu.VMEM((NB_KV * n_sub, kd), jnp.bfloat16),
                        pltpu.VMEM((n_sub * SUB_PITCH, NB_DH), jnp.float32)],
    )
    return pl.pallas_call(
        functools.partial(_cmp_pages_kernel, n_pages=n_pages),
        grid_spec=grid_spec,
        out_shape=[jax.ShapeDtypeStruct((nb, NB_KV * n_sub, NB_DH), jnp.bfloat16)] * 2,
        compiler_params=pltpu.CompilerParams(dimension_semantics=("arbitrary",), vmem_limit_bytes=VMEM_LIMIT),
        name="cmp_pages",
    )(page_table, *([ck] * n_pages), *([cv] * n_pages), w1cat(w1_k), w1cat(w1_v), pecat(pe_k), pecat(pe_v),
      _bf(w2_k), _bf(w2_v), gain.reshape(1, NB_DH))


def _softmax_with_tail(s, mask, s_tail, ok_tail):
    m = jnp.max(s, axis=-1, keepdims=True)
    for sc in s_tail:
        m = jnp.maximum(m, sc)
    e = jnp.where(mask, jnp.exp(s - m), 0.0)
    e_tail = [jnp.where(ok, jnp.exp(sc - m), 0.0) for ok, sc in zip(ok_tail, s_tail)]
    den = jnp.sum(e, axis=-1, keepdims=True)
    for et in e_tail:
        den = den + et
    inv = 1.0 / jnp.maximum(den, 1e-30)
    return e * inv, [et * inv for et in e_tail]


def _nsa_sample_kernel(pt_ref, *refs, n_pages, n_new, win_keep):
    ksp = refs[:n_pages]
    vsp = refs[n_pages:2 * n_pages]
    (q_ref, qr_ref, kcb_ref, vcb_ref, ksn_ref, vsn_ref, kwc_ref, vwc_ref, kwn_ref, vwn_ref, ov_ref,
     oc_ref, os_ref, ow_ref, kwo_ref, vwo_ref) = refs[2 * n_pages:]
    f32 = jnp.float32
    past = n_pages * PAGE
    n_sub = past // CMP_STRIDE
    n_cmp = n_sub - CMP_LEN // CMP_STRIDE + 1
    n_slc = -(-(past + n_new) // SLC_LEN)
    new_blk = past // SLC_LEN
    assert (past % SLC_LEN) + n_new <= SLC_LEN and n_sub == LANES and n_slc <= LANES
    rows = HEADS_PER_GROUP * n_new
    all_rows = NB_KV * rows
    sel_rows = NB_KV * n_new
    scale = NB_DH ** -0.5
    groups = range(NB_KV)

    def grp_rows(g, n_tok):
        return pl.ds(g, n_tok, stride=NB_KV)

    def stack(parts):
        return jnp.concatenate(parts, axis=0)

    t_row = lax.rem(lax.broadcasted_iota(jnp.int32, (all_rows, 1), 0), n_new)
    pos = past + t_row
    pos_t = past + lax.rem(lax.broadcasted_iota(jnp.int32, (sel_rows, 1), 0), n_new)
    lane = lax.broadcasted_iota(jnp.int32, (1, LANES), 1)
    q = [q_ref[g] for g in groups]
    qr = [qr_ref[g] for g in groups]
    qrf = stack(qr).astype(f32)

    def tail_scores(new_ref):
        kb = _bf(new_ref[...]).astype(f32)
        out = []
        for i in range(n_new):
            ki = stack([jnp.broadcast_to(kb[i * NB_KV + g:i * NB_KV + g + 1, :], (rows, NB_DH)) for g in groups])
            out.append(jnp.sum(qrf * ki, axis=-1, keepdims=True) * scale)
        return out

    def tail_values(p_tail, new_ref, g):
        vb = _bf(new_ref[...]).astype(f32)
        acc = _bf(p_tail[0][g * rows:(g + 1) * rows]).astype(f32) * vb[g:g + 1, :]
        for i in range(1, n_new):
            acc = acc + _bf(p_tail[i][g * rows:(g + 1) * rows]).astype(f32) * vb[i * NB_KV + g:i * NB_KV + g + 1, :]
        return acc

    mask_c = (lane * CMP_STRIDE + (CMP_LEN - 1) <= pos) & (lane < n_cmp)
    s = stack([lax.dot_general(q[g], kcb_ref[g * n_sub:(g + 1) * n_sub, :], _NT, preferred_element_type=f32)
               for g in groups]) * scale
    s = jnp.where(mask_c, s, NEG)
    m = jnp.max(s, axis=-1, keepdims=True)
    e = jnp.where(mask_c, jnp.exp(s - m), 0.0)
    p = e / jnp.maximum(jnp.sum(e, axis=-1, keepdims=True), 1e-30)
    pb = _bf(p)
    for g in groups:
        oc_ref[g] = jnp.dot(pb[g * rows:(g + 1) * rows], vcb_ref[g * n_sub:(g + 1) * n_sub, :],
                            preferred_element_type=f32)
    psum = []
    for g in groups:
        acc = p[g * rows:g * rows + n_new]
        for j in range(1, HEADS_PER_GROUP):
            acc = acc + p[g * rows + j * n_new:g * rows + (j + 1) * n_new]
        psum.append(acc)
    psum = stack(psum)
    hi = _bf(psum)
    lo = _bf(psum - hi.astype(f32))
    ov = ov_ref[...]
    imp = jnp.dot(hi, ov, preferred_element_type=f32) + jnp.dot(lo, ov, preferred_element_type=f32)
    cur = lax.shift_right_logical(pos_t, 6)
    forced = (lane == 0) | (lane == cur) | (lane == cur - 1)
    future = lane * SLC_LEN > pos_t
    score = jnp.where(future, NEG, jnp.where(forced, BIG, imp))
    score = jnp.where(lane >= n_slc, REMOVED, score)
    lane_f = lax.broadcasted_iota(jnp.int32, (sel_rows, LANES), 1).astype(f32)
    sel = jnp.zeros((sel_rows, LANES), f32)
    for _ in range(min(N_SELECT, n_slc)):
        mx = jnp.max(score, axis=-1, keepdims=True)
        idx = jnp.min(jnp.where(score == mx, lane_f, float(LANES)), axis=-1, keepdims=True)
        hit = lane_f == idx
        sel = jnp.where(hit, 1.0, sel)
        score = jnp.where(hit, REMOVED, score)
    sel_r = stack([sel[g * n_new:(g + 1) * n_new] for g in groups for _ in range(HEADS_PER_GROUP)])
    expand = jnp.where(lax.broadcasted_iota(jnp.int32, (LANES, past), 0)
                       == lax.shift_right_logical(lax.broadcasted_iota(jnp.int32, (LANES, past), 1), 6),
                       1.0, 0.0).astype(jnp.bfloat16)
    picked = jnp.dot(_bf(sel_r), expand, preferred_element_type=f32) > 0.5
    s_all = stack([jnp.concatenate([lax.dot_general(qr[g], _bf(ksp[pg][grp_rows(g, PAGE), :]), _NT,
                                                    preferred_element_type=f32)
                                    for pg in range(n_pages)], axis=1) for g in groups]) * scale
    s_all = jnp.where(picked, s_all, NEG)
    new_sel = sel_r[:, new_blk:new_blk + 1] > 0.5
    ok_new = [new_sel & (t_row >= i) for i in range(n_new)]
    s_new = [jnp.where(ok, sc, NEG) for ok, sc in zip(ok_new, tail_scores(ksn_ref))]
    p_all, p_new = _softmax_with_tail(s_all, picked, s_new, ok_new)
    p_all = _bf(p_all)
    for g in groups:
        acc = tail_values(p_new, vsn_ref, g)
        for pg in range(n_pages):
            acc = acc + jnp.dot(p_all[g * rows:(g + 1) * rows, pg * PAGE:(pg + 1) * PAGE],
                                _bf(vsp[pg][grp_rows(g, PAGE), :]), preferred_element_type=f32)
        os_ref[g] = acc
    mask_w = lax.broadcasted_iota(jnp.int32, (1, win_keep), 1) >= t_row
    s_w = stack([lax.dot_general(qr[g], _bf(kwc_ref[grp_rows(g, win_keep), :]), _NT, preferred_element_type=f32)
                 for g in groups]) * scale
    s_w = jnp.where(mask_w, s_w, NEG)
    ok_wn = [t_row >= i for i in range(n_new)]
    s_wn = [jnp.where(ok, sc, NEG) for ok, sc in zip(ok_wn, tail_scores(kwn_ref))]
    p_w, p_wn = _softmax_with_tail(s_w, mask_w, s_wn, ok_wn)
    p_w = _bf(p_w)
    for g in groups:
        ow_ref[g] = (tail_values(p_wn, vwn_ref, g)
                     + jnp.dot(p_w[g * rows:(g + 1) * rows], _bf(vwc_ref[grp_rows(g, win_keep), :]),
                               preferred_element_type=f32))
    new_rows = n_new * NB_KV
    kept_rows = win_keep * NB_KV - new_rows
    for src_ref, add_ref, dst_ref in ((kwc_ref, kwn_ref, kwo_ref), (vwc_ref, vwn_ref, vwo_ref)):
        dst_ref[0:kept_rows, :] = src_ref[new_rows:new_rows + kept_rows, :]
        dst_ref[kept_rows:kept_rows + new_rows, :] = add_ref[...]


def _nsa_sample(page_table, q, qr, kcb, vcb, cache_sk, cache_sv, ks_new, vs_new, cache_wk, cache_wv, kw_new, vw_new):
    nb, n_pages = page_table.shape
    t = q.shape[1]
    n_phys = cache_sk.shape[0]
    keep = cache_wk.shape[1]
    assert keep == WINDOW and (t * NB_KV) % 8 == 0 and t < keep
    rows = HEADS_PER_GROUP * t
    n_sub = n_pages * SUBS_PER_PAGE

    def q_rows(a):
        a = a.reshape(nb, t, NB_KV, HEADS_PER_GROUP, NB_DH)
        return _bf(jnp.transpose(a, (0, 2, 3, 1, 4)).reshape(nb, NB_KV, rows, NB_DH))

    def q_unrows(a):
        a = a.reshape(nb, NB_KV, HEADS_PER_GROUP, t, NB_DH)
        return jnp.transpose(a, (0, 3, 1, 2, 4)).reshape(nb, t, NB_HEADS, NB_DH)

    def per_b(*shape):
        return pl.BlockSpec((None,) + shape, lambda b, pt: (b,) + (0,) * len(shape))

    def rows_view(a):
        return a.reshape(a.shape[0], a.shape[1] * NB_KV, NB_DH)

    n_cmp = n_sub - CMP_LEN // CMP_STRIDE + 1
    q_spec = per_b(NB_KV, rows, NB_DH)
    win_spec = per_b(keep * NB_KV, NB_DH)
    new_spec = per_b(t * NB_KV, NB_DH)
    cmp_spec = per_b(NB_KV * n_sub, NB_DH)
    in_specs = (_page_specs(n_pages) * 2
                + [q_spec, q_spec, cmp_spec, cmp_spec, new_spec, new_spec, win_spec, win_spec, new_spec, new_spec,
                   pl.BlockSpec((LANES, LANES), lambda b, pt: (0, 0))])
    grid_spec = pltpu.PrefetchScalarGridSpec(
        num_scalar_prefetch=1, grid=(nb,), in_specs=in_specs, out_specs=[q_spec] * 3 + [win_spec] * 2)
    sk = cache_sk.reshape(n_phys * PAGE * NB_KV, NB_DH)
    sv = cache_sv.reshape(n_phys * PAGE * NB_KV, NB_DH)
    o_c, o_s, o_w, wk_next, wv_next = pl.pallas_call(
        functools.partial(_nsa_sample_kernel, n_pages=n_pages, n_new=t, win_keep=keep),
        grid_spec=grid_spec,
        out_shape=([jax.ShapeDtypeStruct((nb, NB_KV, rows, NB_DH), jnp.float32)] * 3
                   + [jax.ShapeDtypeStruct((nb, keep * NB_KV, NB_DH), jnp.float32)] * 2),
        compiler_params=pltpu.CompilerParams(dimension_semantics=("arbitrary",), vmem_limit_bytes=VMEM_LIMIT),
        name="nsa_sample",
    )(page_table, *([sk] * n_pages), *([sv] * n_pages), q_rows(q), q_rows(qr), kcb, vcb,
      rows_view(ks_new), rows_view(vs_new), rows_view(cache_wk), rows_view(cache_wv), rows_view(kw_new),
      rows_view(vw_new), _overlap_matrix(LANES, n_cmp, LANES))
    return (q_unrows(o_c), q_unrows(o_s), q_unrows(o_w),
            wk_next.reshape(cache_wk.shape), wv_next.reshape(cache_wv.shape))


def _mm_kernel(x_ref, w_ref, o_ref, wb_ref):
    @pl.when(pl.program_id(1) == 0)
    def _():
        wb_ref[...] = w_ref[...].astype(jnp.bfloat16)

    o_ref[...] = jnp.dot(x_ref[...], wb_ref[...], preferred_element_type=jnp.float32)


def _mm_tiles(m, k, n):
    tm = min(m, 1024 if k <= 2048 else 512)
    tn = min(n, 512 if k <= 2048 else 256)
    return tm, tn


def _mm(x, w):
    lead = x.shape[:-1]
    k = x.shape[-1]
    n = w.shape[1]
    x2 = _bf(x.reshape(-1, k))
    m = x2.shape[0]
    tm, tn = _mm_tiles(m, k, n)
    assert m % tm == 0
    out = pl.pallas_call(
        _mm_kernel,
        grid=(pl.cdiv(n, tn), m // tm),
        in_specs=[pl.BlockSpec((tm, k), lambda j, i: (i, 0)),
                  pl.BlockSpec((k, tn), lambda j, i: (0, j))],
        out_specs=pl.BlockSpec((tm, tn), lambda j, i: (i, j)),
        out_shape=jax.ShapeDtypeStruct((m, n), jnp.float32),
        scratch_shapes=[pltpu.VMEM((k, tn), jnp.bfloat16)],
        compiler_params=pltpu.CompilerParams(
            dimension_semantics=("arbitrary", "arbitrary"), vmem_limit_bytes=VMEM_LIMIT),
        name="mm",
    )(x2, w)
    return out.reshape(*lead, n)


def _rmsnorm(x, g):
    xf = x.astype(jnp.float32)
    y = xf * lax.rsqrt(jnp.mean(xf * xf, axis=-1, keepdims=True) + EPS)
    return (y * g.astype(jnp.float32)).astype(x.dtype)


def _rope(x, pos):
    half = x.shape[-1] // 2
    inv = ROPE_THETA ** (-jnp.arange(half, dtype=jnp.float32) / half)
    ang = pos.astype(jnp.float32)[:, None] * inv[None, :]
    cos, sin = jnp.cos(ang)[:, None, :], jnp.sin(ang)[:, None, :]
    xf = x.astype(jnp.float32)
    x1, x2 = xf[..., :half], xf[..., half:]
    return jnp.concatenate([x1 * cos - x2 * sin, x2 * cos + x1 * sin], axis=-1).astype(x.dtype)


def _gla_chunked(q, k, v, logf, s0):
    b, t, h, dk = q.shape
    dv = v.shape[-1]
    c = math.gcd(t, HA_CHUNK)
    n = t // c

    def to_chunks(a):
        return jnp.transpose(a.reshape(b, n, c, h, a.shape[-1]), (1, 0, 3, 2, 4))

    qc, kc, vc, gc = to_chunks(q), to_chunks(k), to_chunks(v), to_chunks(logf)
    bcum = jnp.cumsum(gc, axis=3)
    blast = bcum[:, :, :, -1, :]
    q_in = qc * jnp.exp(bcum)
    k_in = kc * jnp.exp(-bcum)
    k_dec = kc * jnp.exp(blast[:, :, :, None, :] - bcum)
    causal = jnp.tril(jnp.ones((c, c), dtype=bool))
    att = jnp.where(causal, jnp.einsum('nbhik,nbhjk->nbhij', q_in, k_in), 0.0)
    o_intra = jnp.einsum('nbhij,nbhjv->nbhiv', att, vc)

    def step(s, inp):
        qi, kd, vi, dl = inp
        o_inter = jnp.einsum('bhik,bhkv->bhiv', qi, s)
        s = jnp.exp(dl)[..., None] * s + jnp.einsum('bhik,bhiv->bhkv', kd, vi)
        return s, o_inter

    s_final, o_inter = lax.scan(step, s0.astype(jnp.float32), (q_in, k_dec, vc, blast))
    o = o_intra + o_inter
    return jnp.transpose(o, (1, 0, 3, 2, 4)).reshape(b, t, h, dv), s_final


def _hgrn2_layer(xn, s0, lb, w_in, out_gain, w_out):
    b, t, _ = xn.shape
    hk = HA_HEADS * HA_DK
    hv = HA_HEADS * HA_DV
    proj = _mm(xn, w_in)
    if s0 is None:
        assert b == 1
        o, s_new = _gla_prompt(proj[0], lb, out_gain)
        return _mm(o[None], w_out), s_new[None]
    q = jax.nn.silu(proj[..., :hk])
    f = lb + (1.0 - lb) * jax.nn.sigmoid(proj[..., hk:2 * hk])
    i_in = proj[..., 2 * hk:2 * hk + hv]
    gate = proj[..., 2 * hk + hv:]
    shp_k = (b, t, HA_HEADS, HA_DK)
    shp_v = (b, t, HA_HEADS, HA_DV)
    o, s_new = _gla_chunked(q.reshape(shp_k), (1.0 - f).reshape(shp_k), i_in.reshape(shp_v),
                            jnp.log(f).reshape(shp_k), s0)
    o = _rmsnorm(o, out_gain.reshape(HA_HEADS, HA_DV)) * jax.nn.silu(gate).reshape(shp_v)
    return _mm(o.reshape(b, t, hv), w_out), s_new


def _conv_ffn(xn, prev, w_up, conv_w, conv_b, w_down):
    t = xn.shape[1]
    up = _mm(xn, w_up)
    u, gt = up[..., :D_FF], up[..., D_FF:]
    ext = jnp.concatenate([prev.astype(gt.dtype), gt], axis=1)
    conv = conv_b
    for j in range(CONV_W):
        conv = conv + conv_w[j] * ext[:, j:j + t]
    return _mm(jax.nn.silu(conv) * u, w_down), ext[:, t:]


def _ple(h, p, gain, w_p, w_g):
    gate = jax.nn.sigmoid(_mm(_rmsnorm(h, gain), w_g))
    return h + _mm(p, w_p) * gate


def _compress(rows, pe, w1, w2):
    b, l, g, d = rows.shape
    n_sub = l // CMP_STRIDE
    r = CMP_LEN // CMP_STRIDE
    n_cmp = n_sub - r + 1
    sub = rows[:, :n_sub * CMP_STRIDE].reshape(b, n_sub, CMP_STRIDE, g, d)
    pre = None
    for j in range(r):
        sl = slice(j * CMP_STRIDE, (j + 1) * CMP_STRIDE)
        part = jnp.einsum('bnsgd,sde->bnge', sub + pe[sl][None, None, :, None, :], w1[sl])[:, j:j + n_cmp]
        pre = part if pre is None else pre + part
    return jnp.einsum('bnge,ef->bngf', jax.nn.gelu(pre), w2)


def _nsa_layer(xn, pos, shared, w_in, q_gain, w_out, banded):
    b, t, _ = xn.shape
    hd = NB_HEADS * NB_DH
    proj = _mm(xn, w_in)
    q = _rmsnorm(proj[..., :hd].reshape(b, t, NB_HEADS, NB_DH), q_gain)
    gates = jax.nn.sigmoid(proj[..., hd:].astype(jnp.float32)).reshape(b, t, 3, NB_HEADS)
    qr = _rope(q, pos)
    win_next = None
    if banded:
        assert b == 1
        kcb, vcb, kpad, vpad, win_k, win_v = shared
        o_c2, sel = _cmp_topk_prompt(q.reshape(t, hd), kcb[0], vcb[0])
        o_s2, o_w2 = _slc_win_prompt(qr.reshape(t, hd), kpad[0], vpad[0], win_k[0], win_v[0], sel)
        o_c = o_c2.reshape(b, t, NB_HEADS, NB_DH)
        o_s = o_s2.reshape(b, t, NB_HEADS, NB_DH)
        o_w = o_w2.reshape(b, t, NB_HEADS, NB_DH)
    else:
        o_c, o_s, o_w, wk_next, wv_next = _nsa_sample(shared[0], q, qr, *shared[1:])
        win_next = (wk_next, wv_next)
    o = (gates[:, :, 0, :, None] * o_c + gates[:, :, 1, :, None] * o_s + gates[:, :, 2, :, None] * o_w)
    return _mm(o.reshape(b, t, hd), w_out), win_next


def _shared_kv(h, pos, past, wt):
    b, t, _ = h.shape
    kv = _mm(_rmsnorm(h, wt['kv_norm']), wt['kv_w']).reshape(b, t, 6, NB_KV, NB_DH)
    k_cmp, v_cmp = kv[:, :, 0], kv[:, :, 1]
    k_slc = _rope(_rmsnorm(kv[:, :, 2], wt['k_norm_slc']), pos)
    v_slc = kv[:, :, 3]
    k_win = _rope(_rmsnorm(kv[:, :, 4], wt['k_norm_win']), pos)
    v_win = kv[:, :, 5]
    if past is None:
        keep = min(WINDOW, t)
        kcb = _rmsnorm(_compress(k_cmp, wt['cmp_pe_k'], wt['cmp_w1_k'], wt['cmp_w2_k']), wt['k_norm_cmp'])
        vcb = _compress(v_cmp, wt['cmp_pe_v'], wt['cmp_w1_v'], wt['cmp_w2_v'])
        assert t % SLC_LEN == 0
        shared = (kcb, vcb, k_slc, v_slc, k_win, v_win)
        rows = (k_cmp, v_cmp, k_slc, v_slc, k_win[:, -keep:], v_win[:, -keep:])
    else:
        page_table, ckc, cvc, cks, cvs, cwk, cwv = past
        assert t < CMP_STRIDE and (page_table.shape[1] * PAGE) % CMP_STRIDE == 0
        kcb, vcb = _cmp_pages(page_table, ckc, cvc, wt['cmp_pe_k'], wt['cmp_pe_v'], wt['cmp_w1_k'], wt['cmp_w1_v'],
                              wt['cmp_w2_k'], wt['cmp_w2_v'], wt['k_norm_cmp'])
        shared = (page_table, kcb, vcb, cks, cvs, k_slc, v_slc, cwk, cwv, k_win, v_win)
        rows = (k_cmp, v_cmp, k_slc, v_slc, None, None)
    return shared, rows


def _trunk(x, p, start, s_hgrn, s_conv, past, wt):
    b, t, _ = x.shape
    pos = start + jnp.arange(t)
    banded = past is None
    lb = jnp.cumsum(jax.nn.softmax(wt['a_lb_logits'].astype(jnp.float32), axis=0), axis=0)
    h = x
    hgrn_out, conv_out = [], []
    shared, rows = None, None
    for i in range(DEPTH):
        hn = _rmsnorm(h, wt['norm_mix'][i])
        if i < N_A:
            y, s = _hgrn2_layer(hn, None if s_hgrn is None else s_hgrn[i], lb[i], wt['a_w_in'][i],
                                wt['a_out_norm'][i], wt['a_w_out'][i])
            hgrn_out.append(s)
        else:
            j = i - N_A
            y, win_next = _nsa_layer(hn, pos, shared, wt['b_w_in'][j], wt['b_q_norm'][j], wt['b_w_out'][j], banded)
            if win_next is not None:
                assert N_B == 1
                rows = rows[:4] + win_next
        h = h + y
        f, cs = _conv_ffn(_rmsnorm(h, wt['norm_ffn'][i]), s_conv[i], wt['ffn_w_up'][i], wt['ffn_conv_w'][i],
                          wt['ffn_conv_b'][i], wt['ffn_w_down'][i])
        conv_out.append(cs)
        h = h + f
        h = _ple(h, p[i], wt['norm_ple'][i], wt['ple_w_p'][i], wt['ple_w_g'][i])
        if i == N_A - 1:
            shared, rows = _shared_kv(h, pos, past, wt)
    return h, jnp.stack(hgrn_out), jnp.stack(conv_out), rows


def kernel(x_prompt, x_sample, state_hgrn, state_conv, cache_cmp_k, cache_cmp_v, cache_slc_k, cache_slc_v,
           cache_win_k, cache_win_v, page_table, p_prompt, p_sample, norm_mix, norm_ffn, norm_ple,
           a_w_in, a_lb_logits, a_out_norm, a_w_out, kv_norm, kv_w, k_norm_cmp, k_norm_slc, k_norm_win,
           cmp_pe_k, cmp_pe_v, cmp_w1_k, cmp_w2_k, cmp_w1_v, cmp_w2_v, b_w_in, b_q_norm, b_w_out,
           ffn_w_up, ffn_conv_w, ffn_conv_b, ffn_w_down, ple_w_p, ple_w_g):
    wt = {
        'norm_mix': norm_mix, 'norm_ffn': norm_ffn, 'norm_ple': norm_ple,
        'a_w_in': a_w_in, 'a_lb_logits': a_lb_logits, 'a_out_norm': a_out_norm, 'a_w_out': a_w_out,
        'kv_norm': kv_norm, 'kv_w': kv_w, 'k_norm_cmp': k_norm_cmp, 'k_norm_slc': k_norm_slc,
        'k_norm_win': k_norm_win, 'cmp_pe_k': cmp_pe_k, 'cmp_pe_v': cmp_pe_v, 'cmp_w1_k': cmp_w1_k,
        'cmp_w2_k': cmp_w2_k, 'cmp_w1_v': cmp_w1_v, 'cmp_w2_v': cmp_w2_v,
        'b_w_in': b_w_in, 'b_q_norm': b_q_norm, 'b_w_out': b_w_out,
        'ffn_w_up': ffn_w_up, 'ffn_conv_w': ffn_conv_w, 'ffn_conv_b': ffn_conv_b, 'ffn_w_down': ffn_w_down,
        'ple_w_p': ple_w_p, 'ple_w_g': ple_w_g,
    }
    bp, past_len = x_prompt.shape[0], page_table.shape[1] * cache_cmp_k.shape[1]
    s_conv0 = jnp.zeros((DEPTH, bp, CONV_W - 1, D_FF), x_prompt.dtype)
    y_prompt, hgrn_p, conv_p, rows_p = _trunk(x_prompt, p_prompt, 0, None, s_conv0, None, wt)
    past = (page_table, cache_cmp_k, cache_cmp_v, cache_slc_k, cache_slc_v, cache_win_k, cache_win_v)
    y_sample, hgrn_s, conv_s, rows_s = _trunk(x_sample, p_sample, past_len, state_hgrn, state_conv, past, wt)
    kc_p, vc_p, ks_p, vs_p, wk_p, wv_p = rows_p
    kc_s, vc_s, ks_s, vs_s, wk_s, wv_s = rows_s
    return (y_prompt, y_sample, hgrn_p, hgrn_s, conv_p, conv_s, kc_p, vc_p, ks_p, vs_p,
            kc_s, vc_s, ks_s, vs_s, wk_p, wv_p, wk_s, wv_s)
```

```python
import functools
import math

import jax
import jax.numpy as jnp
from jax import lax
from jax.experimental import pallas as pl
from jax.experimental.pallas import tpu as pltpu

D_MODEL = 2048
DEPTH = 2
N_A = DEPTH // 2
N_B = DEPTH - N_A
EPS = 1e-6
HA_DK = 128
HA_HEADS = D_MODEL // HA_DK
HA_DV = D_MODEL // HA_HEADS
HA_CHUNK = 32
NB_DH = 128
NB_HEADS = D_MODEL // NB_DH
NB_KV = 4
HEADS_PER_GROUP = NB_HEADS // NB_KV
CMP_LEN = 32
CMP_STRIDE = 16
SLC_LEN = 64
N_SELECT = 16
WINDOW = 512
Q_BLOCK = 128
ROPE_THETA = 10000.0
D_FF = ((8 * D_MODEL // 3 + 127) // 128) * 128
CONV_W = 3
NEG = -1e30
BIG = 1e30
REMOVED = -3.0e38
LANES = 128
SLC_KV_TILE = 512
WIN_SPAN = WINDOW + Q_BLOCK
GLA_ROWS = 256
GLA_HEADS = 4
PAGE = 128
SUBS_PER_PAGE = PAGE // CMP_STRIDE
SUB_ROWS = CMP_STRIDE * NB_KV
SUB_PITCH = SUB_ROWS + 8
VMEM_LIMIT = 48 * 1024 * 1024
_NT = (((1,), (1,)), ((), ()))
_TN = (((0,), (0,)), ((), ()))


def _bf(x):
    return x.astype(jnp.bfloat16)


def _cmp_topk_kernel(q_ref, k_ref, v_ref, ov_ref, o_ref, sel_ref, *, q_blk, n_cmp_pad, n_slc, n_sel):
    i = pl.program_id(0)
    f32 = jnp.float32
    pos = i * q_blk + lax.broadcasted_iota(jnp.int32, (q_blk, 1), 0)
    kend = lax.broadcasted_iota(jnp.int32, (1, n_cmp_pad), 1) * CMP_STRIDE + (CMP_LEN - 1)
    mask_c = kend <= pos
    scale = NB_DH ** -0.5
    ov = ov_ref[...]
    imp = []
    for g in range(NB_KV):
        k = k_ref[g]
        v = v_ref[g]
        psum = jnp.zeros((q_blk, n_cmp_pad), f32)
        for j in range(HEADS_PER_GROUP):
            cs = slice((g * HEADS_PER_GROUP + j) * NB_DH, (g * HEADS_PER_GROUP + j + 1) * NB_DH)
            s = lax.dot_general(_bf(q_ref[:, cs]), k, _NT, preferred_element_type=f32) * scale
            s = jnp.where(mask_c, s, NEG)
            m = jnp.max(s, axis=-1, keepdims=True)
            e = jnp.where(mask_c, jnp.exp(s - m), 0.0)
            p = e / jnp.maximum(jnp.sum(e, axis=-1, keepdims=True), 1e-30)
            o_ref[:, cs] = jnp.dot(_bf(p), v, preferred_element_type=f32)
            psum = psum + p
        hi = _bf(psum)
        lo = _bf(psum - hi.astype(f32))
        imp.append(jnp.dot(hi, ov, preferred_element_type=f32) + jnp.dot(lo, ov, preferred_element_type=f32))
    imp = jnp.concatenate(imp, axis=0)
    pos_r = jnp.concatenate([pos] * NB_KV, axis=0)
    sidx = lax.broadcasted_iota(jnp.int32, (1, n_slc), 1)
    cur = lax.shift_right_logical(pos_r, 6)
    forced = (sidx == 0) | (sidx == cur) | (sidx == cur - 1)
    future = sidx * SLC_LEN > pos_r
    score = jnp.where(future, NEG, jnp.where(forced, BIG, imp))
    lane = lax.broadcasted_iota(jnp.int32, (NB_KV * q_blk, n_slc), 1).astype(f32)
    sel = jnp.zeros((NB_KV * q_blk, n_slc), f32)
    for _ in range(n_sel):
        m = jnp.max(score, axis=-1, keepdims=True)
        idx = jnp.min(jnp.where(score == m, lane, float(n_slc)), axis=-1, keepdims=True)
        hit = lane == idx
        sel = jnp.where(hit, 1.0, sel)
        score = jnp.where(hit, REMOVED, score)
    for g in range(NB_KV):
        sel_ref[:, g * n_slc:(g + 1) * n_slc] = sel[g * q_blk:(g + 1) * q_blk].astype(sel_ref.dtype)


def _overlap_matrix(n_cmp_rows, n_cmp, n_slc_cols):
    cstart = jnp.arange(n_cmp_rows) * CMP_STRIDE
    sstart = jnp.arange(n_slc_cols) * SLC_LEN
    return _bf((cstart[:, None] <= sstart[None, :] + (SLC_LEN - 1))
               & (cstart[:, None] + (CMP_LEN - 1) >= sstart[None, :])
               & (jnp.arange(n_cmp_rows)[:, None] < n_cmp))


def _cmp_topk_prompt(q2d, kcb, vcb):
    t = q2d.shape[0]
    n_cmp = kcb.shape[0]
    n_cmp_pad = -(-n_cmp // LANES) * LANES
    n_slc = t // SLC_LEN
    assert SLC_LEN == 64 and n_slc == LANES and t % Q_BLOCK == 0
    n_sel = min(N_SELECT, n_slc)
    padc = ((0, 0), (0, n_cmp_pad - n_cmp), (0, 0))
    k = _bf(jnp.pad(jnp.transpose(kcb, (1, 0, 2)), padc))
    v = _bf(jnp.pad(jnp.transpose(vcb, (1, 0, 2)), padc))
    hd = NB_HEADS * NB_DH
    kern = functools.partial(_cmp_topk_kernel, q_blk=Q_BLOCK, n_cmp_pad=n_cmp_pad, n_slc=n_slc, n_sel=n_sel)
    kv_spec = pl.BlockSpec((NB_KV, n_cmp_pad, NB_DH), lambda i: (0, 0, 0))
    return pl.pallas_call(
        kern,
        grid=(t // Q_BLOCK,),
        in_specs=[pl.BlockSpec((Q_BLOCK, hd), lambda i: (i, 0)), kv_spec, kv_spec,
                  pl.BlockSpec((n_cmp_pad, n_slc), lambda i: (0, 0))],
        out_specs=[pl.BlockSpec((Q_BLOCK, hd), lambda i: (i, 0)),
                   pl.BlockSpec((Q_BLOCK, NB_KV * n_slc), lambda i: (i, 0))],
        out_shape=[jax.ShapeDtypeStruct((t, hd), jnp.float32),
                   jax.ShapeDtypeStruct((t, NB_KV * n_slc), jnp.bfloat16)],
        compiler_params=pltpu.CompilerParams(dimension_semantics=("arbitrary",), vmem_limit_bytes=VMEM_LIMIT),
        name="cmp_topk_prompt",
    )(q2d, k, v, _overlap_matrix(n_cmp_pad, n_cmp, n_slc))


def _slc_win_kernel(q_ref, ks_ref, vs_ref, kw_ref, vw_ref, sel_ref, os_ref, ow_ref, m_ref, l_ref, acc_ref,
                    *, q_blk, n_slc, kv_tile):
    i = pl.program_id(1)
    rows = HEADS_PER_GROUP * q_blk
    blocks_per_tile = kv_tile // SLC_LEN
    reps = kv_tile // LANES
    scale = NB_DH ** -0.5
    q4 = jnp.concatenate([q_ref[:, j * NB_DH:(j + 1) * NB_DH] for j in range(HEADS_PER_GROUP)], axis=0)
    sel = sel_ref[...]
    pos = i * q_blk + lax.broadcasted_iota(jnp.int32, (q_blk, 1), 0)
    m_ref[...] = jnp.full(m_ref.shape, NEG, jnp.float32)
    l_ref[...] = jnp.zeros(l_ref.shape, jnp.float32)
    acc_ref[...] = jnp.zeros(acc_ref.shape, jnp.float32)
    blk_row = lax.broadcasted_iota(jnp.int32, (n_slc, kv_tile), 0)
    blk_col = lax.shift_right_logical(lax.broadcasted_iota(jnp.int32, (n_slc, kv_tile), 1), 6)
    key_off = lax.broadcasted_iota(jnp.int32, (1, kv_tile), 1)
    n_tiles = lax.div(i * q_blk + (q_blk - 1), kv_tile) + 1

    def body(j, carry):
        start = pl.multiple_of(j * kv_tile, kv_tile)
        kt = ks_ref[pl.ds(start, kv_tile), :]
        vt = vs_ref[pl.ds(start, kv_tile), :]
        s = lax.dot_general(q4, kt, _NT, preferred_element_type=jnp.float32) * scale
        expand = jnp.where(blk_row == blk_col + j * blocks_per_tile, 1.0, 0.0).astype(jnp.bfloat16)
        picked = jnp.dot(sel, expand, preferred_element_type=jnp.float32) > 0.5
        mask = picked & (key_off + start <= pos)
        s = jnp.where(mask[None], s.reshape(HEADS_PER_GROUP, q_blk, kv_tile), NEG).reshape(rows, kv_tile)
        m_prev = m_ref[...]
        m_new = jnp.maximum(m_prev, jnp.max(s, axis=-1, keepdims=True))
        alpha = jnp.exp(m_prev - m_new)
        p = jnp.exp(s - jnp.concatenate([m_new] * reps, axis=1))
        l_ref[...] = alpha * l_ref[...] + jnp.sum(p, axis=-1, keepdims=True)
        acc_ref[...] = alpha * acc_ref[...] + jnp.dot(p.astype(jnp.bfloat16), vt, preferred_element_type=jnp.float32)
        m_ref[...] = m_new
        return carry

    lax.fori_loop(0, n_tiles, body, 0)
    out = acc_ref[...] / jnp.maximum(l_ref[...], 1e-30)
    for j in range(HEADS_PER_GROUP):
        os_ref[:, j * NB_DH:(j + 1) * NB_DH] = out[j * q_blk:(j + 1) * q_blk, :]

    wstart = pl.multiple_of(jnp.maximum(i - WINDOW // q_blk, 0) * q_blk, q_blk)
    kt = kw_ref[pl.ds(wstart, WIN_SPAN), :]
    vt = vw_ref[pl.ds(wstart, WIN_SPAN), :]
    s = lax.dot_general(q4, kt, _NT, preferred_element_type=jnp.float32) * scale
    diff = pos - (wstart + lax.broadcasted_iota(jnp.int32, (1, WIN_SPAN), 1))
    mask = ((diff >= 0) & (diff <= WINDOW))[None]
    s = jnp.where(mask, s.reshape(HEADS_PER_GROUP, q_blk, WIN_SPAN), NEG)
    n_ghost = jnp.maximum(WINDOW - pos, 0).astype(jnp.float32)[None]
    m = jnp.max(s, axis=-1, keepdims=True)
    m = jnp.where(n_ghost > 0, jnp.maximum(m, 0.0), m)
    e = jnp.where(mask, jnp.exp(s - m), 0.0)
    p = e / jnp.maximum(jnp.sum(e, axis=-1, keepdims=True) + n_ghost * jnp.exp(-m), 1e-30)
    ow = jnp.dot(p.reshape(rows, WIN_SPAN).astype(jnp.bfloat16), vt, preferred_element_type=jnp.float32)
    for j in range(HEADS_PER_GROUP):
        ow_ref[:, j * NB_DH:(j + 1) * NB_DH] = ow[j * q_blk:(j + 1) * q_blk, :]


def _slc_win_prompt(qr2d, kpad, vpad, kwin, vwin, sel):
    t = qr2d.shape[0]
    n_slc = t // SLC_LEN
    assert t % SLC_KV_TILE == 0 and SLC_KV_TILE % SLC_LEN == 0 and t >= WIN_SPAN

    def group_major(a):
        return _bf(jnp.transpose(a, (1, 0, 2)))

    gd = HEADS_PER_GROUP * NB_DH
    rows = HEADS_PER_GROUP * Q_BLOCK
    kern = functools.partial(_slc_win_kernel, q_blk=Q_BLOCK, n_slc=n_slc, kv_tile=SLC_KV_TILE)
    kv_spec = pl.BlockSpec((None, t, NB_DH), lambda g, i: (g, 0, 0))
    o_spec = pl.BlockSpec((Q_BLOCK, gd), lambda g, i: (i, g))
    return pl.pallas_call(
        kern,
        grid=(NB_KV, t // Q_BLOCK),
        in_specs=[o_spec, kv_spec, kv_spec, kv_spec, kv_spec, pl.BlockSpec((Q_BLOCK, n_slc), lambda g, i: (i, g))],
        out_specs=[o_spec, o_spec],
        out_shape=[jax.ShapeDtypeStruct((t, NB_HEADS * NB_DH), jnp.float32)] * 2,
        scratch_shapes=[
            pltpu.VMEM((rows, LANES), jnp.float32),
            pltpu.VMEM((rows, LANES), jnp.float32),
            pltpu.VMEM((rows, NB_DH), jnp.float32),
        ],
        compiler_params=pltpu.CompilerParams(
            dimension_semantics=("arbitrary", "arbitrary"), vmem_limit_bytes=VMEM_LIMIT),
        name="slc_win_prompt",
    )(_bf(qr2d), group_major(kpad), group_major(vpad), group_major(kwin), group_major(vwin), sel)


def _split3(x):
    hi = x.astype(jnp.bfloat16)
    r = x - hi.astype(jnp.float32)
    mid = r.astype(jnp.bfloat16)
    lo = (r - mid.astype(jnp.float32)).astype(jnp.bfloat16)
    return hi, mid, lo


def _gla_kernel(xq_ref, xf_ref, xv_ref, xg_ref, lb_ref, gain_ref, o_ref, sfin_ref, st_ref, *, rows, heads, chunk):
    i = pl.program_id(1)

    @pl.when(i == 0)
    def _():
        st_ref[...] = jnp.zeros(st_ref.shape, jnp.float32)

    r_idx = lax.broadcasted_iota(jnp.int32, (rows, rows), 0)
    c_idx = lax.broadcasted_iota(jnp.int32, (rows, rows), 1)
    same = lax.shift_right_logical(r_idx, 5) == lax.shift_right_logical(c_idx, 5)
    causal = same & (c_idx <= r_idx)
    tri = jnp.where(causal, 1.0, 0.0).astype(jnp.bfloat16)
    blk = jnp.where(same, 1.0, 0.0).astype(jnp.bfloat16)
    f32 = jnp.float32
    for hh in range(heads):
        sl = slice(hh * HA_DK, (hh + 1) * HA_DK)
        lb = lb_ref[:, sl]
        qh = jax.nn.silu(xq_ref[:, sl])
        f = lb + (1.0 - lb) * jax.nn.sigmoid(xf_ref[:, sl])
        kh = 1.0 - f
        gh = jnp.log(f)
        parts = _split3(gh)
        bcum = sum(jnp.dot(tri, p, preferred_element_type=f32) for p in parts)
        tot = sum(jnp.dot(blk, p, preferred_element_type=f32) for p in parts)
        q_in = (qh * jnp.exp(bcum)).astype(jnp.bfloat16)
        k_in = (kh * jnp.exp(-bcum)).astype(jnp.bfloat16)
        k_dec = (kh * jnp.exp(tot - bcum)).astype(jnp.bfloat16)
        vb = _bf(xv_ref[:, sl])
        att = lax.dot_general(q_in, k_in, _NT, preferred_element_type=f32)
        att = jnp.where(causal, att, 0.0).astype(jnp.bfloat16)
        o_intra = jnp.dot(att, vb, preferred_element_type=f32)
        st = st_ref[hh]
        o_inter = []
        for c in range(rows // chunk):
            rs = slice(c * chunk, (c + 1) * chunk)
            o_inter.append(lax.dot_general(q_in[rs], st.astype(jnp.bfloat16), _NT, preferred_element_type=f32))
            upd = lax.dot_general(vb[rs], k_dec[rs], _TN, preferred_element_type=f32)
            st = jnp.exp(tot[c * chunk:c * chunk + 1, :]) * st + upd
        st_ref[hh] = st
        o = o_intra + jnp.concatenate(o_inter, axis=0)
        o = o * lax.rsqrt(jnp.mean(o * o, axis=-1, keepdims=True) + EPS) * gain_ref[:, sl]
        o_ref[:, sl] = (o * jax.nn.silu(xg_ref[:, sl])).astype(o_ref.dtype)

    @pl.when(i == pl.num_programs(1) - 1)
    def _():
        for hh in range(heads):
            sfin_ref[hh] = st_ref[hh].T


def _gla_prompt(proj, lb, out_gain):
    t = proj.shape[0]
    hk = HA_HEADS * HA_DK
    assert proj.shape[1] == 4 * hk and HA_DK == HA_DV
    assert t % GLA_ROWS == 0 and GLA_ROWS % HA_CHUNK == 0 and HA_CHUNK == 32 and HA_HEADS % GLA_HEADS == 0
    w = GLA_HEADS * HA_DK
    n_hb = HA_HEADS // GLA_HEADS

    def section(c):
        return pl.BlockSpec((GLA_ROWS, w), functools.partial(lambda h, i, c: (i, c * n_hb + h), c=c))

    vec_spec = pl.BlockSpec((1, w), lambda h, i: (0, h))
    kern = functools.partial(_gla_kernel, rows=GLA_ROWS, heads=GLA_HEADS, chunk=HA_CHUNK)
    return pl.pallas_call(
        kern,
        grid=(n_hb, t // GLA_ROWS),
        in_specs=[section(0), section(1), section(2), section(3), vec_spec, vec_spec],
        out_specs=[pl.BlockSpec((GLA_ROWS, w), lambda h, i: (i, h)),
                   pl.BlockSpec((GLA_HEADS, HA_DK, HA_DV), lambda h, i: (h, 0, 0))],
        out_shape=[jax.ShapeDtypeStruct((t, hk), jnp.bfloat16),
                   jax.ShapeDtypeStruct((HA_HEADS, HA_DK, HA_DV), jnp.float32)],
        scratch_shapes=[pltpu.VMEM((GLA_HEADS, HA_DV, HA_DK), jnp.float32)],
        compiler_params=pltpu.CompilerParams(
            dimension_semantics=("arbitrary", "arbitrary"), vmem_limit_bytes=VMEM_LIMIT),
        name="gla_prompt",
    )(proj, proj, proj, proj, lb.reshape(1, hk), out_gain.reshape(1, hk))


def _page_specs(n_pages):
    return [pl.BlockSpec((PAGE * NB_KV, NB_DH), functools.partial(lambda b, pt, p: (pt[b, p], 0), p=p))
            for p in range(n_pages)]


def _cmp_pages_kernel(pt_ref, *refs, n_pages):
    kp = refs[:n_pages]
    vp = refs[n_pages:2 * n_pages]
    (w1k_ref, w1v_ref, pek_ref, pev_ref, w2k_ref, w2v_ref, gain_ref, ko_ref, vo_ref, x_ref, st_ref) = refs[2 * n_pages:]
    n_sub = n_pages * SUBS_PER_PAGE
    f32 = jnp.float32
    row = lax.broadcasted_iota(jnp.int32, (NB_KV * n_sub, 1), 0)
    last = lax.rem(row, n_sub) == n_sub - 1
    for pages, w1_ref, pe_ref, w2_ref, o_ref, norm in ((kp, w1k_ref, pek_ref, w2k_ref, ko_ref, True),
                                                       (vp, w1v_ref, pev_ref, w2v_ref, vo_ref, False)):
        for p in range(n_pages):
            for n in range(SUBS_PER_PAGE):
                st_ref[pl.ds((p * SUBS_PER_PAGE + n) * SUB_PITCH, SUB_ROWS), :] = pages[p][pl.ds(n * SUB_ROWS, SUB_ROWS), :]
        for g in range(NB_KV):
            for s in range(CMP_STRIDE):
                x_ref[g * n_sub:(g + 1) * n_sub, s * NB_DH:(s + 1) * NB_DH] = _bf(
                    st_ref[pl.ds(s * NB_KV + g, n_sub, stride=SUB_PITCH), :])
        w1 = w1_ref[...]
        a01 = jnp.dot(x_ref[...], w1, preferred_element_type=f32)
        c = jnp.dot(_bf(pe_ref[...]), w1, preferred_element_type=f32)
        a0 = a01[:, :NB_DH] + c[0:1, :NB_DH]
        a1 = a01[:, NB_DH:] + c[1:2, NB_DH:]
        pre = a0 + pltpu.roll(a1, NB_KV * n_sub - 1, 0)
        y = jnp.dot(_bf(jax.nn.gelu(pre)), w2_ref[...], preferred_element_type=f32)
        if norm:
            y = y * lax.rsqrt(jnp.mean(y * y, axis=-1, keepdims=True) + EPS) * gain_ref[...]
        o_ref[...] = jnp.where(last, 0.0, y).astype(o_ref.dtype)


def _cmp_pages(page_table, cache_k, cache_v, pe_k, pe_v, w1_k, w1_v, w2_k, w2_v, gain):
    nb, n_pages = page_table.shape
    n_phys = cache_k.shape[0]
    assert cache_k.shape[1:] == (PAGE, NB_KV, NB_DH) and CMP_LEN == 2 * CMP_STRIDE
    n_sub = n_pages * SUBS_PER_PAGE
    kd = CMP_STRIDE * NB_DH

    def w1cat(w1):
        return _bf(jnp.concatenate([w1[:CMP_STRIDE].reshape(kd, NB_DH), w1[CMP_STRIDE:].reshape(kd, NB_DH)], axis=1))

    def pecat(pe):
        z = jnp.zeros((8, kd), jnp.float32)
        return z.at[0].set(pe[:CMP_STRIDE].reshape(kd)).at[1].set(pe[CMP_STRIDE:].reshape(kd))

    def full(*shape):
        return pl.BlockSpec(shape, lambda b, pt: (0,) * len(shape))

    out_spec = pl.BlockSpec((None, NB_KV * n_sub, NB_DH), lambda b, pt: (b, 0, 0))
    ck = cache_k.reshape(n_phys * PAGE * NB_KV, NB_DH)
    cv = cache_v.reshape(n_phys * PAGE * NB_KV, NB_DH)
    grid_spec = pltpu.PrefetchScalarGridSpec(
        num_scalar_prefetch=1,
        grid=(nb,),
        in_specs=_page_specs(n_pages) * 2 + [full(kd, 2 * NB_DH), full(kd, 2 * NB_DH), full(8, kd), full(8, kd),
                                            full(NB_DH, NB_DH), full(NB_DH, NB_DH), full(1, NB_DH)],
        out_specs=[out_spec, out_spec],
        scratch_shapes=[pltpu.VMEM((NB_KV * n_sub, kd), jnp.bfloat16),
                        pltpu.VMEM((n_sub * SUB_PITCH, NB_DH), jnp.float32)],
    )
    return pl.pallas_call(
        functools.partial(_cmp_pages_kernel, n_pages=n_pages),
        grid_spec=grid_spec,
        out_shape=[jax.ShapeDtypeStruct((nb, NB_KV * n_sub, NB_DH), jnp.bfloat16)] * 2,
        compiler_params=pltpu.CompilerParams(dimension_semantics=("arbitrary",), vmem_limit_bytes=VMEM_LIMIT),
        name="cmp_pages",
    )(page_table, *([ck] * n_pages), *([cv] * n_pages), w1cat(w1_k), w1cat(w1_v), pecat(pe_k), pecat(pe_v),
      _bf(w2_k), _bf(w2_v), gain.reshape(1, NB_DH))


def _softmax_with_tail(s, mask, s_tail, ok_tail):
    m = jnp.max(s, axis=-1, keepdims=True)
    for sc in s_tail:
        m = jnp.maximum(m, sc)
    e = jnp.where(mask, jnp.exp(s - m), 0.0)
    e_tail = [jnp.where(ok, jnp.exp(sc - m), 0.0) for ok, sc in zip(ok_tail, s_tail)]
    den = jnp.sum(e, axis=-1, keepdims=True)
    for et in e_tail:
        den = den + et
    inv = 1.0 / jnp.maximum(den, 1e-30)
    return e * inv, [et * inv for et in e_tail]


def _nsa_sample_kernel(pt_ref, *refs, n_pages, n_new, win_keep):
    ksp = refs[:n_pages]
    vsp = refs[n_pages:2 * n_pages]
    (q_ref, qr_ref, kcb_ref, vcb_ref, ksn_ref, vsn_ref, kwc_ref, vwc_ref, kwn_ref, vwn_ref, ov_ref,
     oc_ref, os_ref, ow_ref, kwo_ref, vwo_ref) = refs[2 * n_pages:]
    f32 = jnp.float32
    past = n_pages * PAGE
    n_sub = past // CMP_STRIDE
    n_cmp = n_sub - CMP_LEN // CMP_STRIDE + 1
    n_slc = -(-(past + n_new) // SLC_LEN)
    new_blk = past // SLC_LEN
    assert (past % SLC_LEN) + n_new <= SLC_LEN and n_sub == LANES and n_slc <= LANES
    rows = HEADS_PER_GROUP * n_new
    all_rows = NB_KV * rows
    sel_rows = NB_KV * n_new
    scale = NB_DH ** -0.5
    groups = range(NB_KV)

    def grp_rows(g, n_tok):
        return pl.ds(g, n_tok, stride=NB_KV)

    def stack(parts):
        return jnp.concatenate(parts, axis=0)

    t_row = lax.rem(lax.broadcasted_iota(jnp.int32, (all_rows, 1), 0), n_new)
    pos = past + t_row
    pos_t = past + lax.rem(lax.broadcasted_iota(jnp.int32, (sel_rows, 1), 0), n_new)
    lane = lax.broadcasted_iota(jnp.int32, (1, LANES), 1)
    q = [q_ref[g] for g in groups]
    qr = [qr_ref[g] for g in groups]
    qrf = stack(qr).astype(f32)

    def tail_scores(new_ref):
        kb = _bf(new_ref[...]).astype(f32)
        out = []
        for i in range(n_new):
            ki = stack([jnp.broadcast_to(kb[i * NB_KV + g:i * NB_KV + g + 1, :], (rows, NB_DH)) for g in groups])
            out.append(jnp.sum(qrf * ki, axis=-1, keepdims=True) * scale)
        return out

    def tail_values(p_tail, new_ref, g):
        vb = _bf(new_ref[...]).astype(f32)
        acc = _bf(p_tail[0][g * rows:(g + 1) * rows]).astype(f32) * vb[g:g + 1, :]
        for i in range(1, n_new):
            acc = acc + _bf(p_tail[i][g * rows:(g + 1) * rows]).astype(f32) * vb[i * NB_KV + g:i * NB_KV + g + 1, :]
        return acc

    mask_c = (lane * CMP_STRIDE + (CMP_LEN - 1) <= pos) & (lane < n_cmp)
    s = stack([lax.dot_general(q[g], kcb_ref[g * n_sub:(g + 1) * n_sub, :], _NT, preferred_element_type=f32)
               for g in groups]) * scale
    s = jnp.where(mask_c, s, NEG)
    m = jnp.max(s, axis=-1, keepdims=True)
    e = jnp.where(mask_c, jnp.exp(s - m), 0.0)
    p = e / jnp.maximum(jnp.sum(e, axis=-1, keepdims=True), 1e-30)
    pb = _bf(p)
    for g in groups:
        oc_ref[g] = jnp.dot(pb[g * rows:(g + 1) * rows], vcb_ref[g * n_sub:(g + 1) * n_sub, :],
                            preferred_element_type=f32)
    psum = []
    for g in groups:
        acc = p[g * rows:g * rows + n_new]
        for j in range(1, HEADS_PER_GROUP):
            acc = acc + p[g * rows + j * n_new:g * rows + (j + 1) * n_new]
        psum.append(acc)
    psum = stack(psum)
    hi = _bf(psum)
    lo = _bf(psum - hi.astype(f32))
    ov = ov_ref[...]
    imp = jnp.dot(hi, ov, preferred_element_type=f32) + jnp.dot(lo, ov, preferred_element_type=f32)
    cur = lax.shift_right_logical(pos_t, 6)
    forced = (lane == 0) | (lane == cur) | (lane == cur - 1)
    future = lane * SLC_LEN > pos_t
    score = jnp.where(future, NEG, jnp.where(forced, BIG, imp))
    score = jnp.where(lane >= n_slc, REMOVED, score)
    lane_f = lax.broadcasted_iota(jnp.int32, (sel_rows, LANES), 1).astype(f32)
    sel = jnp.zeros((sel_rows, LANES), f32)
    for _ in range(min(N_SELECT, n_slc)):
        mx = jnp.max(score, axis=-1, keepdims=True)
        idx = jnp.min(jnp.where(score == mx, lane_f, float(LANES)), axis=-1, keepdims=True)
        hit = lane_f == idx
        sel = jnp.where(hit, 1.0, sel)
        score = jnp.where(hit, REMOVED, score)
    sel_r = stack([sel[g * n_new:(g + 1) * n_new] for g in groups for _ in range(HEADS_PER_GROUP)])
    expand = jnp.where(lax.broadcasted_iota(jnp.int32, (LANES, past), 0)
                       == lax.shift_right_logical(lax.broadcasted_iota(jnp.int32, (LANES, past), 1), 6),
                       1.0, 0.0).astype(jnp.bfloat16)
    picked = jnp.dot(_bf(sel_r), expand, preferred_element_type=f32) > 0.5
    s_all = stack([jnp.concatenate([lax.dot_general(qr[g], _bf(ksp[pg][grp_rows(g, PAGE), :]), _NT,
                                                    preferred_element_type=f32)
                                    for pg in range(n_pages)], axis=1) for g in groups]) * scale
    s_all = jnp.where(picked, s_all, NEG)
    new_sel = sel_r[:, new_blk:new_blk + 1] > 0.5
    ok_new = [new_sel & (t_row >= i) for i in range(n_new)]
    s_new = [jnp.where(ok, sc, NEG) for ok, sc in zip(ok_new, tail_scores(ksn_ref))]
    p_all, p_new = _softmax_with_tail(s_all, picked, s_new, ok_new)
    p_all = _bf(p_all)
    for g in groups:
        acc = tail_values(p_new, vsn_ref, g)
        for pg in range(n_pages):
            acc = acc + jnp.dot(p_all[g * rows:(g + 1) * rows, pg * PAGE:(pg + 1) * PAGE],
                                _bf(vsp[pg][grp_rows(g, PAGE), :]), preferred_element_type=f32)
        os_ref[g] = acc
    mask_w = lax.broadcasted_iota(jnp.int32, (1, win_keep), 1) >= t_row
    s_w = stack([lax.dot_general(qr[g], _bf(kwc_ref[grp_rows(g, win_keep), :]), _NT, preferred_element_type=f32)
                 for g in groups]) * scale
    s_w = jnp.where(mask_w, s_w, NEG)
    ok_wn = [t_row >= i for i in range(n_new)]
    s_wn = [jnp.where(ok, sc, NEG) for ok, sc in zip(ok_wn, tail_scores(kwn_ref))]
    p_w, p_wn = _softmax_with_tail(s_w, mask_w, s_wn, ok_wn)
    p_w = _bf(p_w)
    for g in groups:
        ow_ref[g] = (tail_values(p_wn, vwn_ref, g)
                     + jnp.dot(p_w[g * rows:(g + 1) * rows], _bf(vwc_ref[grp_rows(g, win_keep), :]),
                               preferred_element_type=f32))
    new_rows = n_new * NB_KV
    kept_rows = win_keep * NB_KV - new_rows
    for src_ref, add_ref, dst_ref in ((kwc_ref, kwn_ref, kwo_ref), (vwc_ref, vwn_ref, vwo_ref)):
        dst_ref[0:kept_rows, :] = src_ref[new_rows:new_rows + kept_rows, :]
        dst_ref[kept_rows:kept_rows + new_rows, :] = add_ref[...]


def _nsa_sample(page_table, q, qr, kcb, vcb, cache_sk, cache_sv, ks_new, vs_new, cache_wk, cache_wv, kw_new, vw_new):
    nb, n_pages = page_table.shape
    t = q.shape[1]
    n_phys = cache_sk.shape[0]
    keep = cache_wk.shape[1]
    assert keep == WINDOW and (t * NB_KV) % 8 == 0 and t < keep
    rows = HEADS_PER_GROUP * t
    n_sub = n_pages * SUBS_PER_PAGE

    def q_rows(a):
        a = a.reshape(nb, t, NB_KV, HEADS_PER_GROUP, NB_DH)
        return _bf(jnp.transpose(a, (0, 2, 3, 1, 4)).reshape(nb, NB_KV, rows, NB_DH))

    def q_unrows(a):
        a = a.reshape(nb, NB_KV, HEADS_PER_GROUP, t, NB_DH)
        return jnp.transpose(a, (0, 3, 1, 2, 4)).reshape(nb, t, NB_HEADS, NB_DH)

    def per_b(*shape):
        return pl.BlockSpec((None,) + shape, lambda b, pt: (b,) + (0,) * len(shape))

    def rows_view(a):
        return a.reshape(a.shape[0], a.shape[1] * NB_KV, NB_DH)

    n_cmp = n_sub - CMP_LEN // CMP_STRIDE + 1
    q_spec = per_b(NB_KV, rows, NB_DH)
    win_spec = per_b(keep * NB_KV, NB_DH)
    new_spec = per_b(t * NB_KV, NB_DH)
    cmp_spec = per_b(NB_KV * n_sub, NB_DH)
    in_specs = (_page_specs(n_pages) * 2
                + [q_spec, q_spec, cmp_spec, cmp_spec, new_spec, new_spec, win_spec, win_spec, new_spec, new_spec,
                   pl.BlockSpec((LANES, LANES), lambda b, pt: (0, 0))])
    grid_spec = pltpu.PrefetchScalarGridSpec(
        num_scalar_prefetch=1, grid=(nb,), in_specs=in_specs, out_specs=[q_spec] * 3 + [win_spec] * 2)
    sk = cache_sk.reshape(n_phys * PAGE * NB_KV, NB_DH)
    sv = cache_sv.reshape(n_phys * PAGE * NB_KV, NB_DH)
    o_c, o_s, o_w, wk_next, wv_next = pl.pallas_call(
        functools.partial(_nsa_sample_kernel, n_pages=n_pages, n_new=t, win_keep=keep),
        grid_spec=grid_spec,
        out_shape=([jax.ShapeDtypeStruct((nb, NB_KV, rows, NB_DH), jnp.float32)] * 3
                   + [jax.ShapeDtypeStruct((nb, keep * NB_KV, NB_DH), jnp.float32)] * 2),
        compiler_params=pltpu.CompilerParams(dimension_semantics=("arbitrary",), vmem_limit_bytes=VMEM_LIMIT),
        name="nsa_sample",
    )(page_table, *([sk] * n_pages), *([sv] * n_pages), q_rows(q), q_rows(qr), kcb, vcb,
      rows_view(ks_new), rows_view(vs_new), rows_view(cache_wk), rows_view(cache_wv), rows_view(kw_new),
      rows_view(vw_new), _overlap_matrix(LANES, n_cmp, LANES))
    return (q_unrows(o_c), q_unrows(o_s), q_unrows(o_w),
            wk_next.reshape(cache_wk.shape), wv_next.reshape(cache_wv.shape))


def _mm_kernel(x_ref, w_ref, o_ref, wb_ref):
    @pl.when(pl.program_id(1) == 0)
    def _():
        wb_ref[...] = w_ref[...].astype(jnp.bfloat16)

    o_ref[...] = jnp.dot(x_ref[...], wb_ref[...], preferred_element_type=jnp.float32)


def _mm_tiles(m, k, n):
    if k <= 2048:
        tm = 1024
        tn = 1024 if (n % 1024 == 0 or n >= 4096) else 512
    else:
        tm, tn = 512, 512
    return min(tm, m), min(tn, n)


def _mm(x, w):
    lead = x.shape[:-1]
    k = x.shape[-1]
    n = w.shape[1]
    x2 = _bf(x.reshape(-1, k))
    m = x2.shape[0]
    tm, tn = _mm_tiles(m, k, n)
    assert m % tm == 0
    out = pl.pallas_call(
        _mm_kernel,
        grid=(pl.cdiv(n, tn), m // tm),
        in_specs=[pl.BlockSpec((tm, k), lambda j, i: (i, 0)),
                  pl.BlockSpec((k, tn), lambda j, i: (0, j))],
        out_specs=pl.BlockSpec((tm, tn), lambda j, i: (i, j)),
        out_shape=jax.ShapeDtypeStruct((m, n), jnp.float32),
        scratch_shapes=[pltpu.VMEM((k, tn), jnp.bfloat16)],
        compiler_params=pltpu.CompilerParams(
            dimension_semantics=("arbitrary", "arbitrary"), vmem_limit_bytes=VMEM_LIMIT),
        name="mm",
    )(x2, w)
    return out.reshape(*lead, n)


def _rmsnorm(x, g):
    xf = x.astype(jnp.float32)
    y = xf * lax.rsqrt(jnp.mean(xf * xf, axis=-1, keepdims=True) + EPS)
    return (y * g.astype(jnp.float32)).astype(x.dtype)


def _rope(x, pos):
    half = x.shape[-1] // 2
    inv = ROPE_THETA ** (-jnp.arange(half, dtype=jnp.float32) / half)
    ang = pos.astype(jnp.float32)[:, None] * inv[None, :]
    cos, sin = jnp.cos(ang)[:, None, :], jnp.sin(ang)[:, None, :]
    xf = x.astype(jnp.float32)
    x1, x2 = xf[..., :half], xf[..., half:]
    return jnp.concatenate([x1 * cos - x2 * sin, x2 * cos + x1 * sin], axis=-1).astype(x.dtype)


def _gla_chunked(q, k, v, logf, s0):
    b, t, h, dk = q.shape
    dv = v.shape[-1]
    c = math.gcd(t, HA_CHUNK)
    n = t // c

    def to_chunks(a):
        return jnp.transpose(a.reshape(b, n, c, h, a.shape[-1]), (1, 0, 3, 2, 4))

    qc, kc, vc, gc = to_chunks(q), to_chunks(k), to_chunks(v), to_chunks(logf)
    bcum = jnp.cumsum(gc, axis=3)
    blast = bcum[:, :, :, -1, :]
    q_in = qc * jnp.exp(bcum)
    k_in = kc * jnp.exp(-bcum)
    k_dec = kc * jnp.exp(blast[:, :, :, None, :] - bcum)
    causal = jnp.tril(jnp.ones((c, c), dtype=bool))
    att = jnp.where(causal, jnp.einsum('nbhik,nbhjk->nbhij', q_in, k_in), 0.0)
    o_intra = jnp.einsum('nbhij,nbhjv->nbhiv', att, vc)

    def step(s, inp):
        qi, kd, vi, dl = inp
        o_inter = jnp.einsum('bhik,bhkv->bhiv', qi, s)
        s = jnp.exp(dl)[..., None] * s + jnp.einsum('bhik,bhiv->bhkv', kd, vi)
        return s, o_inter

    s_final, o_inter = lax.scan(step, s0.astype(jnp.float32), (q_in, k_dec, vc, blast))
    o = o_intra + o_inter
    return jnp.transpose(o, (1, 0, 3, 2, 4)).reshape(b, t, h, dv), s_final


def _hgrn2_layer(xn, s0, lb, w_in, out_gain, w_out):
    b, t, _ = xn.shape
    hk = HA_HEADS * HA_DK
    hv = HA_HEADS * HA_DV
    proj = _mm(xn, w_in)
    if s0 is None:
        assert b == 1
        o, s_new = _gla_prompt(proj[0], lb, out_gain)
        return _mm(o[None], w_out), s_new[None]
    q = jax.nn.silu(proj[..., :hk])
    f = lb + (1.0 - lb) * jax.nn.sigmoid(proj[..., hk:2 * hk])
    i_in = proj[..., 2 * hk:2 * hk + hv]
    gate = proj[..., 2 * hk + hv:]
    shp_k = (b, t, HA_HEADS, HA_DK)
    shp_v = (b, t, HA_HEADS, HA_DV)
    o, s_new = _gla_chunked(q.reshape(shp_k), (1.0 - f).reshape(shp_k), i_in.reshape(shp_v),
                            jnp.log(f).reshape(shp_k), s0)
    o = _rmsnorm(o, out_gain.reshape(HA_HEADS, HA_DV)) * jax.nn.silu(gate).reshape(shp_v)
    return _mm(o.reshape(b, t, hv), w_out), s_new


def _conv_ffn(xn, prev, w_up, conv_w, conv_b, w_down):
    t = xn.shape[1]
    up = _mm(xn, w_up)
    u, gt = up[..., :D_FF], up[..., D_FF:]
    ext = jnp.concatenate([prev.astype(gt.dtype), gt], axis=1)
    conv = conv_b
    for j in range(CONV_W):
        conv = conv + conv_w[j] * ext[:, j:j + t]
    return _mm(jax.nn.silu(conv) * u, w_down), ext[:, t:]


def _ple(h, p, gain, w_p, w_g):
    gate = jax.nn.sigmoid(_mm(_rmsnorm(h, gain), w_g))
    return h + _mm(p, w_p) * gate


def _compress(rows, pe, w1, w2):
    b, l, g, d = rows.shape
    n_sub = l // CMP_STRIDE
    r = CMP_LEN // CMP_STRIDE
    n_cmp = n_sub - r + 1
    sub = rows[:, :n_sub * CMP_STRIDE].reshape(b, n_sub, CMP_STRIDE, g, d)
    pre = None
    for j in range(r):
        sl = slice(j * CMP_STRIDE, (j + 1) * CMP_STRIDE)
        part = jnp.einsum('bnsgd,sde->bnge', sub + pe[sl][None, None, :, None, :], w1[sl])[:, j:j + n_cmp]
        pre = part if pre is None else pre + part
    return jnp.einsum('bnge,ef->bngf', jax.nn.gelu(pre), w2)


def _nsa_layer(xn, pos, shared, w_in, q_gain, w_out, banded):
    b, t, _ = xn.shape
    hd = NB_HEADS * NB_DH
    proj = _mm(xn, w_in)
    q = _rmsnorm(proj[..., :hd].reshape(b, t, NB_HEADS, NB_DH), q_gain)
    gates = jax.nn.sigmoid(proj[..., hd:].astype(jnp.float32)).reshape(b, t, 3, NB_HEADS)
    qr = _rope(q, pos)
    win_next = None
    if banded:
        assert b == 1
        kcb, vcb, kpad, vpad, win_k, win_v = shared
        o_c2, sel = _cmp_topk_prompt(q.reshape(t, hd), kcb[0], vcb[0])
        o_s2, o_w2 = _slc_win_prompt(qr.reshape(t, hd), kpad[0], vpad[0], win_k[0], win_v[0], sel)
        o_c = o_c2.reshape(b, t, NB_HEADS, NB_DH)
        o_s = o_s2.reshape(b, t, NB_HEADS, NB_DH)
        o_w = o_w2.reshape(b, t, NB_HEADS, NB_DH)
    else:
        o_c, o_s, o_w, wk_next, wv_next = _nsa_sample(shared[0], q, qr, *shared[1:])
        win_next = (wk_next, wv_next)
    o = (gates[:, :, 0, :, None] * o_c + gates[:, :, 1, :, None] * o_s + gates[:, :, 2, :, None] * o_w)
    return _mm(o.reshape(b, t, hd), w_out), win_next


def _shared_kv(h, pos, past, wt):
    b, t, _ = h.shape
    kv = _mm(_rmsnorm(h, wt['kv_norm']), wt['kv_w']).reshape(b, t, 6, NB_KV, NB_DH)
    k_cmp, v_cmp = kv[:, :, 0], kv[:, :, 1]
    k_slc = _rope(_rmsnorm(kv[:, :, 2], wt['k_norm_slc']), pos)
    v_slc = kv[:, :, 3]
    k_win = _rope(_rmsnorm(kv[:, :, 4], wt['k_norm_win']), pos)
    v_win = kv[:, :, 5]
    if past is None:
        keep = min(WINDOW, t)
        kcb = _rmsnorm(_compress(k_cmp, wt['cmp_pe_k'], wt['cmp_w1_k'], wt['cmp_w2_k']), wt['k_norm_cmp'])
        vcb = _compress(v_cmp, wt['cmp_pe_v'], wt['cmp_w1_v'], wt['cmp_w2_v'])
        assert t % SLC_LEN == 0
        shared = (kcb, vcb, k_slc, v_slc, k_win, v_win)
        rows = (k_cmp, v_cmp, k_slc, v_slc, k_win[:, -keep:], v_win[:, -keep:])
    else:
        page_table, ckc, cvc, cks, cvs, cwk, cwv = past
        assert t < CMP_STRIDE and (page_table.shape[1] * PAGE) % CMP_STRIDE == 0
        kcb, vcb = _cmp_pages(page_table, ckc, cvc, wt['cmp_pe_k'], wt['cmp_pe_v'], wt['cmp_w1_k'], wt['cmp_w1_v'],
                              wt['cmp_w2_k'], wt['cmp_w2_v'], wt['k_norm_cmp'])
        shared = (page_table, kcb, vcb, cks, cvs, k_slc, v_slc, cwk, cwv, k_win, v_win)
        rows = (k_cmp, v_cmp, k_slc, v_slc, None, None)
    return shared, rows


def _trunk(x, p, start, s_hgrn, s_conv, past, wt):
    b, t, _ = x.shape
    pos = start + jnp.arange(t)
    banded = past is None
    lb = jnp.cumsum(jax.nn.softmax(wt['a_lb_logits'].astype(jnp.float32), axis=0), axis=0)
    h = x
    hgrn_out, conv_out = [], []
    shared, rows = None, None
    for i in range(DEPTH):
        hn = _rmsnorm(h, wt['norm_mix'][i])
        if i < N_A:
            y, s = _hgrn2_layer(hn, None if s_hgrn is None else s_hgrn[i], lb[i], wt['a_w_in'][i],
                                wt['a_out_norm'][i], wt['a_w_out'][i])
            hgrn_out.append(s)
        else:
            j = i - N_A
            y, win_next = _nsa_layer(hn, pos, shared, wt['b_w_in'][j], wt['b_q_norm'][j], wt['b_w_out'][j], banded)
            if win_next is not None:
                assert N_B == 1
                rows = rows[:4] + win_next
        h = h + y
        f, cs = _conv_ffn(_rmsnorm(h, wt['norm_ffn'][i]), s_conv[i], wt['ffn_w_up'][i], wt['ffn_conv_w'][i],
                          wt['ffn_conv_b'][i], wt['ffn_w_down'][i])
        conv_out.append(cs)
        h = h + f
        h = _ple(h, p[i], wt['norm_ple'][i], wt['ple_w_p'][i], wt['ple_w_g'][i])
        if i == N_A - 1:
            shared, rows = _shared_kv(h, pos, past, wt)
    return h, jnp.stack(hgrn_out), jnp.stack(conv_out), rows


def kernel(x_prompt, x_sample, state_hgrn, state_conv, cache_cmp_k, cache_cmp_v, cache_slc_k, cache_slc_v,
           cache_win_k, cache_win_v, page_table, p_prompt, p_sample, norm_mix, norm_ffn, norm_ple,
           a_w_in, a_lb_logits, a_out_norm, a_w_out, kv_norm, kv_w, k_norm_cmp, k_norm_slc, k_norm_win,
           cmp_pe_k, cmp_pe_v, cmp_w1_k, cmp_w2_k, cmp_w1_v, cmp_w2_v, b_w_in, b_q_norm, b_w_out,
           ffn_w_up, ffn_conv_w, ffn_conv_b, ffn_w_down, ple_w_p, ple_w_g):
    wt = {
        'norm_mix': norm_mix, 'norm_ffn': norm_ffn, 'norm_ple': norm_ple,
        'a_w_in': a_w_in, 'a_lb_logits': a_lb_logits, 'a_out_norm': a_out_norm, 'a_w_out': a_w_out,
        'kv_norm': kv_norm, 'kv_w': kv_w, 'k_norm_cmp': k_norm_cmp, 'k_norm_slc': k_norm_slc,
        'k_norm_win': k_norm_win, 'cmp_pe_k': cmp_pe_k, 'cmp_pe_v': cmp_pe_v, 'cmp_w1_k': cmp_w1_k,
        'cmp_w2_k': cmp_w2_k, 'cmp_w1_v': cmp_w1_v, 'cmp_w2_v': cmp_w2_v,
        'b_w_in': b_w_in, 'b_q_norm': b_q_norm, 'b_w_out': b_w_out,
        'ffn_w_up': ffn_w_up, 'ffn_conv_w': ffn_conv_w, 'ffn_conv_b': ffn_conv_b, 'ffn_w_down': ffn_w_down,
        'ple_w_p': ple_w_p, 'ple_w_g': ple_w_g,
    }
    bp, past_len = x_prompt.shape[0], page_table.shape[1] * cache_cmp_k.shape[1]
    s_conv0 = jnp.zeros((DEPTH, bp, CONV_W - 1, D_FF), x_prompt.dtype)
    y_prompt, hgrn_p, conv_p, rows_p = _trunk(x_prompt, p_prompt, 0, None, s_conv0, None, wt)
    past = (page_table, cache_cmp_k, cache_cmp_v, cache_slc_k, cache_slc_v, cache_win_k, cache_win_v)
    y_sample, hgrn_s, conv_s, rows_s = _trunk(x_sample, p_sample, past_len, state_hgrn, state_conv, past, wt)
    kc_p, vc_p, ks_p, vs_p, wk_p, wv_p = rows_p
    kc_s, vc_s, ks_s, vs_s, wk_s, wv_s = rows_s
    return (y_prompt, y_sample, hgrn_p, hgrn_s, conv_p, conv_s, kc_p, vc_p, ks_p, vs_p,
            kc_s, vc_s, ks_s, vs_s, wk_p, wv_p, wk_s, wv_s)
```

```python
import functools
import math

import jax
import jax.numpy as jnp
from jax import lax
from jax.experimental import pallas as pl
from jax.experimental.pallas import tpu as pltpu

D_MODEL = 2048
DEPTH = 2
N_A = DEPTH // 2
N_B = DEPTH - N_A
EPS = 1e-6
HA_DK = 128
HA_HEADS = D_MODEL // HA_DK
HA_DV = D_MODEL // HA_HEADS
HA_CHUNK = 32
NB_DH = 128
NB_HEADS = D_MODEL // NB_DH
NB_KV = 4
HEADS_PER_GROUP = NB_HEADS // NB_KV
CMP_LEN = 32
CMP_STRIDE = 16
SLC_LEN = 64
N_SELECT = 16
WINDOW = 512
Q_BLOCK = 128
ROPE_THETA = 10000.0
D_FF = ((8 * D_MODEL // 3 + 127) // 128) * 128
CONV_W = 3
NEG = -1e30
BIG = 1e30
LOG2_E = math.log2(math.e)
REMOVED = -3.0e38
LANES = 128
SLC_KV_TILE = 512
WIN_SPAN = WINDOW + Q_BLOCK
GLA_ROWS = 256
GLA_HEADS = 4
PAGE = 128
SUBS_PER_PAGE = PAGE // CMP_STRIDE
SUB_ROWS = CMP_STRIDE * NB_KV
SUB_PITCH = SUB_ROWS + 8
FFN_TM = 1024
FFN_TN = 512
VMEM_LIMIT = 48 * 1024 * 1024
_NT = (((1,), (1,)), ((), ()))
_TN = (((0,), (0,)), ((), ()))


def _bf(x):
    return x.astype(jnp.bfloat16)


def _cmp_topk_kernel(q_ref, k_ref, v_ref, ov_ref, o_ref, sel_ref, *, q_blk, n_cmp_pad, n_slc, n_sel):
    i = pl.program_id(0)
    f32 = jnp.float32
    pos = i * q_blk + lax.broadcasted_iota(jnp.int32, (q_blk, 1), 0)
    kend = lax.broadcasted_iota(jnp.int32, (1, n_cmp_pad), 1) * CMP_STRIDE + (CMP_LEN - 1)
    mask_c = kend <= pos
    scale = NB_DH ** -0.5
    ov = ov_ref[...]
    imp = []
    for g in range(NB_KV):
        k = k_ref[g]
        v = v_ref[g]
        psum = jnp.zeros((q_blk, n_cmp_pad), f32)
        for j in range(HEADS_PER_GROUP):
            cs = slice((g * HEADS_PER_GROUP + j) * NB_DH, (g * HEADS_PER_GROUP + j + 1) * NB_DH)
            s = lax.dot_general(_bf(q_ref[:, cs]), k, _NT, preferred_element_type=f32) * scale
            s = jnp.where(mask_c, s, NEG)
            m = jnp.max(s, axis=-1, keepdims=True)
            e = jnp.where(mask_c, jnp.exp(s - m), 0.0)
            p = e / jnp.maximum(jnp.sum(e, axis=-1, keepdims=True), 1e-30)
            o_ref[:, cs] = jnp.dot(_bf(p), v, preferred_element_type=f32)
            psum = psum + p
        hi = _bf(psum)
        lo = _bf(psum - hi.astype(f32))
        imp.append(jnp.dot(hi, ov, preferred_element_type=f32) + jnp.dot(lo, ov, preferred_element_type=f32))
    imp = jnp.concatenate(imp, axis=0)
    pos_r = jnp.concatenate([pos] * NB_KV, axis=0)
    sidx = lax.broadcasted_iota(jnp.int32, (1, n_slc), 1)
    cur = lax.shift_right_logical(pos_r, 6)
    forced = (sidx == 0) | (sidx == cur) | (sidx == cur - 1)
    future = sidx * SLC_LEN > pos_r
    score = jnp.where(future, NEG, jnp.where(forced, BIG, imp))
    lane = lax.broadcasted_iota(jnp.int32, (NB_KV * q_blk, n_slc), 1).astype(f32)
    sel = jnp.zeros((NB_KV * q_blk, n_slc), f32)
    for _ in range(n_sel):
        m = jnp.max(score, axis=-1, keepdims=True)
        idx = jnp.min(jnp.where(score == m, lane, float(n_slc)), axis=-1, keepdims=True)
        hit = lane == idx
        sel = jnp.where(hit, 1.0, sel)
        score = jnp.where(hit, REMOVED, score)
    for g in range(NB_KV):
        sel_ref[:, g * n_slc:(g + 1) * n_slc] = sel[g * q_blk:(g + 1) * q_blk].astype(sel_ref.dtype)


def _overlap_matrix(n_cmp_rows, n_cmp, n_slc_cols):
    cstart = jnp.arange(n_cmp_rows) * CMP_STRIDE
    sstart = jnp.arange(n_slc_cols) * SLC_LEN
    return _bf((cstart[:, None] <= sstart[None, :] + (SLC_LEN - 1))
               & (cstart[:, None] + (CMP_LEN - 1) >= sstart[None, :])
               & (jnp.arange(n_cmp_rows)[:, None] < n_cmp))


def _cmp_topk_prompt(q2d, kcb, vcb):
    t = q2d.shape[0]
    n_cmp = kcb.shape[0]
    n_cmp_pad = -(-n_cmp // LANES) * LANES
    n_slc = t // SLC_LEN
    assert SLC_LEN == 64 and n_slc == LANES and t % Q_BLOCK == 0
    n_sel = min(N_SELECT, n_slc)
    padc = ((0, 0), (0, n_cmp_pad - n_cmp), (0, 0))
    k = _bf(jnp.pad(jnp.transpose(kcb, (1, 0, 2)), padc))
    v = _bf(jnp.pad(jnp.transpose(vcb, (1, 0, 2)), padc))
    hd = NB_HEADS * NB_DH
    kern = functools.partial(_cmp_topk_kernel, q_blk=Q_BLOCK, n_cmp_pad=n_cmp_pad, n_slc=n_slc, n_sel=n_sel)
    kv_spec = pl.BlockSpec((NB_KV, n_cmp_pad, NB_DH), lambda i: (0, 0, 0))
    return pl.pallas_call(
        kern,
        grid=(t // Q_BLOCK,),
        in_specs=[pl.BlockSpec((Q_BLOCK, hd), lambda i: (i, 0)), kv_spec, kv_spec,
                  pl.BlockSpec((n_cmp_pad, n_slc), lambda i: (0, 0))],
        out_specs=[pl.BlockSpec((Q_BLOCK, hd), lambda i: (i, 0)),
                   pl.BlockSpec((Q_BLOCK, NB_KV * n_slc), lambda i: (i, 0))],
        out_shape=[jax.ShapeDtypeStruct((t, hd), jnp.float32),
                   jax.ShapeDtypeStruct((t, NB_KV * n_slc), jnp.bfloat16)],
        compiler_params=pltpu.CompilerParams(dimension_semantics=("arbitrary",), vmem_limit_bytes=VMEM_LIMIT),
        name="cmp_topk_prompt",
    )(q2d, k, v, _overlap_matrix(n_cmp_pad, n_cmp, n_slc))


def _slc_win_kernel(q_ref, ks_ref, vs_ref, kw_ref, vw_ref, sel_ref, os_ref, ow_ref, m_ref, l_ref, acc_ref,
                    *, q_blk, n_slc, kv_tile):
    i = pl.program_id(1)
    rows = HEADS_PER_GROUP * q_blk
    blocks_per_tile = kv_tile // SLC_LEN
    reps = kv_tile // LANES
    scale = NB_DH ** -0.5
    q4 = jnp.concatenate([q_ref[:, j * NB_DH:(j + 1) * NB_DH] for j in range(HEADS_PER_GROUP)], axis=0)
    sel = sel_ref[...]
    pos = i * q_blk + lax.broadcasted_iota(jnp.int32, (q_blk, 1), 0)
    m_ref[...] = jnp.full(m_ref.shape, NEG, jnp.float32)
    l_ref[...] = jnp.zeros(l_ref.shape, jnp.float32)
    acc_ref[...] = jnp.zeros(acc_ref.shape, jnp.float32)
    blk_row = lax.broadcasted_iota(jnp.int32, (n_slc, kv_tile), 0)
    blk_col = lax.shift_right_logical(lax.broadcasted_iota(jnp.int32, (n_slc, kv_tile), 1), 6)
    key_off = lax.broadcasted_iota(jnp.int32, (1, kv_tile), 1)
    n_tiles = lax.div(i * q_blk + (q_blk - 1), kv_tile) + 1

    def body(j, carry):
        start = pl.multiple_of(j * kv_tile, kv_tile)
        kt = ks_ref[pl.ds(start, kv_tile), :]
        vt = vs_ref[pl.ds(start, kv_tile), :]
        s = lax.dot_general(q4, kt, _NT, preferred_element_type=jnp.float32) * (scale * LOG2_E)
        expand = jnp.where(blk_row == blk_col + j * blocks_per_tile, 1.0, 0.0).astype(jnp.bfloat16)
        picked = jnp.dot(sel, expand, preferred_element_type=jnp.float32) > 0.5
        mask = picked & (key_off + start <= pos)
        s = jnp.where(mask[None], s.reshape(HEADS_PER_GROUP, q_blk, kv_tile), NEG).reshape(rows, kv_tile)
        m_prev = m_ref[...]
        m_new = jnp.maximum(m_prev, jnp.max(s, axis=-1, keepdims=True))
        alpha = jnp.exp2(m_prev - m_new)
        p = jnp.exp2(s - jnp.concatenate([m_new] * reps, axis=1))
        l_ref[...] = alpha * l_ref[...] + jnp.sum(p, axis=-1, keepdims=True)
        acc_ref[...] = alpha * acc_ref[...] + jnp.dot(p.astype(jnp.bfloat16), vt, preferred_element_type=jnp.float32)
        m_ref[...] = m_new
        return carry

    lax.fori_loop(0, n_tiles, body, 0)
    out = acc_ref[...] / jnp.maximum(l_ref[...], 1e-30)
    for j in range(HEADS_PER_GROUP):
        os_ref[:, j * NB_DH:(j + 1) * NB_DH] = out[j * q_blk:(j + 1) * q_blk, :]

    wstart = pl.multiple_of(jnp.maximum(i - WINDOW // q_blk, 0) * q_blk, q_blk)
    kt = kw_ref[pl.ds(wstart, WIN_SPAN), :]
    vt = vw_ref[pl.ds(wstart, WIN_SPAN), :]
    s = lax.dot_general(q4, kt, _NT, preferred_element_type=jnp.float32) * scale
    diff = pos - (wstart + lax.broadcasted_iota(jnp.int32, (1, WIN_SPAN), 1))
    mask = ((diff >= 0) & (diff <= WINDOW))[None]
    s = jnp.where(mask, s.reshape(HEADS_PER_GROUP, q_blk, WIN_SPAN), NEG)
    n_ghost = jnp.maximum(WINDOW - pos, 0).astype(jnp.float32)[None]
    m = jnp.max(s, axis=-1, keepdims=True)
    m = jnp.where(n_ghost > 0, jnp.maximum(m, 0.0), m)
    e = jnp.where(mask, jnp.exp(s - m), 0.0)
    p = e / jnp.maximum(jnp.sum(e, axis=-1, keepdims=True) + n_ghost * jnp.exp(-m), 1e-30)
    ow = jnp.dot(p.reshape(rows, WIN_SPAN).astype(jnp.bfloat16), vt, preferred_element_type=jnp.float32)
    for j in range(HEADS_PER_GROUP):
        ow_ref[:, j * NB_DH:(j + 1) * NB_DH] = ow[j * q_blk:(j + 1) * q_blk, :]


def _slc_win_prompt(qr2d, kpad, vpad, kwin, vwin, sel):
    t = qr2d.shape[0]
    n_slc = t // SLC_LEN
    assert t % SLC_KV_TILE == 0 and SLC_KV_TILE % SLC_LEN == 0 and t >= WIN_SPAN

    def group_major(a):
        return _bf(jnp.transpose(a, (1, 0, 2)))

    gd = HEADS_PER_GROUP * NB_DH
    rows = HEADS_PER_GROUP * Q_BLOCK
    kern = functools.partial(_slc_win_kernel, q_blk=Q_BLOCK, n_slc=n_slc, kv_tile=SLC_KV_TILE)
    kv_spec = pl.BlockSpec((None, t, NB_DH), lambda g, i: (g, 0, 0))
    o_spec = pl.BlockSpec((Q_BLOCK, gd), lambda g, i: (i, g))
    return pl.pallas_call(
        kern,
        grid=(NB_KV, t // Q_BLOCK),
        in_specs=[o_spec, kv_spec, kv_spec, kv_spec, kv_spec, pl.BlockSpec((Q_BLOCK, n_slc), lambda g, i: (i, g))],
        out_specs=[o_spec, o_spec],
        out_shape=[jax.ShapeDtypeStruct((t, NB_HEADS * NB_DH), jnp.float32)] * 2,
        scratch_shapes=[
            pltpu.VMEM((rows, LANES), jnp.float32),
            pltpu.VMEM((rows, LANES), jnp.float32),
            pltpu.VMEM((rows, NB_DH), jnp.float32),
        ],
        compiler_params=pltpu.CompilerParams(
            dimension_semantics=("arbitrary", "arbitrary"), vmem_limit_bytes=VMEM_LIMIT),
        name="slc_win_prompt",
    )(_bf(qr2d), group_major(kpad), group_major(vpad), group_major(kwin), group_major(vwin), sel)


def _split3(x):
    hi = x.astype(jnp.bfloat16)
    r = x - hi.astype(jnp.float32)
    mid = r.astype(jnp.bfloat16)
    lo = (r - mid.astype(jnp.float32)).astype(jnp.bfloat16)
    return hi, mid, lo


def _gla_kernel(xq_ref, xf_ref, xv_ref, xg_ref, lb_ref, gain_ref, o_ref, sfin_ref, st_ref, *, rows, heads, chunk):
    i = pl.program_id(1)

    @pl.when(i == 0)
    def _():
        st_ref[...] = jnp.zeros(st_ref.shape, jnp.float32)

    r_idx = lax.broadcasted_iota(jnp.int32, (rows, rows), 0)
    c_idx = lax.broadcasted_iota(jnp.int32, (rows, rows), 1)
    same = lax.shift_right_logical(r_idx, 5) == lax.shift_right_logical(c_idx, 5)
    causal = same & (c_idx <= r_idx)
    tri = jnp.where(causal, 1.0, 0.0).astype(jnp.bfloat16)
    blk = jnp.where(same, 1.0, 0.0).astype(jnp.bfloat16)
    f32 = jnp.float32
    for hh in range(heads):
        sl = slice(hh * HA_DK, (hh + 1) * HA_DK)
        lb = lb_ref[:, sl]
        qh = jax.nn.silu(xq_ref[:, sl])
        f = lb + (1.0 - lb) * jax.nn.sigmoid(xf_ref[:, sl])
        kh = 1.0 - f
        gh = jnp.log(f)
        parts = _split3(gh)
        bcum = sum(jnp.dot(tri, p, preferred_element_type=f32) for p in parts)
        tot = sum(jnp.dot(blk, p, preferred_element_type=f32) for p in parts)
        q_in = (qh * jnp.exp(bcum)).astype(jnp.bfloat16)
        k_in = (kh * jnp.exp(-bcum)).astype(jnp.bfloat16)
        k_dec = (kh * jnp.exp(tot - bcum)).astype(jnp.bfloat16)
        vb = _bf(xv_ref[:, sl])
        att = lax.dot_general(q_in, k_in, _NT, preferred_element_type=f32)
        att = jnp.where(causal, att, 0.0).astype(jnp.bfloat16)
        o_intra = jnp.dot(att, vb, preferred_element_type=f32)
        st = st_ref[hh]
        o_inter = []
        for c in range(rows // chunk):
            rs = slice(c * chunk, (c + 1) * chunk)
            o_inter.append(lax.dot_general(q_in[rs], st.astype(jnp.bfloat16), _NT, preferred_element_type=f32))
            upd = lax.dot_general(vb[rs], k_dec[rs], _TN, preferred_element_type=f32)
            st = jnp.exp(tot[c * chunk:c * chunk + 1, :]) * st + upd
        st_ref[hh] = st
        o = o_intra + jnp.concatenate(o_inter, axis=0)
        o = o * lax.rsqrt(jnp.mean(o * o, axis=-1, keepdims=True) + EPS) * gain_ref[:, sl]
        o_ref[:, sl] = (o * jax.nn.silu(xg_ref[:, sl])).astype(o_ref.dtype)

    @pl.when(i == pl.num_programs(1) - 1)
    def _():
        for hh in range(heads):
            sfin_ref[hh] = st_ref[hh].T


def _gla_prompt(proj, lb, out_gain):
    t = proj.shape[0]
    hk = HA_HEADS * HA_DK
    assert proj.shape[1] == 4 * hk and HA_DK == HA_DV
    assert t % GLA_ROWS == 0 and GLA_ROWS % HA_CHUNK == 0 and HA_CHUNK == 32 and HA_HEADS % GLA_HEADS == 0
    w = GLA_HEADS * HA_DK
    n_hb = HA_HEADS // GLA_HEADS

    def section(c):
        return pl.BlockSpec((GLA_ROWS, w), functools.partial(lambda h, i, c: (i, c * n_hb + h), c=c))

    vec_spec = pl.BlockSpec((1, w), lambda h, i: (0, h))
    kern = functools.partial(_gla_kernel, rows=GLA_ROWS, heads=GLA_HEADS, chunk=HA_CHUNK)
    return pl.pallas_call(
        kern,
        grid=(n_hb, t // GLA_ROWS),
        in_specs=[section(0), section(1), section(2), section(3), vec_spec, vec_spec],
        out_specs=[pl.BlockSpec((GLA_ROWS, w), lambda h, i: (i, h)),
                   pl.BlockSpec((GLA_HEADS, HA_DK, HA_DV), lambda h, i: (h, 0, 0))],
        out_shape=[jax.ShapeDtypeStruct((t, hk), jnp.bfloat16),
                   jax.ShapeDtypeStruct((HA_HEADS, HA_DK, HA_DV), jnp.float32)],
        scratch_shapes=[pltpu.VMEM((GLA_HEADS, HA_DV, HA_DK), jnp.float32)],
        compiler_params=pltpu.CompilerParams(
            dimension_semantics=("arbitrary", "arbitrary"), vmem_limit_bytes=VMEM_LIMIT),
        name="gla_prompt",
    )(proj, proj, proj, proj, lb.reshape(1, hk), out_gain.reshape(1, hk))


def _page_specs(n_pages):
    return [pl.BlockSpec((PAGE * NB_KV, NB_DH), functools.partial(lambda b, pt, p: (pt[b, p], 0), p=p))
            for p in range(n_pages)]


def _cmp_pages_kernel(pt_ref, *refs, n_pages):
    kp = refs[:n_pages]
    vp = refs[n_pages:2 * n_pages]
    (w1k_ref, w1v_ref, pek_ref, pev_ref, w2k_ref, w2v_ref, gain_ref, ko_ref, vo_ref, x_ref, st_ref) = refs[2 * n_pages:]
    n_sub = n_pages * SUBS_PER_PAGE
    f32 = jnp.float32
    row = lax.broadcasted_iota(jnp.int32, (NB_KV * n_sub, 1), 0)
    last = lax.rem(row, n_sub) == n_sub - 1
    for pages, w1_ref, pe_ref, w2_ref, o_ref, norm in ((kp, w1k_ref, pek_ref, w2k_ref, ko_ref, True),
                                                       (vp, w1v_ref, pev_ref, w2v_ref, vo_ref, False)):
        for p in range(n_pages):
            for n in range(SUBS_PER_PAGE):
                st_ref[pl.ds((p * SUBS_PER_PAGE + n) * SUB_PITCH, SUB_ROWS), :] = pages[p][pl.ds(n * SUB_ROWS, SUB_ROWS), :]
        for g in range(NB_KV):
            for s in range(CMP_STRIDE):
                x_ref[g * n_sub:(g + 1) * n_sub, s * NB_DH:(s + 1) * NB_DH] = _bf(
                    st_ref[pl.ds(s * NB_KV + g, n_sub, stride=SUB_PITCH), :])
        w1 = w1_ref[...]
        a01 = jnp.dot(x_ref[...], w1, preferred_element_type=f32)
        c = jnp.dot(_bf(pe_ref[...]), w1, preferred_element_type=f32)
        a0 = a01[:, :NB_DH] + c[0:1, :NB_DH]
        a1 = a01[:, NB_DH:] + c[1:2, NB_DH:]
        pre = a0 + pltpu.roll(a1, NB_KV * n_sub - 1, 0)
        y = jnp.dot(_bf(jax.nn.gelu(pre)), w2_ref[...], preferred_element_type=f32)
        if norm:
            y = y * lax.rsqrt(jnp.mean(y * y, axis=-1, keepdims=True) + EPS) * gain_ref[...]
        o_ref[...] = jnp.where(last, 0.0, y).astype(o_ref.dtype)


def _cmp_pages(page_table, cache_k, cache_v, pe_k, pe_v, w1_k, w1_v, w2_k, w2_v, gain):
    nb, n_pages = page_table.shape
    n_phys = cache_k.shape[0]
    assert cache_k.shape[1:] == (PAGE, NB_KV, NB_DH) and CMP_LEN == 2 * CMP_STRIDE
    n_sub = n_pages * SUBS_PER_PAGE
    kd = CMP_STRIDE * NB_DH

    def w1cat(w1):
        return _bf(jnp.concatenate([w1[:CMP_STRIDE].reshape(kd, NB_DH), w1[CMP_STRIDE:].reshape(kd, NB_DH)], axis=1))

    def pecat(pe):
        z = jnp.zeros((8, kd), jnp.float32)
        return z.at[0].set(pe[:CMP_STRIDE].reshape(kd)).at[1].set(pe[CMP_STRIDE:].reshape(kd))

    def full(*shape):
        return pl.BlockSpec(shape, lambda b, pt: (0,) * len(shape))

    out_spec = pl.BlockSpec((None, NB_KV * n_sub, NB_DH), lambda b, pt: (b, 0, 0))
    ck = cache_k.reshape(n_phys * PAGE * NB_KV, NB_DH)
    cv = cache_v.reshape(n_phys * PAGE * NB_KV, NB_DH)
    grid_spec = pltpu.PrefetchScalarGridSpec(
        num_scalar_prefetch=1,
        grid=(nb,),
        in_specs=_page_specs(n_pages) * 2 + [full(kd, 2 * NB_DH), full(kd, 2 * NB_DH), full(8, kd), full(8, kd),
                                            full(NB_DH, NB_DH), full(NB_DH, NB_DH), full(1, NB_DH)],
        out_specs=[out_spec, out_spec],
        scratch_shapes=[pltpu.VMEM((NB_KV * n_sub, kd), jnp.bfloat16),
                        pltpu.VMEM((n_sub * SUB_PITCH, NB_DH), jnp.float32)],
    )
    return pl.pallas_call(
        functools.partial(_cmp_pages_kernel, n_pages=n_pages),
        grid_spec=grid_spec,
        out_shape=[jax.ShapeDtypeStruct((nb, NB_KV * n_sub, NB_DH), jnp.bfloat16)] * 2,
        compiler_params=pltpu.CompilerParams(dimension_semantics=("arbitrary",), vmem_limit_bytes=VMEM_LIMIT),
        name="cmp_pages",
    )(page_table, *([ck] * n_pages), *([cv] * n_pages), w1cat(w1_k), w1cat(w1_v), pecat(pe_k), pecat(pe_v),
      _bf(w2_k), _bf(w2_v), gain.reshape(1, NB_DH))


def _softmax_with_tail(s, mask, s_tail, ok_tail):
    m = jnp.max(s, axis=-1, keepdims=True)
    for sc in s_tail:
        m = jnp.maximum(m, sc)
    e = jnp.where(mask, jnp.exp(s - m), 0.0)
    e_tail = [jnp.where(ok, jnp.exp(sc - m), 0.0) for ok, sc in zip(ok_tail, s_tail)]
    den = jnp.sum(e, axis=-1, keepdims=True)
    for et in e_tail:
        den = den + et
    inv = 1.0 / jnp.maximum(den, 1e-30)
    return e * inv, [et * inv for et in e_tail]


def _nsa_sample_kernel(pt_ref, *refs, n_pages, n_new, win_keep):
    ksp = refs[:n_pages]
    vsp = refs[n_pages:2 * n_pages]
    (q_ref, qr_ref, kcb_ref, vcb_ref, ksn_ref, vsn_ref, kwc_ref, vwc_ref, kwn_ref, vwn_ref, ov_ref,
     oc_ref, os_ref, ow_ref, kwo_ref, vwo_ref) = refs[2 * n_pages:]
    f32 = jnp.float32
    past = n_pages * PAGE
    n_sub = past // CMP_STRIDE
    n_cmp = n_sub - CMP_LEN // CMP_STRIDE + 1
    n_slc = -(-(past + n_new) // SLC_LEN)
    new_blk = past // SLC_LEN
    assert (past % SLC_LEN) + n_new <= SLC_LEN and n_sub == LANES and n_slc <= LANES
    rows = HEADS_PER_GROUP * n_new
    all_rows = NB_KV * rows
    sel_rows = NB_KV * n_new
    scale = NB_DH ** -0.5
    groups = range(NB_KV)

    def grp_rows(g, n_tok):
        return pl.ds(g, n_tok, stride=NB_KV)

    def stack(parts):
        return jnp.concatenate(parts, axis=0)

    t_row = lax.rem(lax.broadcasted_iota(jnp.int32, (all_rows, 1), 0), n_new)
    pos = past + t_row
    pos_t = past + lax.rem(lax.broadcasted_iota(jnp.int32, (sel_rows, 1), 0), n_new)
    lane = lax.broadcasted_iota(jnp.int32, (1, LANES), 1)
    q = [q_ref[g] for g in groups]
    qr = [qr_ref[g] for g in groups]
    qrf = stack(qr).astype(f32)

    def tail_scores(new_ref):
        kb = _bf(new_ref[...]).astype(f32)
        out = []
        for i in range(n_new):
            ki = stack([jnp.broadcast_to(kb[i * NB_KV + g:i * NB_KV + g + 1, :], (rows, NB_DH)) for g in groups])
            out.append(jnp.sum(qrf * ki, axis=-1, keepdims=True) * scale)
        return out

    def tail_values(p_tail, new_ref, g):
        vb = _bf(new_ref[...]).astype(f32)
        acc = _bf(p_tail[0][g * rows:(g + 1) * rows]).astype(f32) * vb[g:g + 1, :]
        for i in range(1, n_new):
            acc = acc + _bf(p_tail[i][g * rows:(g + 1) * rows]).astype(f32) * vb[i * NB_KV + g:i * NB_KV + g + 1, :]
        return acc

    mask_c = (lane * CMP_STRIDE + (CMP_LEN - 1) <= pos) & (lane < n_cmp)
    s = stack([lax.dot_general(q[g], kcb_ref[g * n_sub:(g + 1) * n_sub, :], _NT, preferred_element_type=f32)
               for g in groups]) * scale
    s = jnp.where(mask_c, s, NEG)
    m = jnp.max(s, axis=-1, keepdims=True)
    e = jnp.where(mask_c, jnp.exp(s - m), 0.0)
    p = e / jnp.maximum(jnp.sum(e, axis=-1, keepdims=True), 1e-30)
    pb = _bf(p)
    for g in groups:
        oc_ref[g] = jnp.dot(pb[g * rows:(g + 1) * rows], vcb_ref[g * n_sub:(g + 1) * n_sub, :],
                            preferred_element_type=f32)
    psum = []
    for g in groups:
        acc = p[g * rows:g * rows + n_new]
        for j in range(1, HEADS_PER_GROUP):
            acc = acc + p[g * rows + j * n_new:g * rows + (j + 1) * n_new]
        psum.append(acc)
    psum = stack(psum)
    hi = _bf(psum)
    lo = _bf(psum - hi.astype(f32))
    ov = ov_ref[...]
    imp = jnp.dot(hi, ov, preferred_element_type=f32) + jnp.dot(lo, ov, preferred_element_type=f32)
    cur = lax.shift_right_logical(pos_t, 6)
    forced = (lane == 0) | (lane == cur) | (lane == cur - 1)
    future = lane * SLC_LEN > pos_t
    score = jnp.where(future, NEG, jnp.where(forced, BIG, imp))
    score = jnp.where(lane >= n_slc, REMOVED, score)
    lane_f = lax.broadcasted_iota(jnp.int32, (sel_rows, LANES), 1).astype(f32)
    sel = jnp.zeros((sel_rows, LANES), f32)
    for _ in range(min(N_SELECT, n_slc)):
        mx = jnp.max(score, axis=-1, keepdims=True)
        idx = jnp.min(jnp.where(score == mx, lane_f, float(LANES)), axis=-1, keepdims=True)
        hit = lane_f == idx
        sel = jnp.where(hit, 1.0, sel)
        score = jnp.where(hit, REMOVED, score)
    sel_r = stack([sel[g * n_new:(g + 1) * n_new] for g in groups for _ in range(HEADS_PER_GROUP)])
    expand = jnp.where(lax.broadcasted_iota(jnp.int32, (LANES, past), 0)
                       == lax.shift_right_logical(lax.broadcasted_iota(jnp.int32, (LANES, past), 1), 6),
                       1.0, 0.0).astype(jnp.bfloat16)
    picked = jnp.dot(_bf(sel_r), expand, preferred_element_type=f32) > 0.5
    s_all = stack([jnp.concatenate([lax.dot_general(qr[g], _bf(ksp[pg][grp_rows(g, PAGE), :]), _NT,
                                                    preferred_element_type=f32)
                                    for pg in range(n_pages)], axis=1) for g in groups]) * scale
    s_all = jnp.where(picked, s_all, NEG)
    new_sel = sel_r[:, new_blk:new_blk + 1] > 0.5
    ok_new = [new_sel & (t_row >= i) for i in range(n_new)]
    s_new = [jnp.where(ok, sc, NEG) for ok, sc in zip(ok_new, tail_scores(ksn_ref))]
    p_all, p_new = _softmax_with_tail(s_all, picked, s_new, ok_new)
    p_all = _bf(p_all)
    for g in groups:
        acc = tail_values(p_new, vsn_ref, g)
        for pg in range(n_pages):
            acc = acc + jnp.dot(p_all[g * rows:(g + 1) * rows, pg * PAGE:(pg + 1) * PAGE],
                                _bf(vsp[pg][grp_rows(g, PAGE), :]), preferred_element_type=f32)
        os_ref[g] = acc
    mask_w = lax.broadcasted_iota(jnp.int32, (1, win_keep), 1) >= t_row
    s_w = stack([lax.dot_general(qr[g], _bf(kwc_ref[grp_rows(g, win_keep), :]), _NT, preferred_element_type=f32)
                 for g in groups]) * scale
    s_w = jnp.where(mask_w, s_w, NEG)
    ok_wn = [t_row >= i for i in range(n_new)]
    s_wn = [jnp.where(ok, sc, NEG) for ok, sc in zip(ok_wn, tail_scores(kwn_ref))]
    p_w, p_wn = _softmax_with_tail(s_w, mask_w, s_wn, ok_wn)
    p_w = _bf(p_w)
    for g in groups:
        ow_ref[g] = (tail_values(p_wn, vwn_ref, g)
                     + jnp.dot(p_w[g * rows:(g + 1) * rows], _bf(vwc_ref[grp_rows(g, win_keep), :]),
                               preferred_element_type=f32))
    new_rows = n_new * NB_KV
    kept_rows = win_keep * NB_KV - new_rows
    for src_ref, add_ref, dst_ref in ((kwc_ref, kwn_ref, kwo_ref), (vwc_ref, vwn_ref, vwo_ref)):
        dst_ref[0:kept_rows, :] = src_ref[new_rows:new_rows + kept_rows, :]
        dst_ref[kept_rows:kept_rows + new_rows, :] = add_ref[...]


def _nsa_sample(page_table, q, qr, kcb, vcb, cache_sk, cache_sv, ks_new, vs_new, cache_wk, cache_wv, kw_new, vw_new):
    nb, n_pages = page_table.shape
    t = q.shape[1]
    n_phys = cache_sk.shape[0]
    keep = cache_wk.shape[1]
    assert keep == WINDOW and (t * NB_KV) % 8 == 0 and t < keep
    rows = HEADS_PER_GROUP * t
    n_sub = n_pages * SUBS_PER_PAGE

    def q_rows(a):
        a = a.reshape(nb, t, NB_KV, HEADS_PER_GROUP, NB_DH)
        return _bf(jnp.transpose(a, (0, 2, 3, 1, 4)).reshape(nb, NB_KV, rows, NB_DH))

    def q_unrows(a):
        a = a.reshape(nb, NB_KV, HEADS_PER_GROUP, t, NB_DH)
        return jnp.transpose(a, (0, 3, 1, 2, 4)).reshape(nb, t, NB_HEADS, NB_DH)

    def per_b(*shape):
        return pl.BlockSpec((None,) + shape, lambda b, pt: (b,) + (0,) * len(shape))

    def rows_view(a):
        return a.reshape(a.shape[0], a.shape[1] * NB_KV, NB_DH)

    n_cmp = n_sub - CMP_LEN // CMP_STRIDE + 1
    q_spec = per_b(NB_KV, rows, NB_DH)
    win_spec = per_b(keep * NB_KV, NB_DH)
    new_spec = per_b(t * NB_KV, NB_DH)
    cmp_spec = per_b(NB_KV * n_sub, NB_DH)
    in_specs = (_page_specs(n_pages) * 2
                + [q_spec, q_spec, cmp_spec, cmp_spec, new_spec, new_spec, win_spec, win_spec, new_spec, new_spec,
                   pl.BlockSpec((LANES, LANES), lambda b, pt: (0, 0))])
    grid_spec = pltpu.PrefetchScalarGridSpec(
        num_scalar_prefetch=1, grid=(nb,), in_specs=in_specs, out_specs=[q_spec] * 3 + [win_spec] * 2)
    sk = cache_sk.reshape(n_phys * PAGE * NB_KV, NB_DH)
    sv = cache_sv.reshape(n_phys * PAGE * NB_KV, NB_DH)
    o_c, o_s, o_w, wk_next, wv_next = pl.pallas_call(
        functools.partial(_nsa_sample_kernel, n_pages=n_pages, n_new=t, win_keep=keep),
        grid_spec=grid_spec,
        out_shape=([jax.ShapeDtypeStruct((nb, NB_KV, rows, NB_DH), jnp.float32)] * 3
                   + [jax.ShapeDtypeStruct((nb, keep * NB_KV, NB_DH), jnp.float32)] * 2),
        compiler_params=pltpu.CompilerParams(dimension_semantics=("arbitrary",), vmem_limit_bytes=VMEM_LIMIT),
        name="nsa_sample",
    )(page_table, *([sk] * n_pages), *([sv] * n_pages), q_rows(q), q_rows(qr), kcb, vcb,
      rows_view(ks_new), rows_view(vs_new), rows_view(cache_wk), rows_view(cache_wv), rows_view(kw_new),
      rows_view(vw_new), _overlap_matrix(LANES, n_cmp, LANES))
    return (q_unrows(o_c), q_unrows(o_s), q_unrows(o_w),
            wk_next.reshape(cache_wk.shape), wv_next.reshape(cache_wv.shape))


def _mm_kernel(x_ref, w_ref, o_ref, wb_ref):
    @pl.when(pl.program_id(1) == 0)
    def _():
        wb_ref[...] = w_ref[...].astype(jnp.bfloat16)

    o_ref[...] = jnp.dot(x_ref[...], wb_ref[...], preferred_element_type=jnp.float32)


def _mm_tiles(m, k, n):
    if k <= 2048:
        tm = 1024
        tn = 1024 if (n % 1024 == 0 or n >= 4096) else 512
    else:
        tm, tn = 512, 512
    return min(tm, m), min(tn, n)


def _mm(x, w):
    lead = x.shape[:-1]
    k = x.shape[-1]
    n = w.shape[1]
    x2 = _bf(x.reshape(-1, k))
    m = x2.shape[0]
    tm, tn = _mm_tiles(m, k, n)
    assert m % tm == 0
    out = pl.pallas_call(
        _mm_kernel,
        grid=(pl.cdiv(n, tn), m // tm),
        in_specs=[pl.BlockSpec((tm, k), lambda j, i: (i, 0)),
                  pl.BlockSpec((k, tn), lambda j, i: (0, j))],
        out_specs=pl.BlockSpec((tm, tn), lambda j, i: (i, j)),
        out_shape=jax.ShapeDtypeStruct((m, n), jnp.float32),
        scratch_shapes=[pltpu.VMEM((k, tn), jnp.bfloat16)],
        compiler_params=pltpu.CompilerParams(
            dimension_semantics=("arbitrary", "arbitrary"), vmem_limit_bytes=VMEM_LIMIT),
        name="mm",
    )(x2, w)
    return out.reshape(*lead, n)


def _ffn_up_kernel(x_ref, wu_ref, wg_ref, cw_ref, cb_ref, prev_ref, act_ref, state_ref, wub_ref, wgb_ref, halo_ref):
    m = pl.program_id(1)
    tm = x_ref.shape[0]
    f32 = jnp.float32

    @pl.when(m == 0)
    def _():
        wub_ref[...] = _bf(wu_ref[...])
        wgb_ref[...] = _bf(wg_ref[...])
        halo_ref[0:CONV_W - 1, :] = prev_ref[...].astype(f32)

    x = x_ref[...]
    u = jnp.dot(x, wub_ref[...], preferred_element_type=f32)
    g = jnp.dot(x, wgb_ref[...], preferred_element_type=f32)
    row = lax.broadcasted_iota(jnp.int32, (tm, 1), 0)
    before1 = halo_ref[1:2, :]
    before2 = halo_ref[0:1, :]
    g1 = jnp.where(row == 0, before1, pltpu.roll(g, 1, 0))
    g2 = pltpu.roll(g, 2, 0)
    g2 = jnp.where(row == 0, before2, jnp.where(row == 1, before1, g2))
    conv = cb_ref[...] + cw_ref[0:1, :] * g2 + cw_ref[1:2, :] * g1 + cw_ref[2:3, :] * g
    act_ref[...] = (jax.nn.silu(conv) * u).astype(act_ref.dtype)
    tail = g[tm - (CONV_W - 1):tm, :]
    halo_ref[0:CONV_W - 1, :] = tail
    state_ref[...] = tail


def _ffn_up_prompt(x, w_up, conv_w, conv_b, prev):
    m, k = x.shape
    assert CONV_W == 3 and m % FFN_TM == 0 and w_up.shape == (k, 2 * D_FF)
    w_u, w_g = w_up[:, :D_FF], w_up[:, D_FF:]

    def col(r):
        return pl.BlockSpec((r, FFN_TN), lambda j, i: (0, j))

    return pl.pallas_call(
        _ffn_up_kernel,
        grid=(pl.cdiv(D_FF, FFN_TN), m // FFN_TM),
        in_specs=[pl.BlockSpec((FFN_TM, k), lambda j, i: (i, 0)), col(k), col(k), col(CONV_W), col(1), col(CONV_W - 1)],
        out_specs=[pl.BlockSpec((FFN_TM, FFN_TN), lambda j, i: (i, j)), col(CONV_W - 1)],
        out_shape=[jax.ShapeDtypeStruct((m, D_FF), jnp.bfloat16),
                   jax.ShapeDtypeStruct((CONV_W - 1, D_FF), jnp.float32)],
        scratch_shapes=[pltpu.VMEM((k, FFN_TN), jnp.bfloat16), pltpu.VMEM((k, FFN_TN), jnp.bfloat16),
                        pltpu.VMEM((8, FFN_TN), jnp.float32)],
        compiler_params=pltpu.CompilerParams(
            dimension_semantics=("arbitrary", "arbitrary"), vmem_limit_bytes=VMEM_LIMIT),
        name="ffn_up_prompt",
    )(_bf(x), w_u, w_g, conv_w, conv_b.reshape(1, D_FF), prev)


def _rmsnorm(x, g):
    xf = x.astype(jnp.float32)
    y = xf * lax.rsqrt(jnp.mean(xf * xf, axis=-1, keepdims=True) + EPS)
    return (y * g.astype(jnp.float32)).astype(x.dtype)


def _rope(x, pos):
    half = x.shape[-1] // 2
    inv = ROPE_THETA ** (-jnp.arange(half, dtype=jnp.float32) / half)
    ang = pos.astype(jnp.float32)[:, None] * inv[None, :]
    cos, sin = jnp.cos(ang)[:, None, :], jnp.sin(ang)[:, None, :]
    xf = x.astype(jnp.float32)
    x1, x2 = xf[..., :half], xf[..., half:]
    return jnp.concatenate([x1 * cos - x2 * sin, x2 * cos + x1 * sin], axis=-1).astype(x.dtype)


def _gla_chunked(q, k, v, logf, s0):
    b, t, h, dk = q.shape
    dv = v.shape[-1]
    c = math.gcd(t, HA_CHUNK)
    n = t // c

    def to_chunks(a):
        return jnp.transpose(a.reshape(b, n, c, h, a.shape[-1]), (1, 0, 3, 2, 4))

    qc, kc, vc, gc = to_chunks(q), to_chunks(k), to_chunks(v), to_chunks(logf)
    bcum = jnp.cumsum(gc, axis=3)
    blast = bcum[:, :, :, -1, :]
    q_in = qc * jnp.exp(bcum)
    k_in = kc * jnp.exp(-bcum)
    k_dec = kc * jnp.exp(blast[:, :, :, None, :] - bcum)
    causal = jnp.tril(jnp.ones((c, c), dtype=bool))
    att = jnp.where(causal, jnp.einsum('nbhik,nbhjk->nbhij', q_in, k_in), 0.0)
    o_intra = jnp.einsum('nbhij,nbhjv->nbhiv', att, vc)

    def step(s, inp):
        qi, kd, vi, dl = inp
        o_inter = jnp.einsum('bhik,bhkv->bhiv', qi, s)
        s = jnp.exp(dl)[..., None] * s + jnp.einsum('bhik,bhiv->bhkv', kd, vi)
        return s, o_inter

    s_final, o_inter = lax.scan(step, s0.astype(jnp.float32), (q_in, k_dec, vc, blast))
    o = o_intra + o_inter
    return jnp.transpose(o, (1, 0, 3, 2, 4)).reshape(b, t, h, dv), s_final


def _hgrn2_layer(xn, s0, lb, w_in, out_gain, w_out):
    b, t, _ = xn.shape
    hk = HA_HEADS * HA_DK
    hv = HA_HEADS * HA_DV
    proj = _mm(xn, w_in)
    if s0 is None:
        assert b == 1
        o, s_new = _gla_prompt(proj[0], lb, out_gain)
        return _mm(o[None], w_out), s_new[None]
    q = jax.nn.silu(proj[..., :hk])
    f = lb + (1.0 - lb) * jax.nn.sigmoid(proj[..., hk:2 * hk])
    i_in = proj[..., 2 * hk:2 * hk + hv]
    gate = proj[..., 2 * hk + hv:]
    shp_k = (b, t, HA_HEADS, HA_DK)
    shp_v = (b, t, HA_HEADS, HA_DV)
    o, s_new = _gla_chunked(q.reshape(shp_k), (1.0 - f).reshape(shp_k), i_in.reshape(shp_v),
                            jnp.log(f).reshape(shp_k), s0)
    o = _rmsnorm(o, out_gain.reshape(HA_HEADS, HA_DV)) * jax.nn.silu(gate).reshape(shp_v)
    return _mm(o.reshape(b, t, hv), w_out), s_new


def _conv_ffn(xn, prev, w_up, conv_w, conv_b, w_down):
    b, t, _ = xn.shape
    if b == 1 and t % FFN_TM == 0:
        act, state = _ffn_up_prompt(xn[0], w_up, conv_w, conv_b, prev[0])
        return _mm(act[None], w_down), state[None].astype(xn.dtype)
    up = _mm(xn, w_up)
    u, gt = up[..., :D_FF], up[..., D_FF:]
    ext = jnp.concatenate([prev.astype(gt.dtype), gt], axis=1)
    conv = conv_b
    for j in range(CONV_W):
        conv = conv + conv_w[j] * ext[:, j:j + t]
    return _mm(jax.nn.silu(conv) * u, w_down), ext[:, t:]


def _ple(h, p, gain, w_p, w_g):
    gate = jax.nn.sigmoid(_mm(_rmsnorm(h, gain), w_g))
    return h + _mm(p, w_p) * gate


def _compress(rows, pe, w1, w2):
    b, l, g, d = rows.shape
    n_sub = l // CMP_STRIDE
    r = CMP_LEN // CMP_STRIDE
    n_cmp = n_sub - r + 1
    sub = rows[:, :n_sub * CMP_STRIDE].reshape(b, n_sub, CMP_STRIDE, g, d)
    pre = None
    for j in range(r):
        sl = slice(j * CMP_STRIDE, (j + 1) * CMP_STRIDE)
        part = jnp.einsum('bnsgd,sde->bnge', sub + pe[sl][None, None, :, None, :], w1[sl])[:, j:j + n_cmp]
        pre = part if pre is None else pre + part
    return jnp.einsum('bnge,ef->bngf', jax.nn.gelu(pre), w2)


def _nsa_layer(xn, pos, shared, w_in, q_gain, w_out, banded):
    b, t, _ = xn.shape
    hd = NB_HEADS * NB_DH
    proj = _mm(xn, w_in)
    q = _rmsnorm(proj[..., :hd].reshape(b, t, NB_HEADS, NB_DH), q_gain)
    gates = jax.nn.sigmoid(proj[..., hd:].astype(jnp.float32)).reshape(b, t, 3, NB_HEADS)
    qr = _rope(q, pos)
    win_next = None
    if banded:
        assert b == 1
        kcb, vcb, kpad, vpad, win_k, win_v = shared
        o_c2, sel = _cmp_topk_prompt(q.reshape(t, hd), kcb[0], vcb[0])
        o_s2, o_w2 = _slc_win_prompt(qr.reshape(t, hd), kpad[0], vpad[0], win_k[0], win_v[0], sel)
        o_c = o_c2.reshape(b, t, NB_HEADS, NB_DH)
        o_s = o_s2.reshape(b, t, NB_HEADS, NB_DH)
        o_w = o_w2.reshape(b, t, NB_HEADS, NB_DH)
    else:
        o_c, o_s, o_w, wk_next, wv_next = _nsa_sample(shared[0], q, qr, *shared[1:])
        win_next = (wk_next, wv_next)
    o = (gates[:, :, 0, :, None] * o_c + gates[:, :, 1, :, None] * o_s + gates[:, :, 2, :, None] * o_w)
    return _mm(o.reshape(b, t, hd), w_out), win_next


def _shared_kv(h, pos, past, wt):
    b, t, _ = h.shape
    kv = _mm(_rmsnorm(h, wt['kv_norm']), wt['kv_w']).reshape(b, t, 6, NB_KV, NB_DH)
    k_cmp, v_cmp = kv[:, :, 0], kv[:, :, 1]
    k_slc = _rope(_rmsnorm(kv[:, :, 2], wt['k_norm_slc']), pos)
    v_slc = kv[:, :, 3]
    k_win = _rope(_rmsnorm(kv[:, :, 4], wt['k_norm_win']), pos)
    v_win = kv[:, :, 5]
    if past is None:
        keep = min(WINDOW, t)
        kcb = _rmsnorm(_compress(k_cmp, wt['cmp_pe_k'], wt['cmp_w1_k'], wt['cmp_w2_k']), wt['k_norm_cmp'])
        vcb = _compress(v_cmp, wt['cmp_pe_v'], wt['cmp_w1_v'], wt['cmp_w2_v'])
        assert t % SLC_LEN == 0
        shared = (kcb, vcb, k_slc, v_slc, k_win, v_win)
        rows = (k_cmp, v_cmp, k_slc, v_slc, k_win[:, -keep:], v_win[:, -keep:])
    else:
        page_table, ckc, cvc, cks, cvs, cwk, cwv = past
        assert t < CMP_STRIDE and (page_table.shape[1] * PAGE) % CMP_STRIDE == 0
        kcb, vcb = _cmp_pages(page_table, ckc, cvc, wt['cmp_pe_k'], wt['cmp_pe_v'], wt['cmp_w1_k'], wt['cmp_w1_v'],
                              wt['cmp_w2_k'], wt['cmp_w2_v'], wt['k_norm_cmp'])
        shared = (page_table, kcb, vcb, cks, cvs, k_slc, v_slc, cwk, cwv, k_win, v_win)
        rows = (k_cmp, v_cmp, k_slc, v_slc, None, None)
    return shared, rows


def _trunk(x, p, start, s_hgrn, s_conv, past, wt):
    b, t, _ = x.shape
    pos = start + jnp.arange(t)
    banded = past is None
    lb = jnp.cumsum(jax.nn.softmax(wt['a_lb_logits'].astype(jnp.float32), axis=0), axis=0)
    h = x
    hgrn_out, conv_out = [], []
    shared, rows = None, None
    for i in range(DEPTH):
        hn = _rmsnorm(h, wt['norm_mix'][i])
        if i < N_A:
            y, s = _hgrn2_layer(hn, None if s_hgrn is None else s_hgrn[i], lb[i], wt['a_w_in'][i],
                                wt['a_out_norm'][i], wt['a_w_out'][i])
            hgrn_out.append(s)
        else:
            j = i - N_A
            y, win_next = _nsa_layer(hn, pos, shared, wt['b_w_in'][j], wt['b_q_norm'][j], wt['b_w_out'][j], banded)
            if win_next is not None:
                assert N_B == 1
                rows = rows[:4] + win_next
        h = h + y
        f, cs = _conv_ffn(_rmsnorm(h, wt['norm_ffn'][i]), s_conv[i], wt['ffn_w_up'][i], wt['ffn_conv_w'][i],
                          wt['ffn_conv_b'][i], wt['ffn_w_down'][i])
        conv_out.append(cs)
        h = h + f
        h = _ple(h, p[i], wt['norm_ple'][i], wt['ple_w_p'][i], wt['ple_w_g'][i])
        if i == N_A - 1:
            shared, rows = _shared_kv(h, pos, past, wt)
    return h, jnp.stack(hgrn_out), jnp.stack(conv_out), rows


def kernel(x_prompt, x_sample, state_hgrn, state_conv, cache_cmp_k, cache_cmp_v, cache_slc_k, cache_slc_v,
           cache_win_k, cache_win_v, page_table, p_prompt, p_sample, norm_mix, norm_ffn, norm_ple,
           a_w_in, a_lb_logits, a_out_norm, a_w_out, kv_norm, kv_w, k_norm_cmp, k_norm_slc, k_norm_win,
           cmp_pe_k, cmp_pe_v, cmp_w1_k, cmp_w2_k, cmp_w1_v, cmp_w2_v, b_w_in, b_q_norm, b_w_out,
           ffn_w_up, ffn_conv_w, ffn_conv_b, ffn_w_down, ple_w_p, ple_w_g):
    wt = {
        'norm_mix': norm_mix, 'norm_ffn': norm_ffn, 'norm_ple': norm_ple,
        'a_w_in': a_w_in, 'a_lb_logits': a_lb_logits, 'a_out_norm': a_out_norm, 'a_w_out': a_w_out,
        'kv_norm': kv_norm, 'kv_w': kv_w, 'k_norm_cmp': k_norm_cmp, 'k_norm_slc': k_norm_slc,
        'k_norm_win': k_norm_win, 'cmp_pe_k': cmp_pe_k, 'cmp_pe_v': cmp_pe_v, 'cmp_w1_k': cmp_w1_k,
        'cmp_w2_k': cmp_w2_k, 'cmp_w1_v': cmp_w1_v, 'cmp_w2_v': cmp_w2_v,
        'b_w_in': b_w_in, 'b_q_norm': b_q_norm, 'b_w_out': b_w_out,
        'ffn_w_up': ffn_w_up, 'ffn_conv_w': ffn_conv_w, 'ffn_conv_b': ffn_conv_b, 'ffn_w_down': ffn_w_down,
        'ple_w_p': ple_w_p, 'ple_w_g': ple_w_g,
    }
    bp, past_len = x_prompt.shape[0], page_table.shape[1] * cache_cmp_k.shape[1]
    s_conv0 = jnp.zeros((DEPTH, bp, CONV_W - 1, D_FF), x_prompt.dtype)
    y_prompt, hgrn_p, conv_p, rows_p = _trunk(x_prompt, p_prompt, 0, None, s_conv0, None, wt)
    past = (page_table, cache_cmp_k, cache_cmp_v, cache_slc_k, cache_slc_v, cache_win_k, cache_win_v)
    y_sample, hgrn_s, conv_s, rows_s = _trunk(x_sample, p_sample, past_len, state_hgrn, state_conv, past, wt)
    kc_p, vc_p, ks_p, vs_p, wk_p, wv_p = rows_p
    kc_s, vc_s, ks_s, vs_s, wk_s, wv_s = rows_s
    return (y_prompt, y_sample, hgrn_p, hgrn_s, conv_p, conv_s, kc_p, vc_p, ks_p, vs_p,
            kc_s, vc_s, ks_s, vs_s, wk_p, wv_p, wk_s, wv_s)
```

```python
import functools
import math

import jax
import jax.numpy as jnp
from jax import lax
from jax.experimental import pallas as pl
from jax.experimental.pallas import tpu as pltpu

D_MODEL = 2048
DEPTH = 2
N_A = DEPTH // 2
N_B = DEPTH - N_A
EPS = 1e-6
HA_DK = 128
HA_HEADS = D_MODEL // HA_DK
HA_DV = D_MODEL // HA_HEADS
HA_CHUNK = 32
NB_DH = 128
NB_HEADS = D_MODEL // NB_DH
NB_KV = 4
HEADS_PER_GROUP = NB_HEADS // NB_KV
CMP_LEN = 32
CMP_STRIDE = 16
SLC_LEN = 64
N_SELECT = 16
WINDOW = 512
Q_BLOCK = 128
ROPE_THETA = 10000.0
D_FF = ((8 * D_MODEL // 3 + 127) // 128) * 128
CONV_W = 3
NEG = -1e30
BIG = 1e30
LOG2_E = math.log2(math.e)
REMOVED = -3.0e38
LANES = 128
SLC_KV_TILE = 1024
WIN_SPAN = WINDOW + Q_BLOCK
GLA_ROWS = 256
GLA_HEADS = 4
PAGE = 128
SUBS_PER_PAGE = PAGE // CMP_STRIDE
SUB_ROWS = CMP_STRIDE * NB_KV
SUB_PITCH = SUB_ROWS + 8
FFN_TM = 1024
FFN_TN = 512
SLC_GROUPS = 2
VMEM_LIMIT = 48 * 1024 * 1024
SLC_VMEM_LIMIT = 56 * 1024 * 1024
_NT = (((1,), (1,)), ((), ()))
_TN = (((0,), (0,)), ((), ()))


def _bf(x):
    return x.astype(jnp.bfloat16)


def _cmp_topk_kernel(q_ref, k_ref, v_ref, ov_ref, o_ref, sel_ref, *, q_blk, n_cmp_pad, n_slc, n_sel):
    i = pl.program_id(0)
    f32 = jnp.float32
    pos = i * q_blk + lax.broadcasted_iota(jnp.int32, (q_blk, 1), 0)
    kend = lax.broadcasted_iota(jnp.int32, (1, n_cmp_pad), 1) * CMP_STRIDE + (CMP_LEN - 1)
    mask_c = kend <= pos
    scale = NB_DH ** -0.5
    ov = ov_ref[...]
    imp = []
    for g in range(NB_KV):
        k = k_ref[g]
        v = v_ref[g]
        psum = jnp.zeros((q_blk, n_cmp_pad), f32)
        for j in range(HEADS_PER_GROUP):
            cs = slice((g * HEADS_PER_GROUP + j) * NB_DH, (g * HEADS_PER_GROUP + j + 1) * NB_DH)
            s = lax.dot_general(_bf(q_ref[:, cs]), k, _NT, preferred_element_type=f32) * scale
            s = jnp.where(mask_c, s, NEG)
            m = jnp.max(s, axis=-1, keepdims=True)
            e = jnp.where(mask_c, jnp.exp(s - m), 0.0)
            p = e / jnp.maximum(jnp.sum(e, axis=-1, keepdims=True), 1e-30)
            o_ref[:, cs] = jnp.dot(_bf(p), v, preferred_element_type=f32)
            psum = psum + p
        hi = _bf(psum)
        lo = _bf(psum - hi.astype(f32))
        imp.append(jnp.dot(hi, ov, preferred_element_type=f32) + jnp.dot(lo, ov, preferred_element_type=f32))
    imp = jnp.concatenate(imp, axis=0)
    pos_r = jnp.concatenate([pos] * NB_KV, axis=0)
    sidx = lax.broadcasted_iota(jnp.int32, (1, n_slc), 1)
    cur = lax.shift_right_logical(pos_r, 6)
    forced = (sidx == 0) | (sidx == cur) | (sidx == cur - 1)
    future = sidx * SLC_LEN > pos_r
    score = jnp.where(future, NEG, jnp.where(forced, BIG, imp))
    lane = lax.broadcasted_iota(jnp.int32, (NB_KV * q_blk, n_slc), 1).astype(f32)
    sel = jnp.zeros((NB_KV * q_blk, n_slc), f32)
    for _ in range(n_sel):
        m = jnp.max(score, axis=-1, keepdims=True)
        idx = jnp.min(jnp.where(score == m, lane, float(n_slc)), axis=-1, keepdims=True)
        hit = lane == idx
        sel = jnp.where(hit, 1.0, sel)
        score = jnp.where(hit, REMOVED, score)
    for g in range(NB_KV):
        sel_ref[:, g * n_slc:(g + 1) * n_slc] = sel[g * q_blk:(g + 1) * q_blk].astype(sel_ref.dtype)


def _overlap_matrix(n_cmp_rows, n_cmp, n_slc_cols):
    cstart = jnp.arange(n_cmp_rows) * CMP_STRIDE
    sstart = jnp.arange(n_slc_cols) * SLC_LEN
    return _bf((cstart[:, None] <= sstart[None, :] + (SLC_LEN - 1))
               & (cstart[:, None] + (CMP_LEN - 1) >= sstart[None, :])
               & (jnp.arange(n_cmp_rows)[:, None] < n_cmp))


def _cmp_topk_prompt(q2d, kcb, vcb):
    t = q2d.shape[0]
    n_cmp = kcb.shape[0]
    n_cmp_pad = -(-n_cmp // LANES) * LANES
    n_slc = t // SLC_LEN
    assert SLC_LEN == 64 and n_slc == LANES and t % Q_BLOCK == 0
    n_sel = min(N_SELECT, n_slc)
    padc = ((0, 0), (0, n_cmp_pad - n_cmp), (0, 0))
    k = _bf(jnp.pad(jnp.transpose(kcb, (1, 0, 2)), padc))
    v = _bf(jnp.pad(jnp.transpose(vcb, (1, 0, 2)), padc))
    hd = NB_HEADS * NB_DH
    kern = functools.partial(_cmp_topk_kernel, q_blk=Q_BLOCK, n_cmp_pad=n_cmp_pad, n_slc=n_slc, n_sel=n_sel)
    kv_spec = pl.BlockSpec((NB_KV, n_cmp_pad, NB_DH), lambda i: (0, 0, 0))
    return pl.pallas_call(
        kern,
        grid=(t // Q_BLOCK,),
        in_specs=[pl.BlockSpec((Q_BLOCK, hd), lambda i: (i, 0)), kv_spec, kv_spec,
                  pl.BlockSpec((n_cmp_pad, n_slc), lambda i: (0, 0))],
        out_specs=[pl.BlockSpec((Q_BLOCK, hd), lambda i: (i, 0)),
                   pl.BlockSpec((Q_BLOCK, NB_KV * n_slc), lambda i: (i, 0))],
        out_shape=[jax.ShapeDtypeStruct((t, hd), jnp.float32),
                   jax.ShapeDtypeStruct((t, NB_KV * n_slc), jnp.bfloat16)],
        compiler_params=pltpu.CompilerParams(dimension_semantics=("arbitrary",), vmem_limit_bytes=VMEM_LIMIT),
        name="cmp_topk_prompt",
    )(q2d, k, v, _overlap_matrix(n_cmp_pad, n_cmp, n_slc))


def _slc_win_kernel(q_ref, ks_ref, vs_ref, kw_ref, vw_ref, sel_ref, os_ref, ow_ref, m_ref, l_ref, acc_ref,
                    *, q_blk, n_slc, kv_tile, n_grp):
    i = pl.program_id(1)
    rows = HEADS_PER_GROUP * q_blk
    gd = HEADS_PER_GROUP * NB_DH
    blocks_per_tile = kv_tile // SLC_LEN
    reps = kv_tile // LANES
    scale = NB_DH ** -0.5
    pos = i * q_blk + lax.broadcasted_iota(jnp.int32, (q_blk, 1), 0)
    q4 = [jnp.concatenate([q_ref[:, gg * gd + j * NB_DH:gg * gd + (j + 1) * NB_DH] for j in range(HEADS_PER_GROUP)],
                          axis=0) for gg in range(n_grp)]
    sel = [sel_ref[:, gg * n_slc:(gg + 1) * n_slc] for gg in range(n_grp)]
    m_ref[...] = jnp.full(m_ref.shape, NEG, jnp.float32)
    l_ref[...] = jnp.zeros(l_ref.shape, jnp.float32)
    acc_ref[...] = jnp.zeros(acc_ref.shape, jnp.float32)
    blk_row = lax.broadcasted_iota(jnp.int32, (n_slc, kv_tile), 0)
    blk_col = lax.shift_right_logical(lax.broadcasted_iota(jnp.int32, (n_slc, kv_tile), 1), 6)
    key_off = lax.broadcasted_iota(jnp.int32, (1, kv_tile), 1)
    n_tiles = lax.div(i * q_blk + (q_blk - 1), kv_tile) + 1

    def body(j, carry):
        start = pl.multiple_of(j * kv_tile, kv_tile)
        expand = _bf(jnp.where(blk_row == blk_col + j * blocks_per_tile, 1.0, 0.0))
        causal = key_off + start <= pos
        for gg in range(n_grp):
            kt = ks_ref[gg, pl.ds(start, kv_tile), :]
            vt = vs_ref[gg, pl.ds(start, kv_tile), :]
            s = lax.dot_general(q4[gg], kt, _NT, preferred_element_type=jnp.float32) * (scale * LOG2_E)
            picked = jnp.dot(sel[gg], expand, preferred_element_type=jnp.float32) > 0.5
            mask = picked & causal
            s = jnp.where(mask[None], s.reshape(HEADS_PER_GROUP, q_blk, kv_tile), NEG).reshape(rows, kv_tile)
            m_prev = m_ref[gg]
            m_new = jnp.maximum(m_prev, jnp.max(s, axis=-1, keepdims=True))
            alpha = jnp.exp2(m_prev - m_new)
            p = jnp.exp2(s - jnp.concatenate([m_new] * reps, axis=1))
            l_ref[gg] = alpha * l_ref[gg] + jnp.sum(p, axis=-1, keepdims=True)
            acc_ref[gg] = alpha * acc_ref[gg] + jnp.dot(_bf(p), vt, preferred_element_type=jnp.float32)
            m_ref[gg] = m_new
        return carry

    lax.fori_loop(0, n_tiles, body, 0)

    wstart = pl.multiple_of(jnp.maximum(i - WINDOW // q_blk, 0) * q_blk, q_blk)
    diff = pos - (wstart + lax.broadcasted_iota(jnp.int32, (1, WIN_SPAN), 1))
    mask = ((diff >= 0) & (diff <= WINDOW))[None]
    n_ghost = jnp.maximum(WINDOW - pos, 0).astype(jnp.float32)[None]
    for gg in range(n_grp):
        out = acc_ref[gg] / jnp.maximum(l_ref[gg], 1e-30)
        for j in range(HEADS_PER_GROUP):
            os_ref[:, gg * gd + j * NB_DH:gg * gd + (j + 1) * NB_DH] = out[j * q_blk:(j + 1) * q_blk, :]
        kt = kw_ref[gg, pl.ds(wstart, WIN_SPAN), :]
        vt = vw_ref[gg, pl.ds(wstart, WIN_SPAN), :]
        s = lax.dot_general(q4[gg], kt, _NT, preferred_element_type=jnp.float32) * scale
        s = jnp.where(mask, s.reshape(HEADS_PER_GROUP, q_blk, WIN_SPAN), NEG)
        m = jnp.max(s, axis=-1, keepdims=True)
        m = jnp.where(n_ghost > 0, jnp.maximum(m, 0.0), m)
        e = jnp.where(mask, jnp.exp(s - m), 0.0)
        p = e / jnp.maximum(jnp.sum(e, axis=-1, keepdims=True) + n_ghost * jnp.exp(-m), 1e-30)
        ow = jnp.dot(_bf(p.reshape(rows, WIN_SPAN)), vt, preferred_element_type=jnp.float32)
        for j in range(HEADS_PER_GROUP):
            ow_ref[:, gg * gd + j * NB_DH:gg * gd + (j + 1) * NB_DH] = ow[j * q_blk:(j + 1) * q_blk, :]


def _slc_win_prompt(qr2d, kpad, vpad, kwin, vwin, sel):
    t = qr2d.shape[0]
    n_slc = t // SLC_LEN
    assert t % SLC_KV_TILE == 0 and SLC_KV_TILE % SLC_LEN == 0 and t >= WIN_SPAN and NB_KV % SLC_GROUPS == 0

    def group_major(a):
        return _bf(jnp.transpose(a, (1, 0, 2)))

    gd = SLC_GROUPS * HEADS_PER_GROUP * NB_DH
    rows = HEADS_PER_GROUP * Q_BLOCK
    kern = functools.partial(_slc_win_kernel, q_blk=Q_BLOCK, n_slc=n_slc, kv_tile=SLC_KV_TILE, n_grp=SLC_GROUPS)
    kv_spec = pl.BlockSpec((SLC_GROUPS, t, NB_DH), lambda g, i: (g, 0, 0))
    o_spec = pl.BlockSpec((Q_BLOCK, gd), lambda g, i: (i, g))
    return pl.pallas_call(
        kern,
        grid=(NB_KV // SLC_GROUPS, t // Q_BLOCK),
        in_specs=[o_spec, kv_spec, kv_spec, kv_spec, kv_spec,
                  pl.BlockSpec((Q_BLOCK, SLC_GROUPS * n_slc), lambda g, i: (i, g))],
        out_specs=[o_spec, o_spec],
        out_shape=[jax.ShapeDtypeStruct((t, NB_HEADS * NB_DH), jnp.float32)] * 2,
        scratch_shapes=[
            pltpu.VMEM((SLC_GROUPS, rows, LANES), jnp.float32),
            pltpu.VMEM((SLC_GROUPS, rows, LANES), jnp.float32),
            pltpu.VMEM((SLC_GROUPS, rows, NB_DH), jnp.float32),
        ],
        compiler_params=pltpu.CompilerParams(
            dimension_semantics=("arbitrary", "arbitrary"), vmem_limit_bytes=SLC_VMEM_LIMIT),
        name="slc_win_prompt",
    )(_bf(qr2d), group_major(kpad), group_major(vpad), group_major(kwin), group_major(vwin), sel)


def _split3(x):
    hi = x.astype(jnp.bfloat16)
    r = x - hi.astype(jnp.float32)
    mid = r.astype(jnp.bfloat16)
    lo = (r - mid.astype(jnp.float32)).astype(jnp.bfloat16)
    return hi, mid, lo


def _gla_kernel(xq_ref, xf_ref, xv_ref, xg_ref, lb_ref, gain_ref, o_ref, sfin_ref, st_ref, *, rows, heads, chunk):
    i = pl.program_id(1)

    @pl.when(i == 0)
    def _():
        st_ref[...] = jnp.zeros(st_ref.shape, jnp.float32)

    r_idx = lax.broadcasted_iota(jnp.int32, (rows, rows), 0)
    c_idx = lax.broadcasted_iota(jnp.int32, (rows, rows), 1)
    same = lax.shift_right_logical(r_idx, 5) == lax.shift_right_logical(c_idx, 5)
    causal = same & (c_idx <= r_idx)
    tri = jnp.where(causal, 1.0, 0.0).astype(jnp.bfloat16)
    blk = jnp.where(same, 1.0, 0.0).astype(jnp.bfloat16)
    f32 = jnp.float32
    for hh in range(heads):
        sl = slice(hh * HA_DK, (hh + 1) * HA_DK)
        lb = lb_ref[:, sl]
        qh = jax.nn.silu(xq_ref[:, sl])
        f = lb + (1.0 - lb) * jax.nn.sigmoid(xf_ref[:, sl])
        kh = 1.0 - f
        gh = jnp.log(f)
        parts = _split3(gh)
        bcum = sum(jnp.dot(tri, p, preferred_element_type=f32) for p in parts)
        tot = sum(jnp.dot(blk, p, preferred_element_type=f32) for p in parts)
        q_in = (qh * jnp.exp(bcum)).astype(jnp.bfloat16)
        k_in = (kh * jnp.exp(-bcum)).astype(jnp.bfloat16)
        k_dec = (kh * jnp.exp(tot - bcum)).astype(jnp.bfloat16)
        vb = _bf(xv_ref[:, sl])
        att = lax.dot_general(q_in, k_in, _NT, preferred_element_type=f32)
        att = jnp.where(causal, att, 0.0).astype(jnp.bfloat16)
        o_intra = jnp.dot(att, vb, preferred_element_type=f32)
        st = st_ref[hh]
        o_inter = []
        for c in range(rows // chunk):
            rs = slice(c * chunk, (c + 1) * chunk)
            o_inter.append(lax.dot_general(q_in[rs], st.astype(jnp.bfloat16), _NT, preferred_element_type=f32))
            upd = lax.dot_general(vb[rs], k_dec[rs], _TN, preferred_element_type=f32)
            st = jnp.exp(tot[c * chunk:c * chunk + 1, :]) * st + upd
        st_ref[hh] = st
        o = o_intra + jnp.concatenate(o_inter, axis=0)
        o = o * lax.rsqrt(jnp.mean(o * o, axis=-1, keepdims=True) + EPS) * gain_ref[:, sl]
        o_ref[:, sl] = (o * jax.nn.silu(xg_ref[:, sl])).astype(o_ref.dtype)

    @pl.when(i == pl.num_programs(1) - 1)
    def _():
        for hh in range(heads):
            sfin_ref[hh] = st_ref[hh].T


def _gla_prompt(proj, lb, out_gain):
    t = proj.shape[0]
    hk = HA_HEADS * HA_DK
    assert proj.shape[1] == 4 * hk and HA_DK == HA_DV
    assert t % GLA_ROWS == 0 and GLA_ROWS % HA_CHUNK == 0 and HA_CHUNK == 32 and HA_HEADS % GLA_HEADS == 0
    w = GLA_HEADS * HA_DK
    n_hb = HA_HEADS // GLA_HEADS

    def section(c):
        return pl.BlockSpec((GLA_ROWS, w), functools.partial(lambda h, i, c: (i, c * n_hb + h), c=c))

    vec_spec = pl.BlockSpec((1, w), lambda h, i: (0, h))
    kern = functools.partial(_gla_kernel, rows=GLA_ROWS, heads=GLA_HEADS, chunk=HA_CHUNK)
    return pl.pallas_call(
        kern,
        grid=(n_hb, t // GLA_ROWS),
        in_specs=[section(0), section(1), section(2), section(3), vec_spec, vec_spec],
        out_specs=[pl.BlockSpec((GLA_ROWS, w), lambda h, i: (i, h)),
                   pl.BlockSpec((GLA_HEADS, HA_DK, HA_DV), lambda h, i: (h, 0, 0))],
        out_shape=[jax.ShapeDtypeStruct((t, hk), jnp.bfloat16),
                   jax.ShapeDtypeStruct((HA_HEADS, HA_DK, HA_DV), jnp.float32)],
        scratch_shapes=[pltpu.VMEM((GLA_HEADS, HA_DV, HA_DK), jnp.float32)],
        compiler_params=pltpu.CompilerParams(
            dimension_semantics=("arbitrary", "arbitrary"), vmem_limit_bytes=VMEM_LIMIT),
        name="gla_prompt",
    )(proj, proj, proj, proj, lb.reshape(1, hk), out_gain.reshape(1, hk))


def _page_specs(n_pages):
    return [pl.BlockSpec((PAGE * NB_KV, NB_DH), functools.partial(lambda b, pt, p: (pt[b, p], 0), p=p))
            for p in range(n_pages)]


def _cmp_pages_kernel(pt_ref, *refs, n_pages):
    kp = refs[:n_pages]
    vp = refs[n_pages:2 * n_pages]
    (w1k_ref, w1v_ref, pek_ref, pev_ref, w2k_ref, w2v_ref, gain_ref, ko_ref, vo_ref, x_ref, st_ref) = refs[2 * n_pages:]
    n_sub = n_pages * SUBS_PER_PAGE
    f32 = jnp.float32
    row = lax.broadcasted_iota(jnp.int32, (NB_KV * n_sub, 1), 0)
    last = lax.rem(row, n_sub) == n_sub - 1
    for pages, w1_ref, pe_ref, w2_ref, o_ref, norm in ((kp, w1k_ref, pek_ref, w2k_ref, ko_ref, True),
                                                       (vp, w1v_ref, pev_ref, w2v_ref, vo_ref, False)):
        for p in range(n_pages):
            for n in range(SUBS_PER_PAGE):
                st_ref[pl.ds((p * SUBS_PER_PAGE + n) * SUB_PITCH, SUB_ROWS), :] = pages[p][pl.ds(n * SUB_ROWS, SUB_ROWS), :]
        for g in range(NB_KV):
            for s in range(CMP_STRIDE):
                x_ref[g * n_sub:(g + 1) * n_sub, s * NB_DH:(s + 1) * NB_DH] = _bf(
                    st_ref[pl.ds(s * NB_KV + g, n_sub, stride=SUB_PITCH), :])
        w1 = w1_ref[...]
        a01 = jnp.dot(x_ref[...], w1, preferred_element_type=f32)
        c = jnp.dot(_bf(pe_ref[...]), w1, preferred_element_type=f32)
        a0 = a01[:, :NB_DH] + c[0:1, :NB_DH]
        a1 = a01[:, NB_DH:] + c[1:2, NB_DH:]
        pre = a0 + pltpu.roll(a1, NB_KV * n_sub - 1, 0)
        y = jnp.dot(_bf(jax.nn.gelu(pre)), w2_ref[...], preferred_element_type=f32)
        if norm:
            y = y * lax.rsqrt(jnp.mean(y * y, axis=-1, keepdims=True) + EPS) * gain_ref[...]
        o_ref[...] = jnp.where(last, 0.0, y).astype(o_ref.dtype)


def _cmp_pages(page_table, cache_k, cache_v, pe_k, pe_v, w1_k, w1_v, w2_k, w2_v, gain):
    nb, n_pages = page_table.shape
    n_phys = cache_k.shape[0]
    assert cache_k.shape[1:] == (PAGE, NB_KV, NB_DH) and CMP_LEN == 2 * CMP_STRIDE
    n_sub = n_pages * SUBS_PER_PAGE
    kd = CMP_STRIDE * NB_DH

    def w1cat(w1):
        return _bf(jnp.concatenate([w1[:CMP_STRIDE].reshape(kd, NB_DH), w1[CMP_STRIDE:].reshape(kd, NB_DH)], axis=1))

    def pecat(pe):
        z = jnp.zeros((8, kd), jnp.float32)
        return z.at[0].set(pe[:CMP_STRIDE].reshape(kd)).at[1].set(pe[CMP_STRIDE:].reshape(kd))

    def full(*shape):
        return pl.BlockSpec(shape, lambda b, pt: (0,) * len(shape))

    out_spec = pl.BlockSpec((None, NB_KV * n_sub, NB_DH), lambda b, pt: (b, 0, 0))
    ck = cache_k.reshape(n_phys * PAGE * NB_KV, NB_DH)
    cv = cache_v.reshape(n_phys * PAGE * NB_KV, NB_DH)
    grid_spec = pltpu.PrefetchScalarGridSpec(
        num_scalar_prefetch=1,
        grid=(nb,),
        in_specs=_page_specs(n_pages) * 2 + [full(kd, 2 * NB_DH), full(kd, 2 * NB_DH), full(8, kd), full(8, kd),
                                            full(NB_DH, NB_DH), full(NB_DH, NB_DH), full(1, NB_DH)],
        out_specs=[out_spec, out_spec],
        scratch_shapes=[pltpu.VMEM((NB_KV * n_sub, kd), jnp.bfloat16),
                        pltpu.VMEM((n_sub * SUB_PITCH, NB_DH), jnp.float32)],
    )
    return pl.pallas_call(
        functools.partial(_cmp_pages_kernel, n_pages=n_pages),
        grid_spec=grid_spec,
        out_shape=[jax.ShapeDtypeStruct((nb, NB_KV * n_sub, NB_DH), jnp.bfloat16)] * 2,
        compiler_params=pltpu.CompilerParams(dimension_semantics=("arbitrary",), vmem_limit_bytes=VMEM_LIMIT),
        name="cmp_pages",
    )(page_table, *([ck] * n_pages), *([cv] * n_pages), w1cat(w1_k), w1cat(w1_v), pecat(pe_k), pecat(pe_v),
      _bf(w2_k), _bf(w2_v), gain.reshape(1, NB_DH))


def _softmax_with_tail(s, mask, s_tail, ok_tail):
    m = jnp.max(s, axis=-1, keepdims=True)
    for sc in s_tail:
        m = jnp.maximum(m, sc)
    e = jnp.where(mask, jnp.exp(s - m), 0.0)
    e_tail = [jnp.where(ok, jnp.exp(sc - m), 0.0) for ok, sc in zip(ok_tail, s_tail)]
    den = jnp.sum(e, axis=-1, keepdims=True)
    for et in e_tail:
        den = den + et
    inv = 1.0 / jnp.maximum(den, 1e-30)
    return e * inv, [et * inv for et in e_tail]


def _nsa_sample_kernel(pt_ref, *refs, n_pages, n_new, win_keep):
    ksp = refs[:n_pages]
    vsp = refs[n_pages:2 * n_pages]
    (q_ref, qr_ref, kcb_ref, vcb_ref, ksn_ref, vsn_ref, kwc_ref, vwc_ref, kwn_ref, vwn_ref, ov_ref,
     oc_ref, os_ref, ow_ref, kwo_ref, vwo_ref) = refs[2 * n_pages:]
    f32 = jnp.float32
    past = n_pages * PAGE
    n_sub = past // CMP_STRIDE
    n_cmp = n_sub - CMP_LEN // CMP_STRIDE + 1
    n_slc = -(-(past + n_new) // SLC_LEN)
    new_blk = past // SLC_LEN
    assert (past % SLC_LEN) + n_new <= SLC_LEN and n_sub == LANES and n_slc <= LANES
    rows = HEADS_PER_GROUP * n_new
    all_rows = NB_KV * rows
    sel_rows = NB_KV * n_new
    scale = NB_DH ** -0.5
    groups = range(NB_KV)

    def grp_rows(g, n_tok):
        return pl.ds(g, n_tok, stride=NB_KV)

    def stack(parts):
        return jnp.concatenate(parts, axis=0)

    t_row = lax.rem(lax.broadcasted_iota(jnp.int32, (all_rows, 1), 0), n_new)
    pos = past + t_row
    pos_t = past + lax.rem(lax.broadcasted_iota(jnp.int32, (sel_rows, 1), 0), n_new)
    lane = lax.broadcasted_iota(jnp.int32, (1, LANES), 1)
    q = [q_ref[g] for g in groups]
    qr = [qr_ref[g] for g in groups]
    qrf = stack(qr).astype(f32)

    def tail_scores(new_ref):
        kb = _bf(new_ref[...]).astype(f32)
        out = []
        for i in range(n_new):
            ki = stack([jnp.broadcast_to(kb[i * NB_KV + g:i * NB_KV + g + 1, :], (rows, NB_DH)) for g in groups])
            out.append(jnp.sum(qrf * ki, axis=-1, keepdims=True) * scale)
        return out

    def tail_values(p_tail, new_ref, g):
        vb = _bf(new_ref[...]).astype(f32)
        acc = _bf(p_tail[0][g * rows:(g + 1) * rows]).astype(f32) * vb[g:g + 1, :]
        for i in range(1, n_new):
            acc = acc + _bf(p_tail[i][g * rows:(g + 1) * rows]).astype(f32) * vb[i * NB_KV + g:i * NB_KV + g + 1, :]
        return acc

    mask_c = (lane * CMP_STRIDE + (CMP_LEN - 1) <= pos) & (lane < n_cmp)
    s = stack([lax.dot_general(q[g], kcb_ref[g * n_sub:(g + 1) * n_sub, :], _NT, preferred_element_type=f32)
               for g in groups]) * scale
    s = jnp.where(mask_c, s, NEG)
    m = jnp.max(s, axis=-1, keepdims=True)
    e = jnp.where(mask_c, jnp.exp(s - m), 0.0)
    p = e / jnp.maximum(jnp.sum(e, axis=-1, keepdims=True), 1e-30)
    pb = _bf(p)
    for g in groups:
        oc_ref[g] = jnp.dot(pb[g * rows:(g + 1) * rows], vcb_ref[g * n_sub:(g + 1) * n_sub, :],
                            preferred_element_type=f32)
    psum = []
    for g in groups:
        acc = p[g * rows:g * rows + n_new]
        for j in range(1, HEADS_PER_GROUP):
            acc = acc + p[g * rows + j * n_new:g * rows + (j + 1) * n_new]
        psum.append(acc)
    psum = stack(psum)
    hi = _bf(psum)
    lo = _bf(psum - hi.astype(f32))
    ov = ov_ref[...]
    imp = jnp.dot(hi, ov, preferred_element_type=f32) + jnp.dot(lo, ov, preferred_element_type=f32)
    cur = lax.shift_right_logical(pos_t, 6)
    forced = (lane == 0) | (lane == cur) | (lane == cur - 1)
    future = lane * SLC_LEN > pos_t
    score = jnp.where(future, NEG, jnp.where(forced, BIG, imp))
    score = jnp.where(lane >= n_slc, REMOVED, score)
    lane_f = lax.broadcasted_iota(jnp.int32, (sel_rows, LANES), 1).astype(f32)
    sel = jnp.zeros((sel_rows, LANES), f32)
    for _ in range(min(N_SELECT, n_slc)):
        mx = jnp.max(score, axis=-1, keepdims=True)
        idx = jnp.min(jnp.where(score == mx, lane_f, float(LANES)), axis=-1, keepdims=True)
        hit = lane_f == idx
        sel = jnp.where(hit, 1.0, sel)
        score = jnp.where(hit, REMOVED, score)
    sel_r = stack([sel[g * n_new:(g + 1) * n_new] for g in groups for _ in range(HEADS_PER_GROUP)])
    expand = jnp.where(lax.broadcasted_iota(jnp.int32, (LANES, past), 0)
                       == lax.shift_right_logical(lax.broadcasted_iota(jnp.int32, (LANES, past), 1), 6),
                       1.0, 0.0).astype(jnp.bfloat16)
    picked = jnp.dot(_bf(sel_r), expand, preferred_element_type=f32) > 0.5
    s_all = stack([jnp.concatenate([lax.dot_general(qr[g], _bf(ksp[pg][grp_rows(g, PAGE), :]), _NT,
                                                    preferred_element_type=f32)
                                    for pg in range(n_pages)], axis=1) for g in groups]) * scale
    s_all = jnp.where(picked, s_all, NEG)
    new_sel = sel_r[:, new_blk:new_blk + 1] > 0.5
    ok_new = [new_sel & (t_row >= i) for i in range(n_new)]
    s_new = [jnp.where(ok, sc, NEG) for ok, sc in zip(ok_new, tail_scores(ksn_ref))]
    p_all, p_new = _softmax_with_tail(s_all, picked, s_new, ok_new)
    p_all = _bf(p_all)
    for g in groups:
        acc = tail_values(p_new, vsn_ref, g)
        for pg in range(n_pages):
            acc = acc + jnp.dot(p_all[g * rows:(g + 1) * rows, pg * PAGE:(pg + 1) * PAGE],
                                _bf(vsp[pg][grp_rows(g, PAGE), :]), preferred_element_type=f32)
        os_ref[g] = acc
    mask_w = lax.broadcasted_iota(jnp.int32, (1, win_keep), 1) >= t_row
    s_w = stack([lax.dot_general(qr[g], _bf(kwc_ref[grp_rows(g, win_keep), :]), _NT, preferred_element_type=f32)
                 for g in groups]) * scale
    s_w = jnp.where(mask_w, s_w, NEG)
    ok_wn = [t_row >= i for i in range(n_new)]
    s_wn = [jnp.where(ok, sc, NEG) for ok, sc in zip(ok_wn, tail_scores(kwn_ref))]
    p_w, p_wn = _softmax_with_tail(s_w, mask_w, s_wn, ok_wn)
    p_w = _bf(p_w)
    for g in groups:
        ow_ref[g] = (tail_values(p_wn, vwn_ref, g)
                     + jnp.dot(p_w[g * rows:(g + 1) * rows], _bf(vwc_ref[grp_rows(g, win_keep), :]),
                               preferred_element_type=f32))
    new_rows = n_new * NB_KV
    kept_rows = win_keep * NB_KV - new_rows
    for src_ref, add_ref, dst_ref in ((kwc_ref, kwn_ref, kwo_ref), (vwc_ref, vwn_ref, vwo_ref)):
        dst_ref[0:kept_rows, :] = src_ref[new_rows:new_rows + kept_rows, :]
        dst_ref[kept_rows:kept_rows + new_rows, :] = add_ref[...]


def _nsa_sample(page_table, q, qr, kcb, vcb, cache_sk, cache_sv, ks_new, vs_new, cache_wk, cache_wv, kw_new, vw_new):
    nb, n_pages = page_table.shape
    t = q.shape[1]
    n_phys = cache_sk.shape[0]
    keep = cache_wk.shape[1]
    assert keep == WINDOW and (t * NB_KV) % 8 == 0 and t < keep
    rows = HEADS_PER_GROUP * t
    n_sub = n_pages * SUBS_PER_PAGE

    def q_rows(a):
        a = a.reshape(nb, t, NB_KV, HEADS_PER_GROUP, NB_DH)
        return _bf(jnp.transpose(a, (0, 2, 3, 1, 4)).reshape(nb, NB_KV, rows, NB_DH))

    def q_unrows(a):
        a = a.reshape(nb, NB_KV, HEADS_PER_GROUP, t, NB_DH)
        return jnp.transpose(a, (0, 3, 1, 2, 4)).reshape(nb, t, NB_HEADS, NB_DH)

    def per_b(*shape):
        return pl.BlockSpec((None,) + shape, lambda b, pt: (b,) + (0,) * len(shape))

    def rows_view(a):
        return a.reshape(a.shape[0], a.shape[1] * NB_KV, NB_DH)

    n_cmp = n_sub - CMP_LEN // CMP_STRIDE + 1
    q_spec = per_b(NB_KV, rows, NB_DH)
    win_spec = per_b(keep * NB_KV, NB_DH)
    new_spec = per_b(t * NB_KV, NB_DH)
    cmp_spec = per_b(NB_KV * n_sub, NB_DH)
    in_specs = (_page_specs(n_pages) * 2
                + [q_spec, q_spec, cmp_spec, cmp_spec, new_spec, new_spec, win_spec, win_spec, new_spec, new_spec,
                   pl.BlockSpec((LANES, LANES), lambda b, pt: (0, 0))])
    grid_spec = pltpu.PrefetchScalarGridSpec(
        num_scalar_prefetch=1, grid=(nb,), in_specs=in_specs, out_specs=[q_spec] * 3 + [win_spec] * 2)
    sk = cache_sk.reshape(n_phys * PAGE * NB_KV, NB_DH)
    sv = cache_sv.reshape(n_phys * PAGE * NB_KV, NB_DH)
    o_c, o_s, o_w, wk_next, wv_next = pl.pallas_call(
        functools.partial(_nsa_sample_kernel, n_pages=n_pages, n_new=t, win_keep=keep),
        grid_spec=grid_spec,
        out_shape=([jax.ShapeDtypeStruct((nb, NB_KV, rows, NB_DH), jnp.float32)] * 3
                   + [jax.ShapeDtypeStruct((nb, keep * NB_KV, NB_DH), jnp.float32)] * 2),
        compiler_params=pltpu.CompilerParams(dimension_semantics=("arbitrary",), vmem_limit_bytes=VMEM_LIMIT),
        name="nsa_sample",
    )(page_table, *([sk] * n_pages), *([sv] * n_pages), q_rows(q), q_rows(qr), kcb, vcb,
      rows_view(ks_new), rows_view(vs_new), rows_view(cache_wk), rows_view(cache_wv), rows_view(kw_new),
      rows_view(vw_new), _overlap_matrix(LANES, n_cmp, LANES))
    return (q_unrows(o_c), q_unrows(o_s), q_unrows(o_w),
            wk_next.reshape(cache_wk.shape), wv_next.reshape(cache_wv.shape))


def _mm_kernel(x_ref, w_ref, o_ref, wb_ref):
    @pl.when(pl.program_id(1) == 0)
    def _():
        wb_ref[...] = w_ref[...].astype(jnp.bfloat16)

    o_ref[...] = jnp.dot(x_ref[...], wb_ref[...], preferred_element_type=jnp.float32)


def _mm_tiles(m, k, n):
    if k <= 2048:
        tm = 1024
        tn = 1024 if (n % 1024 == 0 or n >= 4096) else 512
    else:
        tm, tn = 512, 512
    return min(tm, m), min(tn, n)


def _mm(x, w):
    lead = x.shape[:-1]
    k = x.shape[-1]
    n = w.shape[1]
    x2 = _bf(x.reshape(-1, k))
    m = x2.shape[0]
    tm, tn = _mm_tiles(m, k, n)
    assert m % tm == 0
    out = pl.pallas_call(
        _mm_kernel,
        grid=(pl.cdiv(n, tn), m // tm),
        in_specs=[pl.BlockSpec((tm, k), lambda j, i: (i, 0)),
                  pl.BlockSpec((k, tn), lambda j, i: (0, j))],
        out_specs=pl.BlockSpec((tm, tn), lambda j, i: (i, j)),
        out_shape=jax.ShapeDtypeStruct((m, n), jnp.float32),
        scratch_shapes=[pltpu.VMEM((k, tn), jnp.bfloat16)],
        compiler_params=pltpu.CompilerParams(
            dimension_semantics=("arbitrary", "arbitrary"), vmem_limit_bytes=VMEM_LIMIT),
        name="mm",
    )(x2, w)
    return out.reshape(*lead, n)


def _ffn_up_kernel(x_ref, wu_ref, wg_ref, cw_ref, cb_ref, prev_ref, act_ref, state_ref, wub_ref, wgb_ref, halo_ref):
    m = pl.program_id(1)
    tm = x_ref.shape[0]
    f32 = jnp.float32

    @pl.when(m == 0)
    def _():
        wub_ref[...] = _bf(wu_ref[...])
        wgb_ref[...] = _bf(wg_ref[...])
        halo_ref[0:CONV_W - 1, :] = prev_ref[...].astype(f32)

    x = x_ref[...]
    u = jnp.dot(x, wub_ref[...], preferred_element_type=f32)
    g = jnp.dot(x, wgb_ref[...], preferred_element_type=f32)
    row = lax.broadcasted_iota(jnp.int32, (tm, 1), 0)
    before1 = halo_ref[1:2, :]
    before2 = halo_ref[0:1, :]
    g1 = jnp.where(row == 0, before1, pltpu.roll(g, 1, 0))
    g2 = pltpu.roll(g, 2, 0)
    g2 = jnp.where(row == 0, before2, jnp.where(row == 1, before1, g2))
    conv = cb_ref[...] + cw_ref[0:1, :] * g2 + cw_ref[1:2, :] * g1 + cw_ref[2:3, :] * g
    act_ref[...] = (jax.nn.silu(conv) * u).astype(act_ref.dtype)
    tail = g[tm - (CONV_W - 1):tm, :]
    halo_ref[0:CONV_W - 1, :] = tail
    state_ref[...] = tail


def _ffn_up_prompt(x, w_up, conv_w, conv_b, prev):
    m, k = x.shape
    assert CONV_W == 3 and m % FFN_TM == 0 and w_up.shape == (k, 2 * D_FF)
    w_u, w_g = w_up[:, :D_FF], w_up[:, D_FF:]

    def col(r):
        return pl.BlockSpec((r, FFN_TN), lambda j, i: (0, j))

    return pl.pallas_call(
        _ffn_up_kernel,
        grid=(pl.cdiv(D_FF, FFN_TN), m // FFN_TM),
        in_specs=[pl.BlockSpec((FFN_TM, k), lambda j, i: (i, 0)), col(k), col(k), col(CONV_W), col(1), col(CONV_W - 1)],
        out_specs=[pl.BlockSpec((FFN_TM, FFN_TN), lambda j, i: (i, j)), col(CONV_W - 1)],
        out_shape=[jax.ShapeDtypeStruct((m, D_FF), jnp.bfloat16),
                   jax.ShapeDtypeStruct((CONV_W - 1, D_FF), jnp.float32)],
        scratch_shapes=[pltpu.VMEM((k, FFN_TN), jnp.bfloat16), pltpu.VMEM((k, FFN_TN), jnp.bfloat16),
                        pltpu.VMEM((8, FFN_TN), jnp.float32)],
        compiler_params=pltpu.CompilerParams(
            dimension_semantics=("arbitrary", "arbitrary"), vmem_limit_bytes=VMEM_LIMIT),
        name="ffn_up_prompt",
    )(_bf(x), w_u, w_g, conv_w, conv_b.reshape(1, D_FF), prev)


def _rmsnorm(x, g):
    xf = x.astype(jnp.float32)
    y = xf * lax.rsqrt(jnp.mean(xf * xf, axis=-1, keepdims=True) + EPS)
    return (y * g.astype(jnp.float32)).astype(x.dtype)


def _rope(x, pos):
    half = x.shape[-1] // 2
    inv = ROPE_THETA ** (-jnp.arange(half, dtype=jnp.float32) / half)
    ang = pos.astype(jnp.float32)[:, None] * inv[None, :]
    cos, sin = jnp.cos(ang)[:, None, :], jnp.sin(ang)[:, None, :]
    xf = x.astype(jnp.float32)
    x1, x2 = xf[..., :half], xf[..., half:]
    return jnp.concatenate([x1 * cos - x2 * sin, x2 * cos + x1 * sin], axis=-1).astype(x.dtype)


def _gla_chunked(q, k, v, logf, s0):
    b, t, h, dk = q.shape
    dv = v.shape[-1]
    c = math.gcd(t, HA_CHUNK)
    n = t // c

    def to_chunks(a):
        return jnp.transpose(a.reshape(b, n, c, h, a.shape[-1]), (1, 0, 3, 2, 4))

    qc, kc, vc, gc = to_chunks(q), to_chunks(k), to_chunks(v), to_chunks(logf)
    bcum = jnp.cumsum(gc, axis=3)
    blast = bcum[:, :, :, -1, :]
    q_in = qc * jnp.exp(bcum)
    k_in = kc * jnp.exp(-bcum)
    k_dec = kc * jnp.exp(blast[:, :, :, None, :] - bcum)
    causal = jnp.tril(jnp.ones((c, c), dtype=bool))
    att = jnp.where(causal, jnp.einsum('nbhik,nbhjk->nbhij', q_in, k_in), 0.0)
    o_intra = jnp.einsum('nbhij,nbhjv->nbhiv', att, vc)

    def step(s, inp):
        qi, kd, vi, dl = inp
        o_inter = jnp.einsum('bhik,bhkv->bhiv', qi, s)
        s = jnp.exp(dl)[..., None] * s + jnp.einsum('bhik,bhiv->bhkv', kd, vi)
        return s, o_inter

    s_final, o_inter = lax.scan(step, s0.astype(jnp.float32), (q_in, k_dec, vc, blast))
    o = o_intra + o_inter
    return jnp.transpose(o, (1, 0, 3, 2, 4)).reshape(b, t, h, dv), s_final


def _hgrn2_layer(xn, s0, lb, w_in, out_gain, w_out):
    b, t, _ = xn.shape
    hk = HA_HEADS * HA_DK
    hv = HA_HEADS * HA_DV
    proj = _mm(xn, w_in)
    if s0 is None:
        assert b == 1
        o, s_new = _gla_prompt(proj[0], lb, out_gain)
        return _mm(o[None], w_out), s_new[None]
    q = jax.nn.silu(proj[..., :hk])
    f = lb + (1.0 - lb) * jax.nn.sigmoid(proj[..., hk:2 * hk])
    i_in = proj[..., 2 * hk:2 * hk + hv]
    gate = proj[..., 2 * hk + hv:]
    shp_k = (b, t, HA_HEADS, HA_DK)
    shp_v = (b, t, HA_HEADS, HA_DV)
    o, s_new = _gla_chunked(q.reshape(shp_k), (1.0 - f).reshape(shp_k), i_in.reshape(shp_v),
                            jnp.log(f).reshape(shp_k), s0)
    o = _rmsnorm(o, out_gain.reshape(HA_HEADS, HA_DV)) * jax.nn.silu(gate).reshape(shp_v)
    return _mm(o.reshape(b, t, hv), w_out), s_new


def _conv_ffn(xn, prev, w_up, conv_w, conv_b, w_down):
    b, t, _ = xn.shape
    if b == 1 and t % FFN_TM == 0:
        act, state = _ffn_up_prompt(xn[0], w_up, conv_w, conv_b, prev[0])
        return _mm(act[None], w_down), state[None].astype(xn.dtype)
    up = _mm(xn, w_up)
    u, gt = up[..., :D_FF], up[..., D_FF:]
    ext = jnp.concatenate([prev.astype(gt.dtype), gt], axis=1)
    conv = conv_b
    for j in range(CONV_W):
        conv = conv + conv_w[j] * ext[:, j:j + t]
    return _mm(jax.nn.silu(conv) * u, w_down), ext[:, t:]


def _ple(h, p, gain, w_p, w_g):
    gate = jax.nn.sigmoid(_mm(_rmsnorm(h, gain), w_g))
    return h + _mm(p, w_p) * gate


def _compress(rows, pe, w1, w2):
    b, l, g, d = rows.shape
    n_sub = l // CMP_STRIDE
    r = CMP_LEN // CMP_STRIDE
    n_cmp = n_sub - r + 1
    sub = rows[:, :n_sub * CMP_STRIDE].reshape(b, n_sub, CMP_STRIDE, g, d)
    pre = None
    for j in range(r):
        sl = slice(j * CMP_STRIDE, (j + 1) * CMP_STRIDE)
        part = jnp.einsum('bnsgd,sde->bnge', sub + pe[sl][None, None, :, None, :], w1[sl])[:, j:j + n_cmp]
        pre = part if pre is None else pre + part
    return jnp.einsum('bnge,ef->bngf', jax.nn.gelu(pre), w2)


def _nsa_layer(xn, pos, shared, w_in, q_gain, w_out, banded):
    b, t, _ = xn.shape
    hd = NB_HEADS * NB_DH
    proj = _mm(xn, w_in)
    q = _rmsnorm(proj[..., :hd].reshape(b, t, NB_HEADS, NB_DH), q_gain)
    gates = jax.nn.sigmoid(proj[..., hd:].astype(jnp.float32)).reshape(b, t, 3, NB_HEADS)
    qr = _rope(q, pos)
    win_next = None
    if banded:
        assert b == 1
        kcb, vcb, kpad, vpad, win_k, win_v = shared
        o_c2, sel = _cmp_topk_prompt(q.reshape(t, hd), kcb[0], vcb[0])
        o_s2, o_w2 = _slc_win_prompt(qr.reshape(t, hd), kpad[0], vpad[0], win_k[0], win_v[0], sel)
        o_c = o_c2.reshape(b, t, NB_HEADS, NB_DH)
        o_s = o_s2.reshape(b, t, NB_HEADS, NB_DH)
        o_w = o_w2.reshape(b, t, NB_HEADS, NB_DH)
    else:
        o_c, o_s, o_w, wk_next, wv_next = _nsa_sample(shared[0], q, qr, *shared[1:])
        win_next = (wk_next, wv_next)
    o = (gates[:, :, 0, :, None] * o_c + gates[:, :, 1, :, None] * o_s + gates[:, :, 2, :, None] * o_w)
    return _mm(o.reshape(b, t, hd), w_out), win_next


def _shared_kv(h, pos, past, wt):
    b, t, _ = h.shape
    kv = _mm(_rmsnorm(h, wt['kv_norm']), wt['kv_w']).reshape(b, t, 6, NB_KV, NB_DH)
    k_cmp, v_cmp = kv[:, :, 0], kv[:, :, 1]
    k_slc = _rope(_rmsnorm(kv[:, :, 2], wt['k_norm_slc']), pos)
    v_slc = kv[:, :, 3]
    k_win = _rope(_rmsnorm(kv[:, :, 4], wt['k_norm_win']), pos)
    v_win = kv[:, :, 5]
    if past is None:
        keep = min(WINDOW, t)
        kcb = _rmsnorm(_compress(k_cmp, wt['cmp_pe_k'], wt['cmp_w1_k'], wt['cmp_w2_k']), wt['k_norm_cmp'])
        vcb = _compress(v_cmp, wt['cmp_pe_v'], wt['cmp_w1_v'], wt['cmp_w2_v'])
        assert t % SLC_LEN == 0
        shared = (kcb, vcb, k_slc, v_slc, k_win, v_win)
        rows = (k_cmp, v_cmp, k_slc, v_slc, k_win[:, -keep:], v_win[:, -keep:])
    else:
        page_table, ckc, cvc, cks, cvs, cwk, cwv = past
        assert t < CMP_STRIDE and (page_table.shape[1] * PAGE) % CMP_STRIDE == 0
        kcb, vcb = _cmp_pages(page_table, ckc, cvc, wt['cmp_pe_k'], wt['cmp_pe_v'], wt['cmp_w1_k'], wt['cmp_w1_v'],
                              wt['cmp_w2_k'], wt['cmp_w2_v'], wt['k_norm_cmp'])
        shared = (page_table, kcb, vcb, cks, cvs, k_slc, v_slc, cwk, cwv, k_win, v_win)
        rows = (k_cmp, v_cmp, k_slc, v_slc, None, None)
    return shared, rows


def _trunk(x, p, start, s_hgrn, s_conv, past, wt):
    b, t, _ = x.shape
    pos = start + jnp.arange(t)
    banded = past is None
    lb = jnp.cumsum(jax.nn.softmax(wt['a_lb_logits'].astype(jnp.float32), axis=0), axis=0)
    h = x
    hgrn_out, conv_out = [], []
    shared, rows = None, None
    for i in range(DEPTH):
        hn = _rmsnorm(h, wt['norm_mix'][i])
        if i < N_A:
            y, s = _hgrn2_layer(hn, None if s_hgrn is None else s_hgrn[i], lb[i], wt['a_w_in'][i],
                                wt['a_out_norm'][i], wt['a_w_out'][i])
            hgrn_out.append(s)
        else:
            j = i - N_A
            y, win_next = _nsa_layer(hn, pos, shared, wt['b_w_in'][j], wt['b_q_norm'][j], wt['b_w_out'][j], banded)
            if win_next is not None:
                assert N_B == 1
                rows = rows[:4] + win_next
        h = h + y
        f, cs = _conv_ffn(_rmsnorm(h, wt['norm_ffn'][i]), s_conv[i], wt['ffn_w_up'][i], wt['ffn_conv_w'][i],
                          wt['ffn_conv_b'][i], wt['ffn_w_down'][i])
        conv_out.append(cs)
        h = h + f
        h = _ple(h, p[i], wt['norm_ple'][i], wt['ple_w_p'][i], wt['ple_w_g'][i])
        if i == N_A - 1:
            shared, rows = _shared_kv(h, pos, past, wt)
    return h, jnp.stack(hgrn_out), jnp.stack(conv_out), rows


def kernel(x_prompt, x_sample, state_hgrn, state_conv, cache_cmp_k, cache_cmp_v, cache_slc_k, cache_slc_v,
           cache_win_k, cache_win_v, page_table, p_prompt, p_sample, norm_mix, norm_ffn, norm_ple,
           a_w_in, a_lb_logits, a_out_norm, a_w_out, kv_norm, kv_w, k_norm_cmp, k_norm_slc, k_norm_win,
           cmp_pe_k, cmp_pe_v, cmp_w1_k, cmp_w2_k, cmp_w1_v, cmp_w2_v, b_w_in, b_q_norm, b_w_out,
           ffn_w_up, ffn_conv_w, ffn_conv_b, ffn_w_down, ple_w_p, ple_w_g):
    wt = {
        'norm_mix': norm_mix, 'norm_ffn': norm_ffn, 'norm_ple': norm_ple,
        'a_w_in': a_w_in, 'a_lb_logits': a_lb_logits, 'a_out_norm': a_out_norm, 'a_w_out': a_w_out,
        'kv_norm': kv_norm, 'kv_w': kv_w, 'k_norm_cmp': k_norm_cmp, 'k_norm_slc': k_norm_slc,
        'k_norm_win': k_norm_win, 'cmp_pe_k': cmp_pe_k, 'cmp_pe_v': cmp_pe_v, 'cmp_w1_k': cmp_w1_k,
        'cmp_w2_k': cmp_w2_k, 'cmp_w1_v': cmp_w1_v, 'cmp_w2_v': cmp_w2_v,
        'b_w_in': b_w_in, 'b_q_norm': b_q_norm, 'b_w_out': b_w_out,
        'ffn_w_up': ffn_w_up, 'ffn_conv_w': ffn_conv_w, 'ffn_conv_b': ffn_conv_b, 'ffn_w_down': ffn_w_down,
        'ple_w_p': ple_w_p, 'ple_w_g': ple_w_g,
    }
    bp, past_len = x_prompt.shape[0], page_table.shape[1] * cache_cmp_k.shape[1]
    s_conv0 = jnp.zeros((DEPTH, bp, CONV_W - 1, D_FF), x_prompt.dtype)
    y_prompt, hgrn_p, conv_p, rows_p = _trunk(x_prompt, p_prompt, 0, None, s_conv0, None, wt)
    past = (page_table, cache_cmp_k, cache_cmp_v, cache_slc_k, cache_slc_v, cache_win_k, cache_win_v)
    y_sample, hgrn_s, conv_s, rows_s = _trunk(x_sample, p_sample, past_len, state_hgrn, state_conv, past, wt)
    kc_p, vc_p, ks_p, vs_p, wk_p, wv_p = rows_p
    kc_s, vc_s, ks_s, vs_s, wk_s, wv_s = rows_s
    return (y_prompt, y_sample, hgrn_p, hgrn_s, conv_p, conv_s, kc_p, vc_p, ks_p, vs_p,
            kc_s, vc_s, ks_s, vs_s, wk_p, wv_p, wk_s, wv_s)
```

```python
import functools
import math

import jax
import jax.numpy as jnp
from jax import lax
from jax.experimental import pallas as pl
from jax.experimental.pallas import tpu as pltpu

D_MODEL = 2048
DEPTH = 2
N_A = DEPTH // 2
N_B = DEPTH - N_A
EPS = 1e-6
HA_DK = 128
HA_HEADS = D_MODEL // HA_DK
HA_DV = D_MODEL // HA_HEADS
HA_CHUNK = 32
NB_DH = 128
NB_HEADS = D_MODEL // NB_DH
NB_KV = 4
HEADS_PER_GROUP = NB_HEADS // NB_KV
CMP_LEN = 32
CMP_STRIDE = 16
SLC_LEN = 64
N_SELECT = 16
WINDOW = 512
Q_BLOCK = 128
ROPE_THETA = 10000.0
D_FF = ((8 * D_MODEL // 3 + 127) // 128) * 128
CONV_W = 3
NEG = -1e30
BIG = 1e30
LOG2_E = math.log2(math.e)
REMOVED = -3.0e38
LANES = 128
SLC_KV_TILE = 1024
WIN_SPAN = WINDOW + Q_BLOCK
GLA_ROWS = 256
GLA_HEADS = 4
PAGE = 128
SUBS_PER_PAGE = PAGE // CMP_STRIDE
SUB_ROWS = CMP_STRIDE * NB_KV
SUB_PITCH = SUB_ROWS + 8
PLE_TM = 1024
PLE_TN = 512
FFN_TM = 1024
FFN_TN = 512
SLC_GROUPS = 2
VMEM_LIMIT = 48 * 1024 * 1024
SLC_VMEM_LIMIT = 56 * 1024 * 1024
_NT = (((1,), (1,)), ((), ()))
_TN = (((0,), (0,)), ((), ()))


def _bf(x):
    return x.astype(jnp.bfloat16)


def _cmp_topk_kernel(q_ref, k_ref, v_ref, ov_ref, o_ref, sel_ref, *, q_blk, n_cmp_pad, n_slc, n_sel):
    i = pl.program_id(0)
    f32 = jnp.float32
    pos = i * q_blk + lax.broadcasted_iota(jnp.int32, (q_blk, 1), 0)
    kend = lax.broadcasted_iota(jnp.int32, (1, n_cmp_pad), 1) * CMP_STRIDE + (CMP_LEN - 1)
    mask_c = kend <= pos
    scale = NB_DH ** -0.5
    ov = ov_ref[...]
    imp = []
    for g in range(NB_KV):
        k = k_ref[g]
        v = v_ref[g]
        psum = jnp.zeros((q_blk, n_cmp_pad), f32)
        for j in range(HEADS_PER_GROUP):
            cs = slice((g * HEADS_PER_GROUP + j) * NB_DH, (g * HEADS_PER_GROUP + j + 1) * NB_DH)
            s = lax.dot_general(_bf(q_ref[:, cs]), k, _NT, preferred_element_type=f32) * scale
            s = jnp.where(mask_c, s, NEG)
            m = jnp.max(s, axis=-1, keepdims=True)
            e = jnp.where(mask_c, jnp.exp(s - m), 0.0)
            p = e / jnp.maximum(jnp.sum(e, axis=-1, keepdims=True), 1e-30)
            o_ref[:, cs] = jnp.dot(_bf(p), v, preferred_element_type=f32)
            psum = psum + p
        hi = _bf(psum)
        lo = _bf(psum - hi.astype(f32))
        imp.append(jnp.dot(hi, ov, preferred_element_type=f32) + jnp.dot(lo, ov, preferred_element_type=f32))
    imp = jnp.concatenate(imp, axis=0)
    pos_r = jnp.concatenate([pos] * NB_KV, axis=0)
    sidx = lax.broadcasted_iota(jnp.int32, (1, n_slc), 1)
    cur = lax.shift_right_logical(pos_r, 6)
    forced = (sidx == 0) | (sidx == cur) | (sidx == cur - 1)
    future = sidx * SLC_LEN > pos_r
    score = jnp.where(future, NEG, jnp.where(forced, BIG, imp))
    lane = lax.broadcasted_iota(jnp.int32, (NB_KV * q_blk, n_slc), 1).astype(f32)
    sel = jnp.zeros((NB_KV * q_blk, n_slc), f32)
    for _ in range(n_sel):
        m = jnp.max(score, axis=-1, keepdims=True)
        idx = jnp.min(jnp.where(score == m, lane, float(n_slc)), axis=-1, keepdims=True)
        hit = lane == idx
        sel = jnp.where(hit, 1.0, sel)
        score = jnp.where(hit, REMOVED, score)
    for g in range(NB_KV):
        sel_ref[:, g * n_slc:(g + 1) * n_slc] = sel[g * q_blk:(g + 1) * q_blk].astype(sel_ref.dtype)


def _overlap_matrix(n_cmp_rows, n_cmp, n_slc_cols):
    cstart = jnp.arange(n_cmp_rows) * CMP_STRIDE
    sstart = jnp.arange(n_slc_cols) * SLC_LEN
    return _bf((cstart[:, None] <= sstart[None, :] + (SLC_LEN - 1))
               & (cstart[:, None] + (CMP_LEN - 1) >= sstart[None, :])
               & (jnp.arange(n_cmp_rows)[:, None] < n_cmp))


def _cmp_topk_prompt(q2d, kcb, vcb):
    t = q2d.shape[0]
    n_cmp = kcb.shape[0]
    n_cmp_pad = -(-n_cmp // LANES) * LANES
    n_slc = t // SLC_LEN
    assert SLC_LEN == 64 and n_slc == LANES and t % Q_BLOCK == 0
    n_sel = min(N_SELECT, n_slc)
    padc = ((0, 0), (0, n_cmp_pad - n_cmp), (0, 0))
    k = _bf(jnp.pad(jnp.transpose(kcb, (1, 0, 2)), padc))
    v = _bf(jnp.pad(jnp.transpose(vcb, (1, 0, 2)), padc))
    hd = NB_HEADS * NB_DH
    kern = functools.partial(_cmp_topk_kernel, q_blk=Q_BLOCK, n_cmp_pad=n_cmp_pad, n_slc=n_slc, n_sel=n_sel)
    kv_spec = pl.BlockSpec((NB_KV, n_cmp_pad, NB_DH), lambda i: (0, 0, 0))
    return pl.pallas_call(
        kern,
        grid=(t // Q_BLOCK,),
        in_specs=[pl.BlockSpec((Q_BLOCK, hd), lambda i: (i, 0)), kv_spec, kv_spec,
                  pl.BlockSpec((n_cmp_pad, n_slc), lambda i: (0, 0))],
        out_specs=[pl.BlockSpec((Q_BLOCK, hd), lambda i: (i, 0)),
                   pl.BlockSpec((Q_BLOCK, NB_KV * n_slc), lambda i: (i, 0))],
        out_shape=[jax.ShapeDtypeStruct((t, hd), jnp.float32),
                   jax.ShapeDtypeStruct((t, NB_KV * n_slc), jnp.bfloat16)],
        compiler_params=pltpu.CompilerParams(dimension_semantics=("arbitrary",), vmem_limit_bytes=VMEM_LIMIT),
        name="cmp_topk_prompt",
    )(q2d, k, v, _overlap_matrix(n_cmp_pad, n_cmp, n_slc))


def _slc_win_kernel(q_ref, ks_ref, vs_ref, kw_ref, vw_ref, sel_ref, os_ref, ow_ref, m_ref, l_ref, acc_ref,
                    *, q_blk, n_slc, kv_tile, n_grp):
    i = pl.program_id(1)
    rows = HEADS_PER_GROUP * q_blk
    gd = HEADS_PER_GROUP * NB_DH
    blocks_per_tile = kv_tile // SLC_LEN
    reps = kv_tile // LANES
    scale = NB_DH ** -0.5
    pos = i * q_blk + lax.broadcasted_iota(jnp.int32, (q_blk, 1), 0)
    q4 = [jnp.concatenate([q_ref[:, gg * gd + j * NB_DH:gg * gd + (j + 1) * NB_DH] for j in range(HEADS_PER_GROUP)],
                          axis=0) for gg in range(n_grp)]
    sel = [sel_ref[:, gg * n_slc:(gg + 1) * n_slc] for gg in range(n_grp)]
    m_ref[...] = jnp.full(m_ref.shape, NEG, jnp.float32)
    l_ref[...] = jnp.zeros(l_ref.shape, jnp.float32)
    acc_ref[...] = jnp.zeros(acc_ref.shape, jnp.float32)
    blk_row = lax.broadcasted_iota(jnp.int32, (n_slc, kv_tile), 0)
    blk_col = lax.shift_right_logical(lax.broadcasted_iota(jnp.int32, (n_slc, kv_tile), 1), 6)
    key_off = lax.broadcasted_iota(jnp.int32, (1, kv_tile), 1)
    n_tiles = lax.div(i * q_blk + (q_blk - 1), kv_tile) + 1

    def body(j, carry):
        start = pl.multiple_of(j * kv_tile, kv_tile)
        expand = _bf(jnp.where(blk_row == blk_col + j * blocks_per_tile, 1.0, 0.0))
        causal = key_off + start <= pos
        for gg in range(n_grp):
            kt = ks_ref[gg, pl.ds(start, kv_tile), :]
            vt = vs_ref[gg, pl.ds(start, kv_tile), :]
            s = lax.dot_general(q4[gg], kt, _NT, preferred_element_type=jnp.float32) * (scale * LOG2_E)
            picked = jnp.dot(sel[gg], expand, preferred_element_type=jnp.float32) > 0.5
            mask = picked & causal
            s = jnp.where(mask[None], s.reshape(HEADS_PER_GROUP, q_blk, kv_tile), NEG).reshape(rows, kv_tile)
            m_prev = m_ref[gg]
            m_new = jnp.maximum(m_prev, jnp.max(s, axis=-1, keepdims=True))
            alpha = jnp.exp2(m_prev - m_new)
            p = jnp.exp2(s - jnp.concatenate([m_new] * reps, axis=1))
            l_ref[gg] = alpha * l_ref[gg] + jnp.sum(p, axis=-1, keepdims=True)
            acc_ref[gg] = alpha * acc_ref[gg] + jnp.dot(_bf(p), vt, preferred_element_type=jnp.float32)
            m_ref[gg] = m_new
        return carry

    lax.fori_loop(0, n_tiles, body, 0)

    wstart = pl.multiple_of(jnp.maximum(i - WINDOW // q_blk, 0) * q_blk, q_blk)
    diff = pos - (wstart + lax.broadcasted_iota(jnp.int32, (1, WIN_SPAN), 1))
    mask = ((diff >= 0) & (diff <= WINDOW))[None]
    n_ghost = jnp.maximum(WINDOW - pos, 0).astype(jnp.float32)[None]
    for gg in range(n_grp):
        out = acc_ref[gg] / jnp.maximum(l_ref[gg], 1e-30)
        for j in range(HEADS_PER_GROUP):
            os_ref[:, gg * gd + j * NB_DH:gg * gd + (j + 1) * NB_DH] = out[j * q_blk:(j + 1) * q_blk, :]
        kt = kw_ref[gg, pl.ds(wstart, WIN_SPAN), :]
        vt = vw_ref[gg, pl.ds(wstart, WIN_SPAN), :]
        s = lax.dot_general(q4[gg], kt, _NT, preferred_element_type=jnp.float32) * scale
        s = jnp.where(mask, s.reshape(HEADS_PER_GROUP, q_blk, WIN_SPAN), NEG)
        m = jnp.max(s, axis=-1, keepdims=True)
        m = jnp.where(n_ghost > 0, jnp.maximum(m, 0.0), m)
        e = jnp.where(mask, jnp.exp(s - m), 0.0)
        p = e / jnp.maximum(jnp.sum(e, axis=-1, keepdims=True) + n_ghost * jnp.exp(-m), 1e-30)
        ow = jnp.dot(_bf(p.reshape(rows, WIN_SPAN)), vt, preferred_element_type=jnp.float32)
        for j in range(HEADS_PER_GROUP):
            ow_ref[:, gg * gd + j * NB_DH:gg * gd + (j + 1) * NB_DH] = ow[j * q_blk:(j + 1) * q_blk, :]


def _slc_win_prompt(qr2d, kpad, vpad, kwin, vwin, sel):
    t = qr2d.shape[0]
    n_slc = t // SLC_LEN
    assert t % SLC_KV_TILE == 0 and SLC_KV_TILE % SLC_LEN == 0 and t >= WIN_SPAN and NB_KV % SLC_GROUPS == 0

    def group_major(a):
        return _bf(jnp.transpose(a, (1, 0, 2)))

    gd = SLC_GROUPS * HEADS_PER_GROUP * NB_DH
    rows = HEADS_PER_GROUP * Q_BLOCK
    kern = functools.partial(_slc_win_kernel, q_blk=Q_BLOCK, n_slc=n_slc, kv_tile=SLC_KV_TILE, n_grp=SLC_GROUPS)
    kv_spec = pl.BlockSpec((SLC_GROUPS, t, NB_DH), lambda g, i: (g, 0, 0))
    o_spec = pl.BlockSpec((Q_BLOCK, gd), lambda g, i: (i, g))
    return pl.pallas_call(
        kern,
        grid=(NB_KV // SLC_GROUPS, t // Q_BLOCK),
        in_specs=[o_spec, kv_spec, kv_spec, kv_spec, kv_spec,
                  pl.BlockSpec((Q_BLOCK, SLC_GROUPS * n_slc), lambda g, i: (i, g))],
        out_specs=[o_spec, o_spec],
        out_shape=[jax.ShapeDtypeStruct((t, NB_HEADS * NB_DH), jnp.float32)] * 2,
        scratch_shapes=[
            pltpu.VMEM((SLC_GROUPS, rows, LANES), jnp.float32),
            pltpu.VMEM((SLC_GROUPS, rows, LANES), jnp.float32),
            pltpu.VMEM((SLC_GROUPS, rows, NB_DH), jnp.float32),
        ],
        compiler_params=pltpu.CompilerParams(
            dimension_semantics=("arbitrary", "arbitrary"), vmem_limit_bytes=SLC_VMEM_LIMIT),
        name="slc_win_prompt",
    )(_bf(qr2d), group_major(kpad), group_major(vpad), group_major(kwin), group_major(vwin), sel)


def _split3(x):
    hi = x.astype(jnp.bfloat16)
    r = x - hi.astype(jnp.float32)
    mid = r.astype(jnp.bfloat16)
    lo = (r - mid.astype(jnp.float32)).astype(jnp.bfloat16)
    return hi, mid, lo


def _gla_kernel(xq_ref, xf_ref, xv_ref, xg_ref, lb_ref, gain_ref, o_ref, sfin_ref, st_ref, *, rows, heads, chunk):
    i = pl.program_id(1)

    @pl.when(i == 0)
    def _():
        st_ref[...] = jnp.zeros(st_ref.shape, jnp.float32)

    r_idx = lax.broadcasted_iota(jnp.int32, (rows, rows), 0)
    c_idx = lax.broadcasted_iota(jnp.int32, (rows, rows), 1)
    same = lax.shift_right_logical(r_idx, 5) == lax.shift_right_logical(c_idx, 5)
    causal = same & (c_idx <= r_idx)
    tri = jnp.where(causal, 1.0, 0.0).astype(jnp.bfloat16)
    blk = jnp.where(same, 1.0, 0.0).astype(jnp.bfloat16)
    f32 = jnp.float32
    for hh in range(heads):
        sl = slice(hh * HA_DK, (hh + 1) * HA_DK)
        lb = lb_ref[:, sl]
        qh = jax.nn.silu(xq_ref[:, sl])
        f = lb + (1.0 - lb) * jax.nn.sigmoid(xf_ref[:, sl])
        kh = 1.0 - f
        gh = jnp.log(f)
        parts = _split3(gh)
        bcum = sum(jnp.dot(tri, p, preferred_element_type=f32) for p in parts)
        tot = sum(jnp.dot(blk, p, preferred_element_type=f32) for p in parts)
        q_in = (qh * jnp.exp(bcum)).astype(jnp.bfloat16)
        k_in = (kh * jnp.exp(-bcum)).astype(jnp.bfloat16)
        k_dec = (kh * jnp.exp(tot - bcum)).astype(jnp.bfloat16)
        vb = _bf(xv_ref[:, sl])
        att = lax.dot_general(q_in, k_in, _NT, preferred_element_type=f32)
        att = jnp.where(causal, att, 0.0).astype(jnp.bfloat16)
        o_intra = jnp.dot(att, vb, preferred_element_type=f32)
        st = st_ref[hh]
        o_inter = []
        for c in range(rows // chunk):
            rs = slice(c * chunk, (c + 1) * chunk)
            o_inter.append(lax.dot_general(q_in[rs], st.astype(jnp.bfloat16), _NT, preferred_element_type=f32))
            upd = lax.dot_general(vb[rs], k_dec[rs], _TN, preferred_element_type=f32)
            st = jnp.exp(tot[c * chunk:c * chunk + 1, :]) * st + upd
        st_ref[hh] = st
        o = o_intra + jnp.concatenate(o_inter, axis=0)
        o = o * lax.rsqrt(jnp.mean(o * o, axis=-1, keepdims=True) + EPS) * gain_ref[:, sl]
        o_ref[:, sl] = (o * jax.nn.silu(xg_ref[:, sl])).astype(o_ref.dtype)

    @pl.when(i == pl.num_programs(1) - 1)
    def _():
        for hh in range(heads):
            sfin_ref[hh] = st_ref[hh].T


def _gla_prompt(proj, lb, out_gain):
    t = proj.shape[0]
    hk = HA_HEADS * HA_DK
    assert proj.shape[1] == 4 * hk and HA_DK == HA_DV
    assert t % GLA_ROWS == 0 and GLA_ROWS % HA_CHUNK == 0 and HA_CHUNK == 32 and HA_HEADS % GLA_HEADS == 0
    w = GLA_HEADS * HA_DK
    n_hb = HA_HEADS // GLA_HEADS

    def section(c):
        return pl.BlockSpec((GLA_ROWS, w), functools.partial(lambda h, i, c: (i, c * n_hb + h), c=c))

    vec_spec = pl.BlockSpec((1, w), lambda h, i: (0, h))
    kern = functools.partial(_gla_kernel, rows=GLA_ROWS, heads=GLA_HEADS, chunk=HA_CHUNK)
    return pl.pallas_call(
        kern,
        grid=(n_hb, t // GLA_ROWS),
        in_specs=[section(0), section(1), section(2), section(3), vec_spec, vec_spec],
        out_specs=[pl.BlockSpec((GLA_ROWS, w), lambda h, i: (i, h)),
                   pl.BlockSpec((GLA_HEADS, HA_DK, HA_DV), lambda h, i: (h, 0, 0))],
        out_shape=[jax.ShapeDtypeStruct((t, hk), jnp.bfloat16),
                   jax.ShapeDtypeStruct((HA_HEADS, HA_DK, HA_DV), jnp.float32)],
        scratch_shapes=[pltpu.VMEM((GLA_HEADS, HA_DV, HA_DK), jnp.float32)],
        compiler_params=pltpu.CompilerParams(
            dimension_semantics=("arbitrary", "arbitrary"), vmem_limit_bytes=VMEM_LIMIT),
        name="gla_prompt",
    )(proj, proj, proj, proj, lb.reshape(1, hk), out_gain.reshape(1, hk))


def _page_specs(n_pages):
    return [pl.BlockSpec((PAGE * NB_KV, NB_DH), functools.partial(lambda b, pt, p: (pt[b, p], 0), p=p))
            for p in range(n_pages)]


def _cmp_pages_kernel(pt_ref, *refs, n_pages):
    kp = refs[:n_pages]
    vp = refs[n_pages:2 * n_pages]
    (w1k_ref, w1v_ref, pek_ref, pev_ref, w2k_ref, w2v_ref, gain_ref, ko_ref, vo_ref, x_ref, st_ref) = refs[2 * n_pages:]
    n_sub = n_pages * SUBS_PER_PAGE
    f32 = jnp.float32
    row = lax.broadcasted_iota(jnp.int32, (NB_KV * n_sub, 1), 0)
    last = lax.rem(row, n_sub) == n_sub - 1
    for pages, w1_ref, pe_ref, w2_ref, o_ref, norm in ((kp, w1k_ref, pek_ref, w2k_ref, ko_ref, True),
                                                       (vp, w1v_ref, pev_ref, w2v_ref, vo_ref, False)):
        for p in range(n_pages):
            for n in range(SUBS_PER_PAGE):
                st_ref[pl.ds((p * SUBS_PER_PAGE + n) * SUB_PITCH, SUB_ROWS), :] = pages[p][pl.ds(n * SUB_ROWS, SUB_ROWS), :]
        for g in range(NB_KV):
            for s in range(CMP_STRIDE):
                x_ref[g * n_sub:(g + 1) * n_sub, s * NB_DH:(s + 1) * NB_DH] = _bf(
                    st_ref[pl.ds(s * NB_KV + g, n_sub, stride=SUB_PITCH), :])
        w1 = w1_ref[...]
        a01 = jnp.dot(x_ref[...], w1, preferred_element_type=f32)
        c = jnp.dot(_bf(pe_ref[...]), w1, preferred_element_type=f32)
        a0 = a01[:, :NB_DH] + c[0:1, :NB_DH]
        a1 = a01[:, NB_DH:] + c[1:2, NB_DH:]
        pre = a0 + pltpu.roll(a1, NB_KV * n_sub - 1, 0)
        y = jnp.dot(_bf(jax.nn.gelu(pre)), w2_ref[...], preferred_element_type=f32)
        if norm:
            y = y * lax.rsqrt(jnp.mean(y * y, axis=-1, keepdims=True) + EPS) * gain_ref[...]
        o_ref[...] = jnp.where(last, 0.0, y).astype(o_ref.dtype)


def _cmp_pages(page_table, cache_k, cache_v, pe_k, pe_v, w1_k, w1_v, w2_k, w2_v, gain):
    nb, n_pages = page_table.shape
    n_phys = cache_k.shape[0]
    assert cache_k.shape[1:] == (PAGE, NB_KV, NB_DH) and CMP_LEN == 2 * CMP_STRIDE
    n_sub = n_pages * SUBS_PER_PAGE
    kd = CMP_STRIDE * NB_DH

    def w1cat(w1):
        return _bf(jnp.concatenate([w1[:CMP_STRIDE].reshape(kd, NB_DH), w1[CMP_STRIDE:].reshape(kd, NB_DH)], axis=1))

    def pecat(pe):
        z = jnp.zeros((8, kd), jnp.float32)
        return z.at[0].set(pe[:CMP_STRIDE].reshape(kd)).at[1].set(pe[CMP_STRIDE:].reshape(kd))

    def full(*shape):
        return pl.BlockSpec(shape, lambda b, pt: (0,) * len(shape))

    out_spec = pl.BlockSpec((None, NB_KV * n_sub, NB_DH), lambda b, pt: (b, 0, 0))
    ck = cache_k.reshape(n_phys * PAGE * NB_KV, NB_DH)
    cv = cache_v.reshape(n_phys * PAGE * NB_KV, NB_DH)
    grid_spec = pltpu.PrefetchScalarGridSpec(
        num_scalar_prefetch=1,
        grid=(nb,),
        in_specs=_page_specs(n_pages) * 2 + [full(kd, 2 * NB_DH), full(kd, 2 * NB_DH), full(8, kd), full(8, kd),
                                            full(NB_DH, NB_DH), full(NB_DH, NB_DH), full(1, NB_DH)],
        out_specs=[out_spec, out_spec],
        scratch_shapes=[pltpu.VMEM((NB_KV * n_sub, kd), jnp.bfloat16),
                        pltpu.VMEM((n_sub * SUB_PITCH, NB_DH), jnp.float32)],
    )
    return pl.pallas_call(
        functools.partial(_cmp_pages_kernel, n_pages=n_pages),
        grid_spec=grid_spec,
        out_shape=[jax.ShapeDtypeStruct((nb, NB_KV * n_sub, NB_DH), jnp.bfloat16)] * 2,
        compiler_params=pltpu.CompilerParams(dimension_semantics=("arbitrary",), vmem_limit_bytes=VMEM_LIMIT),
        name="cmp_pages",
    )(page_table, *([ck] * n_pages), *([cv] * n_pages), w1cat(w1_k), w1cat(w1_v), pecat(pe_k), pecat(pe_v),
      _bf(w2_k), _bf(w2_v), gain.reshape(1, NB_DH))


def _softmax_with_tail(s, mask, s_tail, ok_tail):
    m = jnp.max(s, axis=-1, keepdims=True)
    for sc in s_tail:
        m = jnp.maximum(m, sc)
    e = jnp.where(mask, jnp.exp(s - m), 0.0)
    e_tail = [jnp.where(ok, jnp.exp(sc - m), 0.0) for ok, sc in zip(ok_tail, s_tail)]
    den = jnp.sum(e, axis=-1, keepdims=True)
    for et in e_tail:
        den = den + et
    inv = 1.0 / jnp.maximum(den, 1e-30)
    return e * inv, [et * inv for et in e_tail]


def _nsa_sample_kernel(pt_ref, *refs, n_pages, n_new, win_keep):
    ksp = refs[:n_pages]
    vsp = refs[n_pages:2 * n_pages]
    (q_ref, qr_ref, kcb_ref, vcb_ref, ksn_ref, vsn_ref, kwc_ref, vwc_ref, kwn_ref, vwn_ref, ov_ref,
     oc_ref, os_ref, ow_ref, kwo_ref, vwo_ref) = refs[2 * n_pages:]
    f32 = jnp.float32
    past = n_pages * PAGE
    n_sub = past // CMP_STRIDE
    n_cmp = n_sub - CMP_LEN // CMP_STRIDE + 1
    n_slc = -(-(past + n_new) // SLC_LEN)
    new_blk = past // SLC_LEN
    assert (past % SLC_LEN) + n_new <= SLC_LEN and n_sub == LANES and n_slc <= LANES
    rows = HEADS_PER_GROUP * n_new
    all_rows = NB_KV * rows
    sel_rows = NB_KV * n_new
    scale = NB_DH ** -0.5
    groups = range(NB_KV)

    def grp_rows(g, n_tok):
        return pl.ds(g, n_tok, stride=NB_KV)

    def stack(parts):
        return jnp.concatenate(parts, axis=0)

    t_row = lax.rem(lax.broadcasted_iota(jnp.int32, (all_rows, 1), 0), n_new)
    pos = past + t_row
    pos_t = past + lax.rem(lax.broadcasted_iota(jnp.int32, (sel_rows, 1), 0), n_new)
    lane = lax.broadcasted_iota(jnp.int32, (1, LANES), 1)
    q = [q_ref[g] for g in groups]
    qr = [qr_ref[g] for g in groups]
    qrf = stack(qr).astype(f32)

    def tail_scores(new_ref):
        kb = _bf(new_ref[...]).astype(f32)
        out = []
        for i in range(n_new):
            ki = stack([jnp.broadcast_to(kb[i * NB_KV + g:i * NB_KV + g + 1, :], (rows, NB_DH)) for g in groups])
            out.append(jnp.sum(qrf * ki, axis=-1, keepdims=True) * scale)
        return out

    def tail_values(p_tail, new_ref, g):
        vb = _bf(new_ref[...]).astype(f32)
        acc = _bf(p_tail[0][g * rows:(g + 1) * rows]).astype(f32) * vb[g:g + 1, :]
        for i in range(1, n_new):
            acc = acc + _bf(p_tail[i][g * rows:(g + 1) * rows]).astype(f32) * vb[i * NB_KV + g:i * NB_KV + g + 1, :]
        return acc

    mask_c = (lane * CMP_STRIDE + (CMP_LEN - 1) <= pos) & (lane < n_cmp)
    s = stack([lax.dot_general(q[g], kcb_ref[g * n_sub:(g + 1) * n_sub, :], _NT, preferred_element_type=f32)
               for g in groups]) * scale
    s = jnp.where(mask_c, s, NEG)
    m = jnp.max(s, axis=-1, keepdims=True)
    e = jnp.where(mask_c, jnp.exp(s - m), 0.0)
    p = e / jnp.maximum(jnp.sum(e, axis=-1, keepdims=True), 1e-30)
    pb = _bf(p)
    for g in groups:
        oc_ref[g] = jnp.dot(pb[g * rows:(g + 1) * rows], vcb_ref[g * n_sub:(g + 1) * n_sub, :],
                            preferred_element_type=f32)
    psum = []
    for g in groups:
        acc = p[g * rows:g * rows + n_new]
        for j in range(1, HEADS_PER_GROUP):
            acc = acc + p[g * rows + j * n_new:g * rows + (j + 1) * n_new]
        psum.append(acc)
    psum = stack(psum)
    hi = _bf(psum)
    lo = _bf(psum - hi.astype(f32))
    ov = ov_ref[...]
    imp = jnp.dot(hi, ov, preferred_element_type=f32) + jnp.dot(lo, ov, preferred_element_type=f32)
    cur = lax.shift_right_logical(pos_t, 6)
    forced = (lane == 0) | (lane == cur) | (lane == cur - 1)
    future = lane * SLC_LEN > pos_t
    score = jnp.where(future, NEG, jnp.where(forced, BIG, imp))
    score = jnp.where(lane >= n_slc, REMOVED, score)
    lane_f = lax.broadcasted_iota(jnp.int32, (sel_rows, LANES), 1).astype(f32)
    sel = jnp.zeros((sel_rows, LANES), f32)
    for _ in range(min(N_SELECT, n_slc)):
        mx = jnp.max(score, axis=-1, keepdims=True)
        idx = jnp.min(jnp.where(score == mx, lane_f, float(LANES)), axis=-1, keepdims=True)
        hit = lane_f == idx
        sel = jnp.where(hit, 1.0, sel)
        score = jnp.where(hit, REMOVED, score)
    sel_r = stack([sel[g * n_new:(g + 1) * n_new] for g in groups for _ in range(HEADS_PER_GROUP)])
    expand = jnp.where(lax.broadcasted_iota(jnp.int32, (LANES, past), 0)
                       == lax.shift_right_logical(lax.broadcasted_iota(jnp.int32, (LANES, past), 1), 6),
                       1.0, 0.0).astype(jnp.bfloat16)
    picked = jnp.dot(_bf(sel_r), expand, preferred_element_type=f32) > 0.5
    s_all = stack([jnp.concatenate([lax.dot_general(qr[g], _bf(ksp[pg][grp_rows(g, PAGE), :]), _NT,
                                                    preferred_element_type=f32)
                                    for pg in range(n_pages)], axis=1) for g in groups]) * scale
    s_all = jnp.where(picked, s_all, NEG)
    new_sel = sel_r[:, new_blk:new_blk + 1] > 0.5
    ok_new = [new_sel & (t_row >= i) for i in range(n_new)]
    s_new = [jnp.where(ok, sc, NEG) for ok, sc in zip(ok_new, tail_scores(ksn_ref))]
    p_all, p_new = _softmax_with_tail(s_all, picked, s_new, ok_new)
    p_all = _bf(p_all)
    for g in groups:
        acc = tail_values(p_new, vsn_ref, g)
        for pg in range(n_pages):
            acc = acc + jnp.dot(p_all[g * rows:(g + 1) * rows, pg * PAGE:(pg + 1) * PAGE],
                                _bf(vsp[pg][grp_rows(g, PAGE), :]), preferred_element_type=f32)
        os_ref[g] = acc
    mask_w = lax.broadcasted_iota(jnp.int32, (1, win_keep), 1) >= t_row
    s_w = stack([lax.dot_general(qr[g], _bf(kwc_ref[grp_rows(g, win_keep), :]), _NT, preferred_element_type=f32)
                 for g in groups]) * scale
    s_w = jnp.where(mask_w, s_w, NEG)
    ok_wn = [t_row >= i for i in range(n_new)]
    s_wn = [jnp.where(ok, sc, NEG) for ok, sc in zip(ok_wn, tail_scores(kwn_ref))]
    p_w, p_wn = _softmax_with_tail(s_w, mask_w, s_wn, ok_wn)
    p_w = _bf(p_w)
    for g in groups:
        ow_ref[g] = (tail_values(p_wn, vwn_ref, g)
                     + jnp.dot(p_w[g * rows:(g + 1) * rows], _bf(vwc_ref[grp_rows(g, win_keep), :]),
                               preferred_element_type=f32))
    new_rows = n_new * NB_KV
    kept_rows = win_keep * NB_KV - new_rows
    for src_ref, add_ref, dst_ref in ((kwc_ref, kwn_ref, kwo_ref), (vwc_ref, vwn_ref, vwo_ref)):
        dst_ref[0:kept_rows, :] = src_ref[new_rows:new_rows + kept_rows, :]
        dst_ref[kept_rows:kept_rows + new_rows, :] = add_ref[...]


def _nsa_sample(page_table, q, qr, kcb, vcb, cache_sk, cache_sv, ks_new, vs_new, cache_wk, cache_wv, kw_new, vw_new):
    nb, n_pages = page_table.shape
    t = q.shape[1]
    n_phys = cache_sk.shape[0]
    keep = cache_wk.shape[1]
    assert keep == WINDOW and (t * NB_KV) % 8 == 0 and t < keep
    rows = HEADS_PER_GROUP * t
    n_sub = n_pages * SUBS_PER_PAGE

    def q_rows(a):
        a = a.reshape(nb, t, NB_KV, HEADS_PER_GROUP, NB_DH)
        return _bf(jnp.transpose(a, (0, 2, 3, 1, 4)).reshape(nb, NB_KV, rows, NB_DH))

    def q_unrows(a):
        a = a.reshape(nb, NB_KV, HEADS_PER_GROUP, t, NB_DH)
        return jnp.transpose(a, (0, 3, 1, 2, 4)).reshape(nb, t, NB_HEADS, NB_DH)

    def per_b(*shape):
        return pl.BlockSpec((None,) + shape, lambda b, pt: (b,) + (0,) * len(shape))

    def rows_view(a):
        return a.reshape(a.shape[0], a.shape[1] * NB_KV, NB_DH)

    n_cmp = n_sub - CMP_LEN // CMP_STRIDE + 1
    q_spec = per_b(NB_KV, rows, NB_DH)
    win_spec = per_b(keep * NB_KV, NB_DH)
    new_spec = per_b(t * NB_KV, NB_DH)
    cmp_spec = per_b(NB_KV * n_sub, NB_DH)
    in_specs = (_page_specs(n_pages) * 2
                + [q_spec, q_spec, cmp_spec, cmp_spec, new_spec, new_spec, win_spec, win_spec, new_spec, new_spec,
                   pl.BlockSpec((LANES, LANES), lambda b, pt: (0, 0))])
    grid_spec = pltpu.PrefetchScalarGridSpec(
        num_scalar_prefetch=1, grid=(nb,), in_specs=in_specs, out_specs=[q_spec] * 3 + [win_spec] * 2)
    sk = cache_sk.reshape(n_phys * PAGE * NB_KV, NB_DH)
    sv = cache_sv.reshape(n_phys * PAGE * NB_KV, NB_DH)
    o_c, o_s, o_w, wk_next, wv_next = pl.pallas_call(
        functools.partial(_nsa_sample_kernel, n_pages=n_pages, n_new=t, win_keep=keep),
        grid_spec=grid_spec,
        out_shape=([jax.ShapeDtypeStruct((nb, NB_KV, rows, NB_DH), jnp.float32)] * 3
                   + [jax.ShapeDtypeStruct((nb, keep * NB_KV, NB_DH), jnp.float32)] * 2),
        compiler_params=pltpu.CompilerParams(dimension_semantics=("arbitrary",), vmem_limit_bytes=VMEM_LIMIT),
        name="nsa_sample",
    )(page_table, *([sk] * n_pages), *([sv] * n_pages), q_rows(q), q_rows(qr), kcb, vcb,
      rows_view(ks_new), rows_view(vs_new), rows_view(cache_wk), rows_view(cache_wv), rows_view(kw_new),
      rows_view(vw_new), _overlap_matrix(LANES, n_cmp, LANES))
    return (q_unrows(o_c), q_unrows(o_s), q_unrows(o_w),
            wk_next.reshape(cache_wk.shape), wv_next.reshape(cache_wv.shape))


def _mm_kernel(x_ref, w_ref, o_ref, wb_ref):
    @pl.when(pl.program_id(1) == 0)
    def _():
        wb_ref[...] = w_ref[...].astype(jnp.bfloat16)

    o_ref[...] = jnp.dot(x_ref[...], wb_ref[...], preferred_element_type=jnp.float32)


def _mm_tiles(m, k, n):
    if k <= 2048:
        tm = 1024
        tn = 1024 if (n % 1024 == 0 or n >= 4096) else 512
    else:
        tm, tn = 512, 512
    return min(tm, m), min(tn, n)


def _mm(x, w):
    lead = x.shape[:-1]
    k = x.shape[-1]
    n = w.shape[1]
    x2 = _bf(x.reshape(-1, k))
    m = x2.shape[0]
    tm, tn = _mm_tiles(m, k, n)
    assert m % tm == 0
    out = pl.pallas_call(
        _mm_kernel,
        grid=(pl.cdiv(n, tn), m // tm),
        in_specs=[pl.BlockSpec((tm, k), lambda j, i: (i, 0)),
                  pl.BlockSpec((k, tn), lambda j, i: (0, j))],
        out_specs=pl.BlockSpec((tm, tn), lambda j, i: (i, j)),
        out_shape=jax.ShapeDtypeStruct((m, n), jnp.float32),
        scratch_shapes=[pltpu.VMEM((k, tn), jnp.bfloat16)],
        compiler_params=pltpu.CompilerParams(
            dimension_semantics=("arbitrary", "arbitrary"), vmem_limit_bytes=VMEM_LIMIT),
        name="mm",
    )(x2, w)
    return out.reshape(*lead, n)


def _mm_res_kernel(x_ref, w_ref, r_ref, o_ref, wb_ref):
    @pl.when(pl.program_id(1) == 0)
    def _():
        wb_ref[...] = _bf(w_ref[...])

    o_ref[...] = r_ref[...] + jnp.dot(x_ref[...], wb_ref[...], preferred_element_type=jnp.float32)


def _mm_res(x, w, resid):
    lead = x.shape[:-1]
    k = x.shape[-1]
    n = w.shape[1]
    x2 = _bf(x.reshape(-1, k))
    r2 = resid.reshape(-1, n)
    m = x2.shape[0]
    tm, tn = _mm_tiles(m, k, n)
    assert m % tm == 0 and r2.shape == (m, n)
    tile = pl.BlockSpec((tm, tn), lambda j, i: (i, j))
    out = pl.pallas_call(
        _mm_res_kernel,
        grid=(pl.cdiv(n, tn), m // tm),
        in_specs=[pl.BlockSpec((tm, k), lambda j, i: (i, 0)), pl.BlockSpec((k, tn), lambda j, i: (0, j)), tile],
        out_specs=tile,
        out_shape=jax.ShapeDtypeStruct((m, n), jnp.float32),
        scratch_shapes=[pltpu.VMEM((k, tn), jnp.bfloat16)],
        compiler_params=pltpu.CompilerParams(
            dimension_semantics=("arbitrary", "arbitrary"), vmem_limit_bytes=VMEM_LIMIT),
        name="mm_res",
    )(x2, w, r2)
    return out.reshape(*lead, n)


def _ple_kernel(hn_ref, wg_ref, p_ref, wp_ref, h_ref, o_ref, wgb_ref, wpb_ref):
    @pl.when(pl.program_id(1) == 0)
    def _():
        wgb_ref[...] = _bf(wg_ref[...])
        wpb_ref[...] = _bf(wp_ref[...])

    gate = jax.nn.sigmoid(jnp.dot(hn_ref[...], wgb_ref[...], preferred_element_type=jnp.float32))
    o_ref[...] = h_ref[...] + jnp.dot(p_ref[...], wpb_ref[...], preferred_element_type=jnp.float32) * gate


def _ple_fused(h, hn, p, w_p, w_g):
    lead = h.shape[:-1]
    d = h.shape[-1]
    kp = p.shape[-1]
    h2 = h.reshape(-1, d)
    m = h2.shape[0]
    tm, tn = min(m, PLE_TM), PLE_TN
    assert m % tm == 0 and d % tn == 0 and w_g.shape == (d, d) and w_p.shape == (kp, d)
    tile = pl.BlockSpec((tm, tn), lambda j, i: (i, j))
    out = pl.pallas_call(
        _ple_kernel,
        grid=(d // tn, m // tm),
        in_specs=[pl.BlockSpec((tm, d), lambda j, i: (i, 0)), pl.BlockSpec((d, tn), lambda j, i: (0, j)),
                  pl.BlockSpec((tm, kp), lambda j, i: (i, 0)), pl.BlockSpec((kp, tn), lambda j, i: (0, j)), tile],
        out_specs=tile,
        out_shape=jax.ShapeDtypeStruct((m, d), jnp.float32),
        scratch_shapes=[pltpu.VMEM((d, tn), jnp.bfloat16), pltpu.VMEM((kp, tn), jnp.bfloat16)],
        compiler_params=pltpu.CompilerParams(
            dimension_semantics=("arbitrary", "arbitrary"), vmem_limit_bytes=VMEM_LIMIT),
        name="ple",
    )(_bf(hn.reshape(-1, d)), w_g, _bf(p.reshape(-1, kp)), w_p, h2)
    return out.reshape(*lead, d)


def _ffn_up_kernel(x_ref, wu_ref, wg_ref, cw_ref, cb_ref, prev_ref, act_ref, state_ref, wub_ref, wgb_ref, halo_ref):
    m = pl.program_id(1)
    tm = x_ref.shape[0]
    f32 = jnp.float32

    @pl.when(m == 0)
    def _():
        wub_ref[...] = _bf(wu_ref[...])
        wgb_ref[...] = _bf(wg_ref[...])
        halo_ref[0:CONV_W - 1, :] = prev_ref[...].astype(f32)

    x = x_ref[...]
    u = jnp.dot(x, wub_ref[...], preferred_element_type=f32)
    g = jnp.dot(x, wgb_ref[...], preferred_element_type=f32)
    row = lax.broadcasted_iota(jnp.int32, (tm, 1), 0)
    before1 = halo_ref[1:2, :]
    before2 = halo_ref[0:1, :]
    g1 = jnp.where(row == 0, before1, pltpu.roll(g, 1, 0))
    g2 = pltpu.roll(g, 2, 0)
    g2 = jnp.where(row == 0, before2, jnp.where(row == 1, before1, g2))
    conv = cb_ref[...] + cw_ref[0:1, :] * g2 + cw_ref[1:2, :] * g1 + cw_ref[2:3, :] * g
    act_ref[...] = (jax.nn.silu(conv) * u).astype(act_ref.dtype)
    tail = g[tm - (CONV_W - 1):tm, :]
    halo_ref[0:CONV_W - 1, :] = tail
    state_ref[...] = tail


def _ffn_up_prompt(x, w_up, conv_w, conv_b, prev):
    m, k = x.shape
    assert CONV_W == 3 and m % FFN_TM == 0 and w_up.shape == (k, 2 * D_FF)
    w_u, w_g = w_up[:, :D_FF], w_up[:, D_FF:]

    def col(r):
        return pl.BlockSpec((r, FFN_TN), lambda j, i: (0, j))

    return pl.pallas_call(
        _ffn_up_kernel,
        grid=(pl.cdiv(D_FF, FFN_TN), m // FFN_TM),
        in_specs=[pl.BlockSpec((FFN_TM, k), lambda j, i: (i, 0)), col(k), col(k), col(CONV_W), col(1), col(CONV_W - 1)],
        out_specs=[pl.BlockSpec((FFN_TM, FFN_TN), lambda j, i: (i, j)), col(CONV_W - 1)],
        out_shape=[jax.ShapeDtypeStruct((m, D_FF), jnp.bfloat16),
                   jax.ShapeDtypeStruct((CONV_W - 1, D_FF), jnp.float32)],
        scratch_shapes=[pltpu.VMEM((k, FFN_TN), jnp.bfloat16), pltpu.VMEM((k, FFN_TN), jnp.bfloat16),
                        pltpu.VMEM((8, FFN_TN), jnp.float32)],
        compiler_params=pltpu.CompilerParams(
            dimension_semantics=("arbitrary", "arbitrary"), vmem_limit_bytes=VMEM_LIMIT),
        name="ffn_up_prompt",
    )(_bf(x), w_u, w_g, conv_w, conv_b.reshape(1, D_FF), prev)


def _rmsnorm(x, g):
    xf = x.astype(jnp.float32)
    y = xf * lax.rsqrt(jnp.mean(xf * xf, axis=-1, keepdims=True) + EPS)
    return (y * g.astype(jnp.float32)).astype(x.dtype)


def _rope(x, pos):
    half = x.shape[-1] // 2
    inv = ROPE_THETA ** (-jnp.arange(half, dtype=jnp.float32) / half)
    ang = pos.astype(jnp.float32)[:, None] * inv[None, :]
    cos, sin = jnp.cos(ang)[:, None, :], jnp.sin(ang)[:, None, :]
    xf = x.astype(jnp.float32)
    x1, x2 = xf[..., :half], xf[..., half:]
    return jnp.concatenate([x1 * cos - x2 * sin, x2 * cos + x1 * sin], axis=-1).astype(x.dtype)


def _gla_chunked(q, k, v, logf, s0):
    b, t, h, dk = q.shape
    dv = v.shape[-1]
    c = math.gcd(t, HA_CHUNK)
    n = t // c

    def to_chunks(a):
        return jnp.transpose(a.reshape(b, n, c, h, a.shape[-1]), (1, 0, 3, 2, 4))

    qc, kc, vc, gc = to_chunks(q), to_chunks(k), to_chunks(v), to_chunks(logf)
    bcum = jnp.cumsum(gc, axis=3)
    blast = bcum[:, :, :, -1, :]
    q_in = qc * jnp.exp(bcum)
    k_in = kc * jnp.exp(-bcum)
    k_dec = kc * jnp.exp(blast[:, :, :, None, :] - bcum)
    causal = jnp.tril(jnp.ones((c, c), dtype=bool))
    att = jnp.where(causal, jnp.einsum('nbhik,nbhjk->nbhij', q_in, k_in), 0.0)
    o_intra = jnp.einsum('nbhij,nbhjv->nbhiv', att, vc)

    def step(s, inp):
        qi, kd, vi, dl = inp
        o_inter = jnp.einsum('bhik,bhkv->bhiv', qi, s)
        s = jnp.exp(dl)[..., None] * s + jnp.einsum('bhik,bhiv->bhkv', kd, vi)
        return s, o_inter

    s_final, o_inter = lax.scan(step, s0.astype(jnp.float32), (q_in, k_dec, vc, blast))
    o = o_intra + o_inter
    return jnp.transpose(o, (1, 0, 3, 2, 4)).reshape(b, t, h, dv), s_final


def _hgrn2_layer(xn, s0, lb, w_in, out_gain, w_out, resid):
    b, t, _ = xn.shape
    hk = HA_HEADS * HA_DK
    hv = HA_HEADS * HA_DV
    proj = _mm(xn, w_in)
    if s0 is None:
        assert b == 1
        o, s_new = _gla_prompt(proj[0], lb, out_gain)
        return _mm_res(o[None], w_out, resid), s_new[None]
    q = jax.nn.silu(proj[..., :hk])
    f = lb + (1.0 - lb) * jax.nn.sigmoid(proj[..., hk:2 * hk])
    i_in = proj[..., 2 * hk:2 * hk + hv]
    gate = proj[..., 2 * hk + hv:]
    shp_k = (b, t, HA_HEADS, HA_DK)
    shp_v = (b, t, HA_HEADS, HA_DV)
    o, s_new = _gla_chunked(q.reshape(shp_k), (1.0 - f).reshape(shp_k), i_in.reshape(shp_v),
                            jnp.log(f).reshape(shp_k), s0)
    o = _rmsnorm(o, out_gain.reshape(HA_HEADS, HA_DV)) * jax.nn.silu(gate).reshape(shp_v)
    return _mm_res(o.reshape(b, t, hv), w_out, resid), s_new


def _conv_ffn(xn, prev, w_up, conv_w, conv_b, w_down, resid):
    b, t, _ = xn.shape
    if b == 1 and t % FFN_TM == 0:
        act, state = _ffn_up_prompt(xn[0], w_up, conv_w, conv_b, prev[0])
        return _mm_res(act[None], w_down, resid), state[None].astype(xn.dtype)
    up = _mm(xn, w_up)
    u, gt = up[..., :D_FF], up[..., D_FF:]
    ext = jnp.concatenate([prev.astype(gt.dtype), gt], axis=1)
    conv = conv_b
    for j in range(CONV_W):
        conv = conv + conv_w[j] * ext[:, j:j + t]
    return _mm_res(jax.nn.silu(conv) * u, w_down, resid), ext[:, t:]


def _ple(h, p, gain, w_p, w_g):
    return _ple_fused(h, _rmsnorm(h, gain), p, w_p, w_g)


def _compress(rows, pe, w1, w2):
    b, l, g, d = rows.shape
    n_sub = l // CMP_STRIDE
    r = CMP_LEN // CMP_STRIDE
    n_cmp = n_sub - r + 1
    sub = rows[:, :n_sub * CMP_STRIDE].reshape(b, n_sub, CMP_STRIDE, g, d)
    pre = None
    for j in range(r):
        sl = slice(j * CMP_STRIDE, (j + 1) * CMP_STRIDE)
        part = jnp.einsum('bnsgd,sde->bnge', sub + pe[sl][None, None, :, None, :], w1[sl])[:, j:j + n_cmp]
        pre = part if pre is None else pre + part
    return jnp.einsum('bnge,ef->bngf', jax.nn.gelu(pre), w2)


def _nsa_layer(xn, pos, shared, w_in, q_gain, w_out, banded, resid):
    b, t, _ = xn.shape
    hd = NB_HEADS * NB_DH
    proj = _mm(xn, w_in)
    q = _rmsnorm(proj[..., :hd].reshape(b, t, NB_HEADS, NB_DH), q_gain)
    gates = jax.nn.sigmoid(proj[..., hd:].astype(jnp.float32)).reshape(b, t, 3, NB_HEADS)
    qr = _rope(q, pos)
    win_next = None
    if banded:
        assert b == 1
        kcb, vcb, kpad, vpad, win_k, win_v = shared
        o_c2, sel = _cmp_topk_prompt(q.reshape(t, hd), kcb[0], vcb[0])
        o_s2, o_w2 = _slc_win_prompt(qr.reshape(t, hd), kpad[0], vpad[0], win_k[0], win_v[0], sel)
        o_c = o_c2.reshape(b, t, NB_HEADS, NB_DH)
        o_s = o_s2.reshape(b, t, NB_HEADS, NB_DH)
        o_w = o_w2.reshape(b, t, NB_HEADS, NB_DH)
    else:
        o_c, o_s, o_w, wk_next, wv_next = _nsa_sample(shared[0], q, qr, *shared[1:])
        win_next = (wk_next, wv_next)
    o = (gates[:, :, 0, :, None] * o_c + gates[:, :, 1, :, None] * o_s + gates[:, :, 2, :, None] * o_w)
    return _mm_res(o.reshape(b, t, hd), w_out, resid), win_next


def _shared_kv(h, pos, past, wt):
    b, t, _ = h.shape
    kv = _mm(_rmsnorm(h, wt['kv_norm']), wt['kv_w']).reshape(b, t, 6, NB_KV, NB_DH)
    k_cmp, v_cmp = kv[:, :, 0], kv[:, :, 1]
    k_slc = _rope(_rmsnorm(kv[:, :, 2], wt['k_norm_slc']), pos)
    v_slc = kv[:, :, 3]
    k_win = _rope(_rmsnorm(kv[:, :, 4], wt['k_norm_win']), pos)
    v_win = kv[:, :, 5]
    if past is None:
        keep = min(WINDOW, t)
        kcb = _rmsnorm(_compress(k_cmp, wt['cmp_pe_k'], wt['cmp_w1_k'], wt['cmp_w2_k']), wt['k_norm_cmp'])
        vcb = _compress(v_cmp, wt['cmp_pe_v'], wt['cmp_w1_v'], wt['cmp_w2_v'])
        assert t % SLC_LEN == 0
        shared = (kcb, vcb, k_slc, v_slc, k_win, v_win)
        rows = (k_cmp, v_cmp, k_slc, v_slc, k_win[:, -keep:], v_win[:, -keep:])
    else:
        page_table, ckc, cvc, cks, cvs, cwk, cwv = past
        assert t < CMP_STRIDE and (page_table.shape[1] * PAGE) % CMP_STRIDE == 0
        kcb, vcb = _cmp_pages(page_table, ckc, cvc, wt['cmp_pe_k'], wt['cmp_pe_v'], wt['cmp_w1_k'], wt['cmp_w1_v'],
                              wt['cmp_w2_k'], wt['cmp_w2_v'], wt['k_norm_cmp'])
        shared = (page_table, kcb, vcb, cks, cvs, k_slc, v_slc, cwk, cwv, k_win, v_win)
        rows = (k_cmp, v_cmp, k_slc, v_slc, None, None)
    return shared, rows


def _trunk(x, p, start, s_hgrn, s_conv, past, wt):
    b, t, _ = x.shape
    pos = start + jnp.arange(t)
    banded = past is None
    lb = jnp.cumsum(jax.nn.softmax(wt['a_lb_logits'].astype(jnp.float32), axis=0), axis=0)
    h = x
    hgrn_out, conv_out = [], []
    shared, rows = None, None
    for i in range(DEPTH):
        hn = _rmsnorm(h, wt['norm_mix'][i])
        if i < N_A:
            h, s = _hgrn2_layer(hn, None if s_hgrn is None else s_hgrn[i], lb[i], wt['a_w_in'][i],
                                wt['a_out_norm'][i], wt['a_w_out'][i], h)
            hgrn_out.append(s)
        else:
            j = i - N_A
            h, win_next = _nsa_layer(hn, pos, shared, wt['b_w_in'][j], wt['b_q_norm'][j], wt['b_w_out'][j],
                                     banded, h)
            if win_next is not None:
                assert N_B == 1
                rows = rows[:4] + win_next
        h, cs = _conv_ffn(_rmsnorm(h, wt['norm_ffn'][i]), s_conv[i], wt['ffn_w_up'][i], wt['ffn_conv_w'][i],
                          wt['ffn_conv_b'][i], wt['ffn_w_down'][i], h)
        conv_out.append(cs)
        h = _ple(h, p[i], wt['norm_ple'][i], wt['ple_w_p'][i], wt['ple_w_g'][i])
        if i == N_A - 1:
            shared, rows = _shared_kv(h, pos, past, wt)
    return h, jnp.stack(hgrn_out), jnp.stack(conv_out), rows


def kernel(x_prompt, x_sample, state_hgrn, state_conv, cache_cmp_k, cache_cmp_v, cache_slc_k, cache_slc_v,
           cache_win_k, cache_win_v, page_table, p_prompt, p_sample, norm_mix, norm_ffn, norm_ple,
           a_w_in, a_lb_logits, a_out_norm, a_w_out, kv_norm, kv_w, k_norm_cmp, k_norm_slc, k_norm_win,
           cmp_pe_k, cmp_pe_v, cmp_w1_k, cmp_w2_k, cmp_w1_v, cmp_w2_v, b_w_in, b_q_norm, b_w_out,
           ffn_w_up, ffn_conv_w, ffn_conv_b, ffn_w_down, ple_w_p, ple_w_g):
    wt = {
        'norm_mix': norm_mix, 'norm_ffn': norm_ffn, 'norm_ple': norm_ple,
        'a_w_in': a_w_in, 'a_lb_logits': a_lb_logits, 'a_out_norm': a_out_norm, 'a_w_out': a_w_out,
        'kv_norm': kv_norm, 'kv_w': kv_w, 'k_norm_cmp': k_norm_cmp, 'k_norm_slc': k_norm_slc,
        'k_norm_win': k_norm_win, 'cmp_pe_k': cmp_pe_k, 'cmp_pe_v': cmp_pe_v, 'cmp_w1_k': cmp_w1_k,
        'cmp_w2_k': cmp_w2_k, 'cmp_w1_v': cmp_w1_v, 'cmp_w2_v': cmp_w2_v,
        'b_w_in': b_w_in, 'b_q_norm': b_q_norm, 'b_w_out': b_w_out,
        'ffn_w_up': ffn_w_up, 'ffn_conv_w': ffn_conv_w, 'ffn_conv_b': ffn_conv_b, 'ffn_w_down': ffn_w_down,
        'ple_w_p': ple_w_p, 'ple_w_g': ple_w_g,
    }
    bp, past_len = x_prompt.shape[0], page_table.shape[1] * cache_cmp_k.shape[1]
    s_conv0 = jnp.zeros((DEPTH, bp, CONV_W - 1, D_FF), x_prompt.dtype)
    y_prompt, hgrn_p, conv_p, rows_p = _trunk(x_prompt, p_prompt, 0, None, s_conv0, None, wt)
    past = (page_table, cache_cmp_k, cache_cmp_v, cache_slc_k, cache_slc_v, cache_win_k, cache_win_v)
    y_sample, hgrn_s, conv_s, rows_s = _trunk(x_sample, p_sample, past_len, state_hgrn, state_conv, past, wt)
    kc_p, vc_p, ks_p, vs_p, wk_p, wv_p = rows_p
    kc_s, vc_s, ks_s, vs_s, wk_s, wv_s = rows_s
    return (y_prompt, y_sample, hgrn_p, hgrn_s, conv_p, conv_s, kc_p, vc_p, ks_p, vs_p,
            kc_s, vc_s, ks_s, vs_s, wk_p, wv_p, wk_s, wv_s)
```

```python
import functools
import math

import jax
import jax.numpy as jnp
from jax import lax
from jax.experimental import pallas as pl
from jax.experimental.pallas import tpu as pltpu

D_MODEL = 2048
DEPTH = 2
N_A = DEPTH // 2
N_B = DEPTH - N_A
EPS = 1e-6
HA_DK = 128
HA_HEADS = D_MODEL // HA_DK
HA_DV = D_MODEL // HA_HEADS
HA_CHUNK = 32
NB_DH = 128
NB_HEADS = D_MODEL // NB_DH
NB_KV = 4
HEADS_PER_GROUP = NB_HEADS // NB_KV
CMP_LEN = 32
CMP_STRIDE = 16
SLC_LEN = 64
N_SELECT = 16
WINDOW = 512
Q_BLOCK = 128
ROPE_THETA = 10000.0
D_FF = ((8 * D_MODEL // 3 + 127) // 128) * 128
CONV_W = 3
NEG = -1e30
BIG = 1e30
LOG2_E = math.log2(math.e)
REMOVED = -3.0e38
LANES = 128
SLC_KV_TILE = 1024
WIN_SPAN = WINDOW + Q_BLOCK
GLA_ROWS = 256
GLA_HEADS = 4
PAGE = 128
SUBS_PER_PAGE = PAGE // CMP_STRIDE
SUB_ROWS = CMP_STRIDE * NB_KV
SUB_PITCH = SUB_ROWS + 8
PLE_TM = 1024
PLE_TN = 512
FFN_TM = 1024
FFN_TN = 512
SLC_GROUPS = 2
VMEM_LIMIT = 48 * 1024 * 1024
SLC_VMEM_LIMIT = 56 * 1024 * 1024
_NT = (((1,), (1,)), ((), ()))
_TN = (((0,), (0,)), ((), ()))


def _bf(x):
    return x.astype(jnp.bfloat16)


def _cmp_topk_kernel(q_ref, k_ref, v_ref, ov_ref, gate_ref, o_ref, sel_ref, *, q_blk, n_cmp_pad, n_slc, n_sel):
    i = pl.program_id(0)
    f32 = jnp.float32
    pos = i * q_blk + lax.broadcasted_iota(jnp.int32, (q_blk, 1), 0)
    kend = lax.broadcasted_iota(jnp.int32, (1, n_cmp_pad), 1) * CMP_STRIDE + (CMP_LEN - 1)
    mask_c = kend <= pos
    scale = NB_DH ** -0.5
    ov = ov_ref[...]
    imp = []
    for g in range(NB_KV):
        k = k_ref[g]
        v = v_ref[g]
        psum = jnp.zeros((q_blk, n_cmp_pad), f32)
        for j in range(HEADS_PER_GROUP):
            cs = slice((g * HEADS_PER_GROUP + j) * NB_DH, (g * HEADS_PER_GROUP + j + 1) * NB_DH)
            s = lax.dot_general(_bf(q_ref[:, cs]), k, _NT, preferred_element_type=f32) * scale
            s = jnp.where(mask_c, s, NEG)
            m = jnp.max(s, axis=-1, keepdims=True)
            e = jnp.where(mask_c, jnp.exp(s - m), 0.0)
            p = e / jnp.maximum(jnp.sum(e, axis=-1, keepdims=True), 1e-30)
            h = g * HEADS_PER_GROUP + j
            o_ref[:, cs] = gate_ref[:, h:h + 1] * jnp.dot(_bf(p), v, preferred_element_type=f32)
            psum = psum + p
        hi = _bf(psum)
        lo = _bf(psum - hi.astype(f32))
        imp.append(jnp.dot(hi, ov, preferred_element_type=f32) + jnp.dot(lo, ov, preferred_element_type=f32))
    imp = jnp.concatenate(imp, axis=0)
    pos_r = jnp.concatenate([pos] * NB_KV, axis=0)
    sidx = lax.broadcasted_iota(jnp.int32, (1, n_slc), 1)
    cur = lax.shift_right_logical(pos_r, 6)
    forced = (sidx == 0) | (sidx == cur) | (sidx == cur - 1)
    future = sidx * SLC_LEN > pos_r
    score = jnp.where(future, NEG, jnp.where(forced, BIG, imp))
    lane = lax.broadcasted_iota(jnp.int32, (NB_KV * q_blk, n_slc), 1).astype(f32)
    sel = jnp.zeros((NB_KV * q_blk, n_slc), f32)
    for _ in range(n_sel):
        m = jnp.max(score, axis=-1, keepdims=True)
        idx = jnp.min(jnp.where(score == m, lane, float(n_slc)), axis=-1, keepdims=True)
        hit = lane == idx
        sel = jnp.where(hit, 1.0, sel)
        score = jnp.where(hit, REMOVED, score)
    for g in range(NB_KV):
        sel_ref[:, g * n_slc:(g + 1) * n_slc] = sel[g * q_blk:(g + 1) * q_blk].astype(sel_ref.dtype)


def _overlap_matrix(n_cmp_rows, n_cmp, n_slc_cols):
    cstart = jnp.arange(n_cmp_rows) * CMP_STRIDE
    sstart = jnp.arange(n_slc_cols) * SLC_LEN
    return _bf((cstart[:, None] <= sstart[None, :] + (SLC_LEN - 1))
               & (cstart[:, None] + (CMP_LEN - 1) >= sstart[None, :])
               & (jnp.arange(n_cmp_rows)[:, None] < n_cmp))


def _cmp_topk_prompt(q2d, kcb, vcb, gate_c):
    t = q2d.shape[0]
    n_cmp = kcb.shape[0]
    n_cmp_pad = -(-n_cmp // LANES) * LANES
    n_slc = t // SLC_LEN
    assert SLC_LEN == 64 and n_slc == LANES and t % Q_BLOCK == 0
    n_sel = min(N_SELECT, n_slc)
    padc = ((0, 0), (0, n_cmp_pad - n_cmp), (0, 0))
    k = _bf(jnp.pad(jnp.transpose(kcb, (1, 0, 2)), padc))
    v = _bf(jnp.pad(jnp.transpose(vcb, (1, 0, 2)), padc))
    hd = NB_HEADS * NB_DH
    kern = functools.partial(_cmp_topk_kernel, q_blk=Q_BLOCK, n_cmp_pad=n_cmp_pad, n_slc=n_slc, n_sel=n_sel)
    kv_spec = pl.BlockSpec((NB_KV, n_cmp_pad, NB_DH), lambda i: (0, 0, 0))
    return pl.pallas_call(
        kern,
        grid=(t // Q_BLOCK,),
        in_specs=[pl.BlockSpec((Q_BLOCK, hd), lambda i: (i, 0)), kv_spec, kv_spec,
                  pl.BlockSpec((n_cmp_pad, n_slc), lambda i: (0, 0)),
                  pl.BlockSpec((Q_BLOCK, NB_HEADS), lambda i: (i, 0))],
        out_specs=[pl.BlockSpec((Q_BLOCK, hd), lambda i: (i, 0)),
                   pl.BlockSpec((Q_BLOCK, NB_KV * n_slc), lambda i: (i, 0))],
        out_shape=[jax.ShapeDtypeStruct((t, hd), jnp.float32),
                   jax.ShapeDtypeStruct((t, NB_KV * n_slc), jnp.bfloat16)],
        compiler_params=pltpu.CompilerParams(dimension_semantics=("arbitrary",), vmem_limit_bytes=VMEM_LIMIT),
        name="cmp_topk_prompt",
    )(q2d, k, v, _overlap_matrix(n_cmp_pad, n_cmp, n_slc), gate_c)


def _slc_win_kernel(q_ref, ks_ref, vs_ref, kw_ref, vw_ref, sel_ref, oc_ref, gate_ref, o_ref, m_ref, l_ref, acc_ref,
                    *, q_blk, n_slc, kv_tile, n_grp):
    i = pl.program_id(1)
    rows = HEADS_PER_GROUP * q_blk
    gd = HEADS_PER_GROUP * NB_DH
    blocks_per_tile = kv_tile // SLC_LEN
    reps = kv_tile // LANES
    scale = NB_DH ** -0.5
    pos = i * q_blk + lax.broadcasted_iota(jnp.int32, (q_blk, 1), 0)
    q4 = [jnp.concatenate([q_ref[:, gg * gd + j * NB_DH:gg * gd + (j + 1) * NB_DH] for j in range(HEADS_PER_GROUP)],
                          axis=0) for gg in range(n_grp)]
    sel = [sel_ref[:, gg * n_slc:(gg + 1) * n_slc] for gg in range(n_grp)]
    m_ref[...] = jnp.full(m_ref.shape, NEG, jnp.float32)
    l_ref[...] = jnp.zeros(l_ref.shape, jnp.float32)
    acc_ref[...] = jnp.zeros(acc_ref.shape, jnp.float32)
    blk_row = lax.broadcasted_iota(jnp.int32, (n_slc, kv_tile), 0)
    blk_col = lax.shift_right_logical(lax.broadcasted_iota(jnp.int32, (n_slc, kv_tile), 1), 6)
    key_off = lax.broadcasted_iota(jnp.int32, (1, kv_tile), 1)
    n_tiles = lax.div(i * q_blk + (q_blk - 1), kv_tile) + 1

    def body(j, carry):
        start = pl.multiple_of(j * kv_tile, kv_tile)
        expand = _bf(jnp.where(blk_row == blk_col + j * blocks_per_tile, 1.0, 0.0))
        causal = key_off + start <= pos
        for gg in range(n_grp):
            kt = ks_ref[gg, pl.ds(start, kv_tile), :]
            vt = vs_ref[gg, pl.ds(start, kv_tile), :]
            s = lax.dot_general(q4[gg], kt, _NT, preferred_element_type=jnp.float32) * (scale * LOG2_E)
            picked = jnp.dot(sel[gg], expand, preferred_element_type=jnp.float32) > 0.5
            mask = picked & causal
            s = jnp.where(mask[None], s.reshape(HEADS_PER_GROUP, q_blk, kv_tile), NEG).reshape(rows, kv_tile)
            m_prev = m_ref[gg]
            m_new = jnp.maximum(m_prev, jnp.max(s, axis=-1, keepdims=True))
            alpha = jnp.exp2(m_prev - m_new)
            p = jnp.exp2(s - jnp.concatenate([m_new] * reps, axis=1))
            l_ref[gg] = alpha * l_ref[gg] + jnp.sum(p, axis=-1, keepdims=True)
            acc_ref[gg] = alpha * acc_ref[gg] + jnp.dot(_bf(p), vt, preferred_element_type=jnp.float32)
            m_ref[gg] = m_new
        return carry

    lax.fori_loop(0, n_tiles, body, 0)

    wstart = pl.multiple_of(jnp.maximum(i - WINDOW // q_blk, 0) * q_blk, q_blk)
    diff = pos - (wstart + lax.broadcasted_iota(jnp.int32, (1, WIN_SPAN), 1))
    mask = ((diff >= 0) & (diff <= WINDOW))[None]
    n_ghost = jnp.maximum(WINDOW - pos, 0).astype(jnp.float32)[None]
    heads = n_grp * HEADS_PER_GROUP
    for gg in range(n_grp):
        out = acc_ref[gg] / jnp.maximum(l_ref[gg], 1e-30)
        kt = kw_ref[gg, pl.ds(wstart, WIN_SPAN), :]
        vt = vw_ref[gg, pl.ds(wstart, WIN_SPAN), :]
        s = lax.dot_general(q4[gg], kt, _NT, preferred_element_type=jnp.float32) * scale
        s = jnp.where(mask, s.reshape(HEADS_PER_GROUP, q_blk, WIN_SPAN), NEG)
        m = jnp.max(s, axis=-1, keepdims=True)
        m = jnp.where(n_ghost > 0, jnp.maximum(m, 0.0), m)
        e = jnp.where(mask, jnp.exp(s - m), 0.0)
        p = e / jnp.maximum(jnp.sum(e, axis=-1, keepdims=True) + n_ghost * jnp.exp(-m), 1e-30)
        ow = jnp.dot(_bf(p.reshape(rows, WIN_SPAN)), vt, preferred_element_type=jnp.float32)
        for j in range(HEADS_PER_GROUP):
            cs = slice(gg * gd + j * NB_DH, gg * gd + (j + 1) * NB_DH)
            h = gg * HEADS_PER_GROUP + j
            rs = slice(j * q_blk, (j + 1) * q_blk)
            o_ref[:, cs] = (oc_ref[:, cs] + gate_ref[:, h:h + 1] * out[rs, :]
                            + gate_ref[:, heads + h:heads + h + 1] * ow[rs, :]).astype(o_ref.dtype)


def _slc_win_prompt(qr2d, kpad, vpad, kwin, vwin, sel, oc_gated, gate_sw):
    t = qr2d.shape[0]
    n_slc = t // SLC_LEN
    assert t % SLC_KV_TILE == 0 and SLC_KV_TILE % SLC_LEN == 0 and t >= WIN_SPAN and NB_KV % SLC_GROUPS == 0

    def group_major(a):
        return _bf(jnp.transpose(a, (1, 0, 2)))

    gd = SLC_GROUPS * HEADS_PER_GROUP * NB_DH
    rows = HEADS_PER_GROUP * Q_BLOCK
    kern = functools.partial(_slc_win_kernel, q_blk=Q_BLOCK, n_slc=n_slc, kv_tile=SLC_KV_TILE, n_grp=SLC_GROUPS)
    kv_spec = pl.BlockSpec((SLC_GROUPS, t, NB_DH), lambda g, i: (g, 0, 0))
    o_spec = pl.BlockSpec((Q_BLOCK, gd), lambda g, i: (i, g))
    n_gp = NB_KV // SLC_GROUPS
    heads = SLC_GROUPS * HEADS_PER_GROUP
    gate_blk = jnp.transpose(gate_sw.reshape(t, 2, n_gp, heads), (2, 0, 1, 3)).reshape(n_gp, t, 2 * heads)
    return pl.pallas_call(
        kern,
        grid=(n_gp, t // Q_BLOCK),
        in_specs=[o_spec, kv_spec, kv_spec, kv_spec, kv_spec,
                  pl.BlockSpec((Q_BLOCK, SLC_GROUPS * n_slc), lambda g, i: (i, g)), o_spec,
                  pl.BlockSpec((None, Q_BLOCK, 2 * heads), lambda g, i: (g, i, 0))],
        out_specs=o_spec,
        out_shape=jax.ShapeDtypeStruct((t, NB_HEADS * NB_DH), jnp.bfloat16),
        scratch_shapes=[
            pltpu.VMEM((SLC_GROUPS, rows, LANES), jnp.float32),
            pltpu.VMEM((SLC_GROUPS, rows, LANES), jnp.float32),
            pltpu.VMEM((SLC_GROUPS, rows, NB_DH), jnp.float32),
        ],
        compiler_params=pltpu.CompilerParams(
            dimension_semantics=("arbitrary", "arbitrary"), vmem_limit_bytes=SLC_VMEM_LIMIT),
        name="slc_win_prompt",
    )(_bf(qr2d), group_major(kpad), group_major(vpad), group_major(kwin), group_major(vwin), sel, oc_gated, gate_blk)


def _split3(x):
    hi = x.astype(jnp.bfloat16)
    r = x - hi.astype(jnp.float32)
    mid = r.astype(jnp.bfloat16)
    lo = (r - mid.astype(jnp.float32)).astype(jnp.bfloat16)
    return hi, mid, lo


def _gla_kernel(xq_ref, xf_ref, xv_ref, xg_ref, lb_ref, gain_ref, o_ref, sfin_ref, st_ref, *, rows, heads, chunk):
    i = pl.program_id(1)

    @pl.when(i == 0)
    def _():
        st_ref[...] = jnp.zeros(st_ref.shape, jnp.float32)

    r_idx = lax.broadcasted_iota(jnp.int32, (rows, rows), 0)
    c_idx = lax.broadcasted_iota(jnp.int32, (rows, rows), 1)
    same = lax.shift_right_logical(r_idx, 5) == lax.shift_right_logical(c_idx, 5)
    causal = same & (c_idx <= r_idx)
    tri = jnp.where(causal, 1.0, 0.0).astype(jnp.bfloat16)
    blk = jnp.where(same, 1.0, 0.0).astype(jnp.bfloat16)
    f32 = jnp.float32
    for hh in range(heads):
        sl = slice(hh * HA_DK, (hh + 1) * HA_DK)
        lb = lb_ref[:, sl]
        qh = jax.nn.silu(xq_ref[:, sl])
        f = lb + (1.0 - lb) * jax.nn.sigmoid(xf_ref[:, sl])
        kh = 1.0 - f
        gh = jnp.log(f)
        parts = _split3(gh)
        bcum = sum(jnp.dot(tri, p, preferred_element_type=f32) for p in parts)
        tot = sum(jnp.dot(blk, p, preferred_element_type=f32) for p in parts)
        q_in = (qh * jnp.exp(bcum)).astype(jnp.bfloat16)
        k_in = (kh * jnp.exp(-bcum)).astype(jnp.bfloat16)
        k_dec = (kh * jnp.exp(tot - bcum)).astype(jnp.bfloat16)
        vb = _bf(xv_ref[:, sl])
        att = lax.dot_general(q_in, k_in, _NT, preferred_element_type=f32)
        att = jnp.where(causal, att, 0.0).astype(jnp.bfloat16)
        o_intra = jnp.dot(att, vb, preferred_element_type=f32)
        st = st_ref[hh]
        o_inter = []
        for c in range(rows // chunk):
            rs = slice(c * chunk, (c + 1) * chunk)
            o_inter.append(lax.dot_general(q_in[rs], st.astype(jnp.bfloat16), _NT, preferred_element_type=f32))
            upd = lax.dot_general(vb[rs], k_dec[rs], _TN, preferred_element_type=f32)
            st = jnp.exp(tot[c * chunk:c * chunk + 1, :]) * st + upd
        st_ref[hh] = st
        o = o_intra + jnp.concatenate(o_inter, axis=0)
        o = o * lax.rsqrt(jnp.mean(o * o, axis=-1, keepdims=True) + EPS) * gain_ref[:, sl]
        o_ref[:, sl] = (o * jax.nn.silu(xg_ref[:, sl])).astype(o_ref.dtype)

    @pl.when(i == pl.num_programs(1) - 1)
    def _():
        for hh in range(heads):
            sfin_ref[hh] = st_ref[hh].T


def _gla_prompt(proj, lb, out_gain):
    t = proj.shape[0]
    hk = HA_HEADS * HA_DK
    assert proj.shape[1] == 4 * hk and HA_DK == HA_DV
    assert t % GLA_ROWS == 0 and GLA_ROWS % HA_CHUNK == 0 and HA_CHUNK == 32 and HA_HEADS % GLA_HEADS == 0
    w = GLA_HEADS * HA_DK
    n_hb = HA_HEADS // GLA_HEADS

    def section(c):
        return pl.BlockSpec((GLA_ROWS, w), functools.partial(lambda h, i, c: (i, c * n_hb + h), c=c))

    vec_spec = pl.BlockSpec((1, w), lambda h, i: (0, h))
    kern = functools.partial(_gla_kernel, rows=GLA_ROWS, heads=GLA_HEADS, chunk=HA_CHUNK)
    return pl.pallas_call(
        kern,
        grid=(n_hb, t // GLA_ROWS),
        in_specs=[section(0), section(1), section(2), section(3), vec_spec, vec_spec],
        out_specs=[pl.BlockSpec((GLA_ROWS, w), lambda h, i: (i, h)),
                   pl.BlockSpec((GLA_HEADS, HA_DK, HA_DV), lambda h, i: (h, 0, 0))],
        out_shape=[jax.ShapeDtypeStruct((t, hk), jnp.bfloat16),
                   jax.ShapeDtypeStruct((HA_HEADS, HA_DK, HA_DV), jnp.float32)],
        scratch_shapes=[pltpu.VMEM((GLA_HEADS, HA_DV, HA_DK), jnp.float32)],
        compiler_params=pltpu.CompilerParams(
            dimension_semantics=("arbitrary", "arbitrary"), vmem_limit_bytes=VMEM_LIMIT),
        name="gla_prompt",
    )(proj, proj, proj, proj, lb.reshape(1, hk), out_gain.reshape(1, hk))


def _page_specs(n_pages):
    return [pl.BlockSpec((PAGE * NB_KV, NB_DH), functools.partial(lambda b, pt, p: (pt[b, p], 0), p=p))
            for p in range(n_pages)]


def _cmp_pages_kernel(pt_ref, *refs, n_pages):
    kp = refs[:n_pages]
    vp = refs[n_pages:2 * n_pages]
    (w1k_ref, w1v_ref, pek_ref, pev_ref, w2k_ref, w2v_ref, gain_ref, ko_ref, vo_ref, x_ref, st_ref) = refs[2 * n_pages:]
    n_sub = n_pages * SUBS_PER_PAGE
    f32 = jnp.float32
    row = lax.broadcasted_iota(jnp.int32, (NB_KV * n_sub, 1), 0)
    last = lax.rem(row, n_sub) == n_sub - 1
    for pages, w1_ref, pe_ref, w2_ref, o_ref, norm in ((kp, w1k_ref, pek_ref, w2k_ref, ko_ref, True),
                                                       (vp, w1v_ref, pev_ref, w2v_ref, vo_ref, False)):
        for p in range(n_pages):
            for n in range(SUBS_PER_PAGE):
                st_ref[pl.ds((p * SUBS_PER_PAGE + n) * SUB_PITCH, SUB_ROWS), :] = pages[p][pl.ds(n * SUB_ROWS, SUB_ROWS), :]
        for g in range(NB_KV):
            for s in range(CMP_STRIDE):
                x_ref[g * n_sub:(g + 1) * n_sub, s * NB_DH:(s + 1) * NB_DH] = _bf(
                    st_ref[pl.ds(s * NB_KV + g, n_sub, stride=SUB_PITCH), :])
        w1 = w1_ref[...]
        a01 = jnp.dot(x_ref[...], w1, preferred_element_type=f32)
        c = jnp.dot(_bf(pe_ref[...]), w1, preferred_element_type=f32)
        a0 = a01[:, :NB_DH] + c[0:1, :NB_DH]
        a1 = a01[:, NB_DH:] + c[1:2, NB_DH:]
        pre = a0 + pltpu.roll(a1, NB_KV * n_sub - 1, 0)
        y = jnp.dot(_bf(jax.nn.gelu(pre)), w2_ref[...], preferred_element_type=f32)
        if norm:
            y = y * lax.rsqrt(jnp.mean(y * y, axis=-1, keepdims=True) + EPS) * gain_ref[...]
        o_ref[...] = jnp.where(last, 0.0, y).astype(o_ref.dtype)


def _cmp_pages(page_table, cache_k, cache_v, pe_k, pe_v, w1_k, w1_v, w2_k, w2_v, gain):
    nb, n_pages = page_table.shape
    n_phys = cache_k.shape[0]
    assert cache_k.shape[1:] == (PAGE, NB_KV, NB_DH) and CMP_LEN == 2 * CMP_STRIDE
    n_sub = n_pages * SUBS_PER_PAGE
    kd = CMP_STRIDE * NB_DH

    def w1cat(w1):
        return _bf(jnp.concatenate([w1[:CMP_STRIDE].reshape(kd, NB_DH), w1[CMP_STRIDE:].reshape(kd, NB_DH)], axis=1))

    def pecat(pe):
        z = jnp.zeros((8, kd), jnp.float32)
        return z.at[0].set(pe[:CMP_STRIDE].reshape(kd)).at[1].set(pe[CMP_STRIDE:].reshape(kd))

    def full(*shape):
        return pl.BlockSpec(shape, lambda b, pt: (0,) * len(shape))

    out_spec = pl.BlockSpec((None, NB_KV * n_sub, NB_DH), lambda b, pt: (b, 0, 0))
    ck = cache_k.reshape(n_phys * PAGE * NB_KV, NB_DH)
    cv = cache_v.reshape(n_phys * PAGE * NB_KV, NB_DH)
    grid_spec = pltpu.PrefetchScalarGridSpec(
        num_scalar_prefetch=1,
        grid=(nb,),
        in_specs=_page_specs(n_pages) * 2 + [full(kd, 2 * NB_DH), full(kd, 2 * NB_DH), full(8, kd), full(8, kd),
                                            full(NB_DH, NB_DH), full(NB_DH, NB_DH), full(1, NB_DH)],
        out_specs=[out_spec, out_spec],
        scratch_shapes=[pltpu.VMEM((NB_KV * n_sub, kd), jnp.bfloat16),
                        pltpu.VMEM((n_sub * SUB_PITCH, NB_DH), jnp.float32)],
    )
    return pl.pallas_call(
        functools.partial(_cmp_pages_kernel, n_pages=n_pages),
        grid_spec=grid_spec,
        out_shape=[jax.ShapeDtypeStruct((nb, NB_KV * n_sub, NB_DH), jnp.bfloat16)] * 2,
        compiler_params=pltpu.CompilerParams(dimension_semantics=("arbitrary",), vmem_limit_bytes=VMEM_LIMIT),
        name="cmp_pages",
    )(page_table, *([ck] * n_pages), *([cv] * n_pages), w1cat(w1_k), w1cat(w1_v), pecat(pe_k), pecat(pe_v),
      _bf(w2_k), _bf(w2_v), gain.reshape(1, NB_DH))


def _softmax_with_tail(s, mask, s_tail, ok_tail):
    m = jnp.max(s, axis=-1, keepdims=True)
    for sc in s_tail:
        m = jnp.maximum(m, sc)
    e = jnp.where(mask, jnp.exp(s - m), 0.0)
    e_tail = [jnp.where(ok, jnp.exp(sc - m), 0.0) for ok, sc in zip(ok_tail, s_tail)]
    den = jnp.sum(e, axis=-1, keepdims=True)
    for et in e_tail:
        den = den + et
    inv = 1.0 / jnp.maximum(den, 1e-30)
    return e * inv, [et * inv for et in e_tail]


def _nsa_sample_kernel(pt_ref, *refs, n_pages, n_new, win_keep):
    ksp = refs[:n_pages]
    vsp = refs[n_pages:2 * n_pages]
    (q_ref, qr_ref, kcb_ref, vcb_ref, ksn_ref, vsn_ref, kwc_ref, vwc_ref, kwn_ref, vwn_ref, ov_ref,
     oc_ref, os_ref, ow_ref, kwo_ref, vwo_ref) = refs[2 * n_pages:]
    f32 = jnp.float32
    past = n_pages * PAGE
    n_sub = past // CMP_STRIDE
    n_cmp = n_sub - CMP_LEN // CMP_STRIDE + 1
    n_slc = -(-(past + n_new) // SLC_LEN)
    new_blk = past // SLC_LEN
    assert (past % SLC_LEN) + n_new <= SLC_LEN and n_sub == LANES and n_slc <= LANES
    rows = HEADS_PER_GROUP * n_new
    all_rows = NB_KV * rows
    sel_rows = NB_KV * n_new
    scale = NB_DH ** -0.5
    groups = range(NB_KV)

    def grp_rows(g, n_tok):
        return pl.ds(g, n_tok, stride=NB_KV)

    def stack(parts):
        return jnp.concatenate(parts, axis=0)

    t_row = lax.rem(lax.broadcasted_iota(jnp.int32, (all_rows, 1), 0), n_new)
    pos = past + t_row
    pos_t = past + lax.rem(lax.broadcasted_iota(jnp.int32, (sel_rows, 1), 0), n_new)
    lane = lax.broadcasted_iota(jnp.int32, (1, LANES), 1)
    q = [q_ref[g] for g in groups]
    qr = [qr_ref[g] for g in groups]
    qrf = stack(qr).astype(f32)

    def tail_scores(new_ref):
        kb = _bf(new_ref[...]).astype(f32)
        out = []
        for i in range(n_new):
            ki = stack([jnp.broadcast_to(kb[i * NB_KV + g:i * NB_KV + g + 1, :], (rows, NB_DH)) for g in groups])
            out.append(jnp.sum(qrf * ki, axis=-1, keepdims=True) * scale)
        return out

    def tail_values(p_tail, new_ref, g):
        vb = _bf(new_ref[...]).astype(f32)
        acc = _bf(p_tail[0][g * rows:(g + 1) * rows]).astype(f32) * vb[g:g + 1, :]
        for i in range(1, n_new):
            acc = acc + _bf(p_tail[i][g * rows:(g + 1) * rows]).astype(f32) * vb[i * NB_KV + g:i * NB_KV + g + 1, :]
        return acc

    mask_c = (lane * CMP_STRIDE + (CMP_LEN - 1) <= pos) & (lane < n_cmp)
    s = stack([lax.dot_general(q[g], kcb_ref[g * n_sub:(g + 1) * n_sub, :], _NT, preferred_element_type=f32)
               for g in groups]) * scale
    s = jnp.where(mask_c, s, NEG)
    m = jnp.max(s, axis=-1, keepdims=True)
    e = jnp.where(mask_c, jnp.exp(s - m), 0.0)
    p = e / jnp.maximum(jnp.sum(e, axis=-1, keepdims=True), 1e-30)
    pb = _bf(p)
    for g in groups:
        oc_ref[g] = jnp.dot(pb[g * rows:(g + 1) * rows], vcb_ref[g * n_sub:(g + 1) * n_sub, :],
                            preferred_element_type=f32)
    psum = []
    for g in groups:
        acc = p[g * rows:g * rows + n_new]
        for j in range(1, HEADS_PER_GROUP):
            acc = acc + p[g * rows + j * n_new:g * rows + (j + 1) * n_new]
        psum.append(acc)
    psum = stack(psum)
    hi = _bf(psum)
    lo = _bf(psum - hi.astype(f32))
    ov = ov_ref[...]
    imp = jnp.dot(hi, ov, preferred_element_type=f32) + jnp.dot(lo, ov, preferred_element_type=f32)
    cur = lax.shift_right_logical(pos_t, 6)
    forced = (lane == 0) | (lane == cur) | (lane == cur - 1)
    future = lane * SLC_LEN > pos_t
    score = jnp.where(future, NEG, jnp.where(forced, BIG, imp))
    score = jnp.where(lane >= n_slc, REMOVED, score)
    lane_f = lax.broadcasted_iota(jnp.int32, (sel_rows, LANES), 1).astype(f32)
    sel = jnp.zeros((sel_rows, LANES), f32)
    for _ in range(min(N_SELECT, n_slc)):
        mx = jnp.max(score, axis=-1, keepdims=True)
        idx = jnp.min(jnp.where(score == mx, lane_f, float(LANES)), axis=-1, keepdims=True)
        hit = lane_f == idx
        sel = jnp.where(hit, 1.0, sel)
        score = jnp.where(hit, REMOVED, score)
    sel_r = stack([sel[g * n_new:(g + 1) * n_new] for g in groups for _ in range(HEADS_PER_GROUP)])
    expand = jnp.where(lax.broadcasted_iota(jnp.int32, (LANES, past), 0)
                       == lax.shift_right_logical(lax.broadcasted_iota(jnp.int32, (LANES, past), 1), 6),
                       1.0, 0.0).astype(jnp.bfloat16)
    picked = jnp.dot(_bf(sel_r), expand, preferred_element_type=f32) > 0.5
    s_all = stack([jnp.concatenate([lax.dot_general(qr[g], _bf(ksp[pg][grp_rows(g, PAGE), :]), _NT,
                                                    preferred_element_type=f32)
                                    for pg in range(n_pages)], axis=1) for g in groups]) * scale
    s_all = jnp.where(picked, s_all, NEG)
    new_sel = sel_r[:, new_blk:new_blk + 1] > 0.5
    ok_new = [new_sel & (t_row >= i) for i in range(n_new)]
    s_new = [jnp.where(ok, sc, NEG) for ok, sc in zip(ok_new, tail_scores(ksn_ref))]
    p_all, p_new = _softmax_with_tail(s_all, picked, s_new, ok_new)
    p_all = _bf(p_all)
    for g in groups:
        acc = tail_values(p_new, vsn_ref, g)
        for pg in range(n_pages):
            acc = acc + jnp.dot(p_all[g * rows:(g + 1) * rows, pg * PAGE:(pg + 1) * PAGE],
                                _bf(vsp[pg][grp_rows(g, PAGE), :]), preferred_element_type=f32)
        os_ref[g] = acc
    mask_w = lax.broadcasted_iota(jnp.int32, (1, win_keep), 1) >= t_row
    s_w = stack([lax.dot_general(qr[g], _bf(kwc_ref[grp_rows(g, win_keep), :]), _NT, preferred_element_type=f32)
                 for g in groups]) * scale
    s_w = jnp.where(mask_w, s_w, NEG)
    ok_wn = [t_row >= i for i in range(n_new)]
    s_wn = [jnp.where(ok, sc, NEG) for ok, sc in zip(ok_wn, tail_scores(kwn_ref))]
    p_w, p_wn = _softmax_with_tail(s_w, mask_w, s_wn, ok_wn)
    p_w = _bf(p_w)
    for g in groups:
        ow_ref[g] = (tail_values(p_wn, vwn_ref, g)
                     + jnp.dot(p_w[g * rows:(g + 1) * rows], _bf(vwc_ref[grp_rows(g, win_keep), :]),
                               preferred_element_type=f32))
    new_rows = n_new * NB_KV
    kept_rows = win_keep * NB_KV - new_rows
    for src_ref, add_ref, dst_ref in ((kwc_ref, kwn_ref, kwo_ref), (vwc_ref, vwn_ref, vwo_ref)):
        dst_ref[0:kept_rows, :] = src_ref[new_rows:new_rows + kept_rows, :]
        dst_ref[kept_rows:kept_rows + new_rows, :] = add_ref[...]


def _nsa_sample(page_table, q, qr, kcb, vcb, cache_sk, cache_sv, ks_new, vs_new, cache_wk, cache_wv, kw_new, vw_new):
    nb, n_pages = page_table.shape
    t = q.shape[1]
    n_phys = cache_sk.shape[0]
    keep = cache_wk.shape[1]
    assert keep == WINDOW and (t * NB_KV) % 8 == 0 and t < keep
    rows = HEADS_PER_GROUP * t
    n_sub = n_pages * SUBS_PER_PAGE

    def q_rows(a):
        a = a.reshape(nb, t, NB_KV, HEADS_PER_GROUP, NB_DH)
        return _bf(jnp.transpose(a, (0, 2, 3, 1, 4)).reshape(nb, NB_KV, rows, NB_DH))

    def q_unrows(a):
        a = a.reshape(nb, NB_KV, HEADS_PER_GROUP, t, NB_DH)
        return jnp.transpose(a, (0, 3, 1, 2, 4)).reshape(nb, t, NB_HEADS, NB_DH)

    def per_b(*shape):
        return pl.BlockSpec((None,) + shape, lambda b, pt: (b,) + (0,) * len(shape))

    def rows_view(a):
        return a.reshape(a.shape[0], a.shape[1] * NB_KV, NB_DH)

    n_cmp = n_sub - CMP_LEN // CMP_STRIDE + 1
    q_spec = per_b(NB_KV, rows, NB_DH)
    win_spec = per_b(keep * NB_KV, NB_DH)
    new_spec = per_b(t * NB_KV, NB_DH)
    cmp_spec = per_b(NB_KV * n_sub, NB_DH)
    in_specs = (_page_specs(n_pages) * 2
                + [q_spec, q_spec, cmp_spec, cmp_spec, new_spec, new_spec, win_spec, win_spec, new_spec, new_spec,
                   pl.BlockSpec((LANES, LANES), lambda b, pt: (0, 0))])
    grid_spec = pltpu.PrefetchScalarGridSpec(
        num_scalar_prefetch=1, grid=(nb,), in_specs=in_specs, out_specs=[q_spec] * 3 + [win_spec] * 2)
    sk = cache_sk.reshape(n_phys * PAGE * NB_KV, NB_DH)
    sv = cache_sv.reshape(n_phys * PAGE * NB_KV, NB_DH)
    o_c, o_s, o_w, wk_next, wv_next = pl.pallas_call(
        functools.partial(_nsa_sample_kernel, n_pages=n_pages, n_new=t, win_keep=keep),
        grid_spec=grid_spec,
        out_shape=([jax.ShapeDtypeStruct((nb, NB_KV, rows, NB_DH), jnp.float32)] * 3
                   + [jax.ShapeDtypeStruct((nb, keep * NB_KV, NB_DH), jnp.float32)] * 2),
        compiler_params=pltpu.CompilerParams(dimension_semantics=("arbitrary",), vmem_limit_bytes=VMEM_LIMIT),
        name="nsa_sample",
    )(page_table, *([sk] * n_pages), *([sv] * n_pages), q_rows(q), q_rows(qr), kcb, vcb,
      rows_view(ks_new), rows_view(vs_new), rows_view(cache_wk), rows_view(cache_wv), rows_view(kw_new),
      rows_view(vw_new), _overlap_matrix(LANES, n_cmp, LANES))
    return (q_unrows(o_c), q_unrows(o_s), q_unrows(o_w),
            wk_next.reshape(cache_wk.shape), wv_next.reshape(cache_wv.shape))


def _mm_kernel(x_ref, w_ref, o_ref, wb_ref):
    @pl.when(pl.program_id(1) == 0)
    def _():
        wb_ref[...] = w_ref[...].astype(jnp.bfloat16)

    o_ref[...] = jnp.dot(x_ref[...], wb_ref[...], preferred_element_type=jnp.float32)


def _mm_tiles(m, k, n):
    if k <= 2048:
        tm = 1024
        tn = 1024 if (n % 1024 == 0 or n >= 4096) else 512
    else:
        tm, tn = 512, 512
    return min(tm, m), min(tn, n)


def _mm(x, w):
    lead = x.shape[:-1]
    k = x.shape[-1]
    n = w.shape[1]
    x2 = _bf(x.reshape(-1, k))
    m = x2.shape[0]
    tm, tn = _mm_tiles(m, k, n)
    assert m % tm == 0
    out = pl.pallas_call(
        _mm_kernel,
        grid=(pl.cdiv(n, tn), m // tm),
        in_specs=[pl.BlockSpec((tm, k), lambda j, i: (i, 0)),
                  pl.BlockSpec((k, tn), lambda j, i: (0, j))],
        out_specs=pl.BlockSpec((tm, tn), lambda j, i: (i, j)),
        out_shape=jax.ShapeDtypeStruct((m, n), jnp.float32),
        scratch_shapes=[pltpu.VMEM((k, tn), jnp.bfloat16)],
        compiler_params=pltpu.CompilerParams(
            dimension_semantics=("arbitrary", "arbitrary"), vmem_limit_bytes=VMEM_LIMIT),
        name="mm",
    )(x2, w)
    return out.reshape(*lead, n)


def _mm_res_kernel(x_ref, w_ref, r_ref, o_ref, wb_ref):
    @pl.when(pl.program_id(1) == 0)
    def _():
        wb_ref[...] = _bf(w_ref[...])

    o_ref[...] = r_ref[...] + jnp.dot(x_ref[...], wb_ref[...], preferred_element_type=jnp.float32)


def _mm_res(x, w, resid):
    lead = x.shape[:-1]
    k = x.shape[-1]
    n = w.shape[1]
    x2 = _bf(x.reshape(-1, k))
    r2 = resid.reshape(-1, n)
    m = x2.shape[0]
    tm, tn = _mm_tiles(m, k, n)
    assert m % tm == 0 and r2.shape == (m, n)
    tile = pl.BlockSpec((tm, tn), lambda j, i: (i, j))
    out = pl.pallas_call(
        _mm_res_kernel,
        grid=(pl.cdiv(n, tn), m // tm),
        in_specs=[pl.BlockSpec((tm, k), lambda j, i: (i, 0)), pl.BlockSpec((k, tn), lambda j, i: (0, j)), tile],
        out_specs=tile,
        out_shape=jax.ShapeDtypeStruct((m, n), jnp.float32),
        scratch_shapes=[pltpu.VMEM((k, tn), jnp.bfloat16)],
        compiler_params=pltpu.CompilerParams(
            dimension_semantics=("arbitrary", "arbitrary"), vmem_limit_bytes=VMEM_LIMIT),
        name="mm_res",
    )(x2, w, r2)
    return out.reshape(*lead, n)


def _ple_kernel(hn_ref, wg_ref, p_ref, wp_ref, h_ref, o_ref, wgb_ref, wpb_ref):
    @pl.when(pl.program_id(1) == 0)
    def _():
        wgb_ref[...] = _bf(wg_ref[...])
        wpb_ref[...] = _bf(wp_ref[...])

    gate = jax.nn.sigmoid(jnp.dot(hn_ref[...], wgb_ref[...], preferred_element_type=jnp.float32))
    o_ref[...] = h_ref[...] + jnp.dot(p_ref[...], wpb_ref[...], preferred_element_type=jnp.float32) * gate


def _ple_fused(h, hn, p, w_p, w_g):
    lead = h.shape[:-1]
    d = h.shape[-1]
    kp = p.shape[-1]
    h2 = h.reshape(-1, d)
    m = h2.shape[0]
    tm, tn = min(m, PLE_TM), PLE_TN
    assert m % tm == 0 and d % tn == 0 and w_g.shape == (d, d) and w_p.shape == (kp, d)
    tile = pl.BlockSpec((tm, tn), lambda j, i: (i, j))
    out = pl.pallas_call(
        _ple_kernel,
        grid=(d // tn, m // tm),
        in_specs=[pl.BlockSpec((tm, d), lambda j, i: (i, 0)), pl.BlockSpec((d, tn), lambda j, i: (0, j)),
                  pl.BlockSpec((tm, kp), lambda j, i: (i, 0)), pl.BlockSpec((kp, tn), lambda j, i: (0, j)), tile],
        out_specs=tile,
        out_shape=jax.ShapeDtypeStruct((m, d), jnp.float32),
        scratch_shapes=[pltpu.VMEM((d, tn), jnp.bfloat16), pltpu.VMEM((kp, tn), jnp.bfloat16)],
        compiler_params=pltpu.CompilerParams(
            dimension_semantics=("arbitrary", "arbitrary"), vmem_limit_bytes=VMEM_LIMIT),
        name="ple",
    )(_bf(hn.reshape(-1, d)), w_g, _bf(p.reshape(-1, kp)), w_p, h2)
    return out.reshape(*lead, d)


def _ffn_up_kernel(x_ref, wu_ref, wg_ref, cw_ref, cb_ref, prev_ref, act_ref, state_ref, wub_ref, wgb_ref, halo_ref):
    m = pl.program_id(1)
    tm = x_ref.shape[0]
    f32 = jnp.float32

    @pl.when(m == 0)
    def _():
        wub_ref[...] = _bf(wu_ref[...])
        wgb_ref[...] = _bf(wg_ref[...])
        halo_ref[0:CONV_W - 1, :] = prev_ref[...].astype(f32)

    x = x_ref[...]
    u = jnp.dot(x, wub_ref[...], preferred_element_type=f32)
    g = jnp.dot(x, wgb_ref[...], preferred_element_type=f32)
    row = lax.broadcasted_iota(jnp.int32, (tm, 1), 0)
    before1 = halo_ref[1:2, :]
    before2 = halo_ref[0:1, :]
    g1 = jnp.where(row == 0, before1, pltpu.roll(g, 1, 0))
    g2 = pltpu.roll(g, 2, 0)
    g2 = jnp.where(row == 0, before2, jnp.where(row == 1, before1, g2))
    conv = cb_ref[...] + cw_ref[0:1, :] * g2 + cw_ref[1:2, :] * g1 + cw_ref[2:3, :] * g
    act_ref[...] = (jax.nn.silu(conv) * u).astype(act_ref.dtype)
    tail = g[tm - (CONV_W - 1):tm, :]
    halo_ref[0:CONV_W - 1, :] = tail
    state_ref[...] = tail


def _ffn_up_prompt(x, w_up, conv_w, conv_b, prev):
    m, k = x.shape
    assert CONV_W == 3 and m % FFN_TM == 0 and w_up.shape == (k, 2 * D_FF)
    w_u, w_g = w_up[:, :D_FF], w_up[:, D_FF:]

    def col(r):
        return pl.BlockSpec((r, FFN_TN), lambda j, i: (0, j))

    return pl.pallas_call(
        _ffn_up_kernel,
        grid=(pl.cdiv(D_FF, FFN_TN), m // FFN_TM),
        in_specs=[pl.BlockSpec((FFN_TM, k), lambda j, i: (i, 0)), col(k), col(k), col(CONV_W), col(1), col(CONV_W - 1)],
        out_specs=[pl.BlockSpec((FFN_TM, FFN_TN), lambda j, i: (i, j)), col(CONV_W - 1)],
        out_shape=[jax.ShapeDtypeStruct((m, D_FF), jnp.bfloat16),
                   jax.ShapeDtypeStruct((CONV_W - 1, D_FF), jnp.float32)],
        scratch_shapes=[pltpu.VMEM((k, FFN_TN), jnp.bfloat16), pltpu.VMEM((k, FFN_TN), jnp.bfloat16),
                        pltpu.VMEM((8, FFN_TN), jnp.float32)],
        compiler_params=pltpu.CompilerParams(
            dimension_semantics=("arbitrary", "arbitrary"), vmem_limit_bytes=VMEM_LIMIT),
        name="ffn_up_prompt",
    )(_bf(x), w_u, w_g, conv_w, conv_b.reshape(1, D_FF), prev)


def _rmsnorm(x, g):
    xf = x.astype(jnp.float32)
    y = xf * lax.rsqrt(jnp.mean(xf * xf, axis=-1, keepdims=True) + EPS)
    return (y * g.astype(jnp.float32)).astype(x.dtype)


def _rope(x, pos):
    half = x.shape[-1] // 2
    inv = ROPE_THETA ** (-jnp.arange(half, dtype=jnp.float32) / half)
    ang = pos.astype(jnp.float32)[:, None] * inv[None, :]
    cos, sin = jnp.cos(ang)[:, None, :], jnp.sin(ang)[:, None, :]
    xf = x.astype(jnp.float32)
    x1, x2 = xf[..., :half], xf[..., half:]
    return jnp.concatenate([x1 * cos - x2 * sin, x2 * cos + x1 * sin], axis=-1).astype(x.dtype)


def _gla_chunked(q, k, v, logf, s0):
    b, t, h, dk = q.shape
    dv = v.shape[-1]
    c = math.gcd(t, HA_CHUNK)
    n = t // c

    def to_chunks(a):
        return jnp.transpose(a.reshape(b, n, c, h, a.shape[-1]), (1, 0, 3, 2, 4))

    qc, kc, vc, gc = to_chunks(q), to_chunks(k), to_chunks(v), to_chunks(logf)
    bcum = jnp.cumsum(gc, axis=3)
    blast = bcum[:, :, :, -1, :]
    q_in = qc * jnp.exp(bcum)
    k_in = kc * jnp.exp(-bcum)
    k_dec = kc * jnp.exp(blast[:, :, :, None, :] - bcum)
    causal = jnp.tril(jnp.ones((c, c), dtype=bool))
    att = jnp.where(causal, jnp.einsum('nbhik,nbhjk->nbhij', q_in, k_in), 0.0)
    o_intra = jnp.einsum('nbhij,nbhjv->nbhiv', att, vc)

    def step(s, inp):
        qi, kd, vi, dl = inp
        o_inter = jnp.einsum('bhik,bhkv->bhiv', qi, s)
        s = jnp.exp(dl)[..., None] * s + jnp.einsum('bhik,bhiv->bhkv', kd, vi)
        return s, o_inter

    s_final, o_inter = lax.scan(step, s0.astype(jnp.float32), (q_in, k_dec, vc, blast))
    o = o_intra + o_inter
    return jnp.transpose(o, (1, 0, 3, 2, 4)).reshape(b, t, h, dv), s_final


def _hgrn2_layer(xn, s0, lb, w_in, out_gain, w_out, resid):
    b, t, _ = xn.shape
    hk = HA_HEADS * HA_DK
    hv = HA_HEADS * HA_DV
    proj = _mm(xn, w_in)
    if s0 is None:
        assert b == 1
        o, s_new = _gla_prompt(proj[0], lb, out_gain)
        return _mm_res(o[None], w_out, resid), s_new[None]
    q = jax.nn.silu(proj[..., :hk])
    f = lb + (1.0 - lb) * jax.nn.sigmoid(proj[..., hk:2 * hk])
    i_in = proj[..., 2 * hk:2 * hk + hv]
    gate = proj[..., 2 * hk + hv:]
    shp_k = (b, t, HA_HEADS, HA_DK)
    shp_v = (b, t, HA_HEADS, HA_DV)
    o, s_new = _gla_chunked(q.reshape(shp_k), (1.0 - f).reshape(shp_k), i_in.reshape(shp_v),
                            jnp.log(f).reshape(shp_k), s0)
    o = _rmsnorm(o, out_gain.reshape(HA_HEADS, HA_DV)) * jax.nn.silu(gate).reshape(shp_v)
    return _mm_res(o.reshape(b, t, hv), w_out, resid), s_new


def _conv_ffn(xn, prev, w_up, conv_w, conv_b, w_down, resid):
    b, t, _ = xn.shape
    if b == 1 and t % FFN_TM == 0:
        act, state = _ffn_up_prompt(xn[0], w_up, conv_w, conv_b, prev[0])
        return _mm_res(act[None], w_down, resid), state[None].astype(xn.dtype)
    up = _mm(xn, w_up)
    u, gt = up[..., :D_FF], up[..., D_FF:]
    ext = jnp.concatenate([prev.astype(gt.dtype), gt], axis=1)
    conv = conv_b
    for j in range(CONV_W):
        conv = conv + conv_w[j] * ext[:, j:j + t]
    return _mm_res(jax.nn.silu(conv) * u, w_down, resid), ext[:, t:]


def _ple(h, p, gain, w_p, w_g):
    return _ple_fused(h, _rmsnorm(h, gain), p, w_p, w_g)


def _compress(rows, pe, w1, w2):
    b, l, g, d = rows.shape
    n_sub = l // CMP_STRIDE
    r = CMP_LEN // CMP_STRIDE
    n_cmp = n_sub - r + 1
    sub = rows[:, :n_sub * CMP_STRIDE].reshape(b, n_sub, CMP_STRIDE, g, d)
    pre = None
    for j in range(r):
        sl = slice(j * CMP_STRIDE, (j + 1) * CMP_STRIDE)
        part = jnp.einsum('bnsgd,sde->bnge', sub + pe[sl][None, None, :, None, :], w1[sl])[:, j:j + n_cmp]
        pre = part if pre is None else pre + part
    return jnp.einsum('bnge,ef->bngf', jax.nn.gelu(pre), w2)


def _nsa_layer(xn, pos, shared, w_in, q_gain, w_out, banded, resid):
    b, t, _ = xn.shape
    hd = NB_HEADS * NB_DH
    q = _rmsnorm(_mm(xn, w_in[:, :hd]).reshape(b, t, NB_HEADS, NB_DH), q_gain)
    gates = jax.nn.sigmoid(_mm(xn, w_in[:, hd:])).reshape(b, t, 3, NB_HEADS)
    qr = _rope(q, pos)
    if banded:
        assert b == 1
        kcb, vcb, kpad, vpad, win_k, win_v = shared
        oc_gated, sel = _cmp_topk_prompt(q.reshape(t, hd), kcb[0], vcb[0], gates[0, :, 0, :])
        o = _slc_win_prompt(qr.reshape(t, hd), kpad[0], vpad[0], win_k[0], win_v[0], sel, oc_gated,
                            gates[0, :, 1:3, :])
        return _mm_res(o[None], w_out, resid), None
    o_c, o_s, o_w, wk_next, wv_next = _nsa_sample(shared[0], q, qr, *shared[1:])
    o = (gates[:, :, 0, :, None] * o_c + gates[:, :, 1, :, None] * o_s + gates[:, :, 2, :, None] * o_w)
    return _mm_res(o.reshape(b, t, hd), w_out, resid), (wk_next, wv_next)


def _shared_kv(h, pos, past, wt):
    b, t, _ = h.shape
    kv = _mm(_rmsnorm(h, wt['kv_norm']), wt['kv_w']).reshape(b, t, 6, NB_KV, NB_DH)
    k_cmp, v_cmp = kv[:, :, 0], kv[:, :, 1]
    k_slc = _rope(_rmsnorm(kv[:, :, 2], wt['k_norm_slc']), pos)
    v_slc = kv[:, :, 3]
    k_win = _rope(_rmsnorm(kv[:, :, 4], wt['k_norm_win']), pos)
    v_win = kv[:, :, 5]
    if past is None:
        keep = min(WINDOW, t)
        kcb = _rmsnorm(_compress(k_cmp, wt['cmp_pe_k'], wt['cmp_w1_k'], wt['cmp_w2_k']), wt['k_norm_cmp'])
        vcb = _compress(v_cmp, wt['cmp_pe_v'], wt['cmp_w1_v'], wt['cmp_w2_v'])
        assert t % SLC_LEN == 0
        shared = (kcb, vcb, k_slc, v_slc, k_win, v_win)
        rows = (k_cmp, v_cmp, k_slc, v_slc, k_win[:, -keep:], v_win[:, -keep:])
    else:
        page_table, ckc, cvc, cks, cvs, cwk, cwv = past
        assert t < CMP_STRIDE and (page_table.shape[1] * PAGE) % CMP_STRIDE == 0
        kcb, vcb = _cmp_pages(page_table, ckc, cvc, wt['cmp_pe_k'], wt['cmp_pe_v'], wt['cmp_w1_k'], wt['cmp_w1_v'],
                              wt['cmp_w2_k'], wt['cmp_w2_v'], wt['k_norm_cmp'])
        shared = (page_table, kcb, vcb, cks, cvs, k_slc, v_slc, cwk, cwv, k_win, v_win)
        rows = (k_cmp, v_cmp, k_slc, v_slc, None, None)
    return shared, rows


def _trunk(x, p, start, s_hgrn, s_conv, past, wt):
    b, t, _ = x.shape
    pos = start + jnp.arange(t)
    banded = past is None
    lb = jnp.cumsum(jax.nn.softmax(wt['a_lb_logits'].astype(jnp.float32), axis=0), axis=0)
    h = x
    hgrn_out, conv_out = [], []
    shared, rows = None, None
    for i in range(DEPTH):
        hn = _rmsnorm(h, wt['norm_mix'][i])
        if i < N_A:
            h, s = _hgrn2_layer(hn, None if s_hgrn is None else s_hgrn[i], lb[i], wt['a_w_in'][i],
                                wt['a_out_norm'][i], wt['a_w_out'][i], h)
            hgrn_out.append(s)
        else:
            j = i - N_A
            h, win_next = _nsa_layer(hn, pos, shared, wt['b_w_in'][j], wt['b_q_norm'][j], wt['b_w_out'][j],
                                     banded, h)
            if win_next is not None:
                assert N_B == 1
                rows = rows[:4] + win_next
        h, cs = _conv_ffn(_rmsnorm(h, wt['norm_ffn'][i]), s_conv[i], wt['ffn_w_up'][i], wt['ffn_conv_w'][i],
                          wt['ffn_conv_b'][i], wt['ffn_w_down'][i], h)
        conv_out.append(cs)
        h = _ple(h, p[i], wt['norm_ple'][i], wt['ple_w_p'][i], wt['ple_w_g'][i])
        if i == N_A - 1:
            shared, rows = _shared_kv(h, pos, past, wt)
    return h, jnp.stack(hgrn_out), jnp.stack(conv_out), rows


def kernel(x_prompt, x_sample, state_hgrn, state_conv, cache_cmp_k, cache_cmp_v, cache_slc_k, cache_slc_v,
           cache_win_k, cache_win_v, page_table, p_prompt, p_sample, norm_mix, norm_ffn, norm_ple,
           a_w_in, a_lb_logits, a_out_norm, a_w_out, kv_norm, kv_w, k_norm_cmp, k_norm_slc, k_norm_win,
           cmp_pe_k, cmp_pe_v, cmp_w1_k, cmp_w2_k, cmp_w1_v, cmp_w2_v, b_w_in, b_q_norm, b_w_out,
           ffn_w_up, ffn_conv_w, ffn_conv_b, ffn_w_down, ple_w_p, ple_w_g):
    wt = {
        'norm_mix': norm_mix, 'norm_ffn': norm_ffn, 'norm_ple': norm_ple,
        'a_w_in': a_w_in, 'a_lb_logits': a_lb_logits, 'a_out_norm': a_out_norm, 'a_w_out': a_w_out,
        'kv_norm': kv_norm, 'kv_w': kv_w, 'k_norm_cmp': k_norm_cmp, 'k_norm_slc': k_norm_slc,
        'k_norm_win': k_norm_win, 'cmp_pe_k': cmp_pe_k, 'cmp_pe_v': cmp_pe_v, 'cmp_w1_k': cmp_w1_k,
        'cmp_w2_k': cmp_w2_k, 'cmp_w1_v': cmp_w1_v, 'cmp_w2_v': cmp_w2_v,
        'b_w_in': b_w_in, 'b_q_norm': b_q_norm, 'b_w_out': b_w_out,
        'ffn_w_up': ffn_w_up, 'ffn_conv_w': ffn_conv_w, 'ffn_conv_b': ffn_conv_b, 'ffn_w_down': ffn_w_down,
        'ple_w_p': ple_w_p, 'ple_w_g': ple_w_g,
    }
    bp, past_len = x_prompt.shape[0], page_table.shape[1] * cache_cmp_k.shape[1]
    s_conv0 = jnp.zeros((DEPTH, bp, CONV_W - 1, D_FF), x_prompt.dtype)
    y_prompt, hgrn_p, conv_p, rows_p = _trunk(x_prompt, p_prompt, 0, None, s_conv0, None, wt)
    past = (page_table, cache_cmp_k, cache_cmp_v, cache_slc_k, cache_slc_v, cache_win_k, cache_win_v)
    y_sample, hgrn_s, conv_s, rows_s = _trunk(x_sample, p_sample, past_len, state_hgrn, state_conv, past, wt)
    kc_p, vc_p, ks_p, vs_p, wk_p, wv_p = rows_p
    kc_s, vc_s, ks_s, vs_s, wk_s, wv_s = rows_s
    return (y_prompt, y_sample, hgrn_p, hgrn_s, conv_p, conv_s, kc_p, vc_p, ks_p, vs_p,
            kc_s, vc_s, ks_s, vs_s, wk_p, wv_p, wk_s, wv_s)
```

```python
import functools
import math

import jax
import jax.numpy as jnp
from jax import lax
from jax.experimental import pallas as pl
from jax.experimental.pallas import tpu as pltpu

D_MODEL = 2048
DEPTH = 2
N_A = DEPTH // 2
N_B = DEPTH - N_A
EPS = 1e-6
HA_DK = 128
HA_HEADS = D_MODEL // HA_DK
HA_DV = D_MODEL // HA_HEADS
HA_CHUNK = 32
NB_DH = 128
NB_HEADS = D_MODEL // NB_DH
NB_KV = 4
HEADS_PER_GROUP = NB_HEADS // NB_KV
CMP_LEN = 32
CMP_STRIDE = 16
SLC_LEN = 64
N_SELECT = 16
WINDOW = 512
Q_BLOCK = 128
ROPE_THETA = 10000.0
D_FF = ((8 * D_MODEL // 3 + 127) // 128) * 128
CONV_W = 3
NEG = -1e30
BIG = 1e30
LOG2_E = math.log2(math.e)
REMOVED = -3.0e38
LANES = 128
SLC_KV_TILE = 1024
WIN_SPAN = WINDOW + Q_BLOCK
GLA_ROWS = 256
GLA_HEADS = 4
PAGE = 128
SUBS_PER_PAGE = PAGE // CMP_STRIDE
SUB_ROWS = CMP_STRIDE * NB_KV
SUB_PITCH = SUB_ROWS + 8
PLE_TM = 1024
PLE_TN = 512
FFN_TM = 1024
FFN_TN = 512
SLC_GROUPS = 2
VMEM_LIMIT = 48 * 1024 * 1024
SLC_VMEM_LIMIT = 56 * 1024 * 1024
_NT = (((1,), (1,)), ((), ()))
_TN = (((0,), (0,)), ((), ()))


def _bf(x):
    return x.astype(jnp.bfloat16)


def _cmp_topk_kernel(q_ref, k_ref, v_ref, ov_ref, gate_ref, o_ref, sel_ref, *, q_blk, n_cmp_pad, n_slc, n_sel):
    i = pl.program_id(0)
    f32 = jnp.float32
    pos = i * q_blk + lax.broadcasted_iota(jnp.int32, (q_blk, 1), 0)
    kend = lax.broadcasted_iota(jnp.int32, (1, n_cmp_pad), 1) * CMP_STRIDE + (CMP_LEN - 1)
    mask_c = kend <= pos
    scale = NB_DH ** -0.5
    ov = ov_ref[...]
    imp = []
    for g in range(NB_KV):
        k = k_ref[g]
        v = v_ref[g]
        psum = jnp.zeros((q_blk, n_cmp_pad), f32)
        for j in range(HEADS_PER_GROUP):
            cs = slice((g * HEADS_PER_GROUP + j) * NB_DH, (g * HEADS_PER_GROUP + j + 1) * NB_DH)
            s = lax.dot_general(_bf(q_ref[:, cs]), k, _NT, preferred_element_type=f32) * scale
            s = jnp.where(mask_c, s, NEG)
            m = jnp.max(s, axis=-1, keepdims=True)
            e = jnp.where(mask_c, jnp.exp(s - m), 0.0)
            p = e / jnp.maximum(jnp.sum(e, axis=-1, keepdims=True), 1e-30)
            h = g * HEADS_PER_GROUP + j
            o_ref[:, cs] = gate_ref[:, h:h + 1] * jnp.dot(_bf(p), v, preferred_element_type=f32)
            psum = psum + p
        hi = _bf(psum)
        lo = _bf(psum - hi.astype(f32))
        imp.append(jnp.dot(hi, ov, preferred_element_type=f32) + jnp.dot(lo, ov, preferred_element_type=f32))
    imp = jnp.concatenate(imp, axis=0)
    pos_r = jnp.concatenate([pos] * NB_KV, axis=0)
    sidx = lax.broadcasted_iota(jnp.int32, (1, n_slc), 1)
    cur = lax.shift_right_logical(pos_r, 6)
    forced = (sidx == 0) | (sidx == cur) | (sidx == cur - 1)
    future = sidx * SLC_LEN > pos_r
    score = jnp.where(future, NEG, jnp.where(forced, BIG, imp))
    lane = lax.broadcasted_iota(jnp.int32, (NB_KV * q_blk, n_slc), 1).astype(f32)
    sel = jnp.zeros((NB_KV * q_blk, n_slc), f32)
    for _ in range(n_sel):
        m = jnp.max(score, axis=-1, keepdims=True)
        idx = jnp.min(jnp.where(score == m, lane, float(n_slc)), axis=-1, keepdims=True)
        hit = lane == idx
        sel = jnp.where(hit, 1.0, sel)
        score = jnp.where(hit, REMOVED, score)
    for g in range(NB_KV):
        sel_ref[:, g * n_slc:(g + 1) * n_slc] = sel[g * q_blk:(g + 1) * q_blk].astype(sel_ref.dtype)


def _overlap_matrix(n_cmp_rows, n_cmp, n_slc_cols):
    cstart = jnp.arange(n_cmp_rows) * CMP_STRIDE
    sstart = jnp.arange(n_slc_cols) * SLC_LEN
    return _bf((cstart[:, None] <= sstart[None, :] + (SLC_LEN - 1))
               & (cstart[:, None] + (CMP_LEN - 1) >= sstart[None, :])
               & (jnp.arange(n_cmp_rows)[:, None] < n_cmp))


def _cmp_topk_prompt(q2d, kcb, vcb, gate_c):
    t = q2d.shape[0]
    n_cmp = kcb.shape[0]
    n_cmp_pad = -(-n_cmp // LANES) * LANES
    n_slc = t // SLC_LEN
    assert SLC_LEN == 64 and n_slc == LANES and t % Q_BLOCK == 0
    n_sel = min(N_SELECT, n_slc)
    padc = ((0, 0), (0, n_cmp_pad - n_cmp), (0, 0))
    k = _bf(jnp.pad(jnp.transpose(kcb, (1, 0, 2)), padc))
    v = _bf(jnp.pad(jnp.transpose(vcb, (1, 0, 2)), padc))
    hd = NB_HEADS * NB_DH
    kern = functools.partial(_cmp_topk_kernel, q_blk=Q_BLOCK, n_cmp_pad=n_cmp_pad, n_slc=n_slc, n_sel=n_sel)
    kv_spec = pl.BlockSpec((NB_KV, n_cmp_pad, NB_DH), lambda i: (0, 0, 0))
    return pl.pallas_call(
        kern,
        grid=(t // Q_BLOCK,),
        in_specs=[pl.BlockSpec((Q_BLOCK, hd), lambda i: (i, 0)), kv_spec, kv_spec,
                  pl.BlockSpec((n_cmp_pad, n_slc), lambda i: (0, 0)),
                  pl.BlockSpec((Q_BLOCK, NB_HEADS), lambda i: (i, 0))],
        out_specs=[pl.BlockSpec((Q_BLOCK, hd), lambda i: (i, 0)),
                   pl.BlockSpec((Q_BLOCK, NB_KV * n_slc), lambda i: (i, 0))],
        out_shape=[jax.ShapeDtypeStruct((t, hd), jnp.float32),
                   jax.ShapeDtypeStruct((t, NB_KV * n_slc), jnp.bfloat16)],
        compiler_params=pltpu.CompilerParams(dimension_semantics=("arbitrary",), vmem_limit_bytes=VMEM_LIMIT),
        name="cmp_topk_prompt",
    )(q2d, k, v, _overlap_matrix(n_cmp_pad, n_cmp, n_slc), gate_c)


def _slc_win_kernel(q_ref, ks_ref, vs_ref, kw_ref, vw_ref, sel_ref, oc_ref, gate_ref, o_ref, m_ref, l_ref, acc_ref,
                    *, q_blk, n_slc, kv_tile, n_grp):
    i = pl.program_id(1)
    rows = HEADS_PER_GROUP * q_blk
    gd = HEADS_PER_GROUP * NB_DH
    blocks_per_tile = kv_tile // SLC_LEN
    reps = kv_tile // LANES
    scale = NB_DH ** -0.5
    pos = i * q_blk + lax.broadcasted_iota(jnp.int32, (q_blk, 1), 0)
    q4 = [jnp.concatenate([q_ref[:, gg * gd + j * NB_DH:gg * gd + (j + 1) * NB_DH] for j in range(HEADS_PER_GROUP)],
                          axis=0) for gg in range(n_grp)]
    sel = [sel_ref[:, gg * n_slc:(gg + 1) * n_slc] for gg in range(n_grp)]
    m_ref[...] = jnp.full(m_ref.shape, NEG, jnp.float32)
    l_ref[...] = jnp.zeros(l_ref.shape, jnp.float32)
    acc_ref[...] = jnp.zeros(acc_ref.shape, jnp.float32)
    blk_row = lax.broadcasted_iota(jnp.int32, (n_slc, kv_tile), 0)
    blk_col = lax.shift_right_logical(lax.broadcasted_iota(jnp.int32, (n_slc, kv_tile), 1), 6)
    key_off = lax.broadcasted_iota(jnp.int32, (1, kv_tile), 1)
    n_tiles = lax.div(i * q_blk + (q_blk - 1), kv_tile) + 1

    def body(j, carry):
        start = pl.multiple_of(j * kv_tile, kv_tile)
        expand = _bf(jnp.where(blk_row == blk_col + j * blocks_per_tile, 1.0, 0.0))
        causal = key_off + start <= pos
        for gg in range(n_grp):
            kt = ks_ref[gg, pl.ds(start, kv_tile), :]
            vt = vs_ref[gg, pl.ds(start, kv_tile), :]
            s = lax.dot_general(q4[gg], kt, _NT, preferred_element_type=jnp.float32) * (scale * LOG2_E)
            picked = jnp.dot(sel[gg], expand, preferred_element_type=jnp.float32) > 0.5
            mask = picked & causal
            s = jnp.where(mask[None], s.reshape(HEADS_PER_GROUP, q_blk, kv_tile), NEG).reshape(rows, kv_tile)
            m_prev = m_ref[gg]
            m_new = jnp.maximum(m_prev, jnp.max(s, axis=-1, keepdims=True))
            alpha = jnp.exp2(m_prev - m_new)
            p = jnp.exp2(s - jnp.concatenate([m_new] * reps, axis=1))
            l_ref[gg] = alpha * l_ref[gg] + jnp.sum(p, axis=-1, keepdims=True)
            acc_ref[gg] = alpha * acc_ref[gg] + jnp.dot(_bf(p), vt, preferred_element_type=jnp.float32)
            m_ref[gg] = m_new
        return carry

    lax.fori_loop(0, n_tiles, body, 0)

    wstart = pl.multiple_of(jnp.maximum(i - WINDOW // q_blk, 0) * q_blk, q_blk)
    diff = pos - (wstart + lax.broadcasted_iota(jnp.int32, (1, WIN_SPAN), 1))
    mask = ((diff >= 0) & (diff <= WINDOW))[None]
    n_ghost = jnp.maximum(WINDOW - pos, 0).astype(jnp.float32)[None]
    heads = n_grp * HEADS_PER_GROUP
    for gg in range(n_grp):
        out = acc_ref[gg] / jnp.maximum(l_ref[gg], 1e-30)
        kt = kw_ref[gg, pl.ds(wstart, WIN_SPAN), :]
        vt = vw_ref[gg, pl.ds(wstart, WIN_SPAN), :]
        s = lax.dot_general(q4[gg], kt, _NT, preferred_element_type=jnp.float32) * scale
        s = jnp.where(mask, s.reshape(HEADS_PER_GROUP, q_blk, WIN_SPAN), NEG)
        m = jnp.max(s, axis=-1, keepdims=True)
        m = jnp.where(n_ghost > 0, jnp.maximum(m, 0.0), m)
        e = jnp.where(mask, jnp.exp(s - m), 0.0)
        p = e / jnp.maximum(jnp.sum(e, axis=-1, keepdims=True) + n_ghost * jnp.exp(-m), 1e-30)
        ow = jnp.dot(_bf(p.reshape(rows, WIN_SPAN)), vt, preferred_element_type=jnp.float32)
        for j in range(HEADS_PER_GROUP):
            cs = slice(gg * gd + j * NB_DH, gg * gd + (j + 1) * NB_DH)
            h = gg * HEADS_PER_GROUP + j
            rs = slice(j * q_blk, (j + 1) * q_blk)
            o_ref[:, cs] = (oc_ref[:, cs] + gate_ref[:, h:h + 1] * out[rs, :]
                            + gate_ref[:, heads + h:heads + h + 1] * ow[rs, :]).astype(o_ref.dtype)


def _slc_win_prompt(qr2d, kpad, vpad, kwin, vwin, sel, oc_gated, gate_sw):
    t = qr2d.shape[0]
    n_slc = t // SLC_LEN
    assert t % SLC_KV_TILE == 0 and SLC_KV_TILE % SLC_LEN == 0 and t >= WIN_SPAN and NB_KV % SLC_GROUPS == 0

    def group_major(a):
        return _bf(jnp.transpose(a, (1, 0, 2)))

    gd = SLC_GROUPS * HEADS_PER_GROUP * NB_DH
    rows = HEADS_PER_GROUP * Q_BLOCK
    kern = functools.partial(_slc_win_kernel, q_blk=Q_BLOCK, n_slc=n_slc, kv_tile=SLC_KV_TILE, n_grp=SLC_GROUPS)
    kv_spec = pl.BlockSpec((SLC_GROUPS, t, NB_DH), lambda g, i: (g, 0, 0))
    o_spec = pl.BlockSpec((Q_BLOCK, gd), lambda g, i: (i, g))
    n_gp = NB_KV // SLC_GROUPS
    heads = SLC_GROUPS * HEADS_PER_GROUP
    gate_blk = jnp.transpose(gate_sw.reshape(t, 2, n_gp, heads), (2, 0, 1, 3)).reshape(n_gp, t, 2 * heads)
    return pl.pallas_call(
        kern,
        grid=(n_gp, t // Q_BLOCK),
        in_specs=[o_spec, kv_spec, kv_spec, kv_spec, kv_spec,
                  pl.BlockSpec((Q_BLOCK, SLC_GROUPS * n_slc), lambda g, i: (i, g)), o_spec,
                  pl.BlockSpec((None, Q_BLOCK, 2 * heads), lambda g, i: (g, i, 0))],
        out_specs=o_spec,
        out_shape=jax.ShapeDtypeStruct((t, NB_HEADS * NB_DH), jnp.bfloat16),
        scratch_shapes=[
            pltpu.VMEM((SLC_GROUPS, rows, LANES), jnp.float32),
            pltpu.VMEM((SLC_GROUPS, rows, LANES), jnp.float32),
            pltpu.VMEM((SLC_GROUPS, rows, NB_DH), jnp.float32),
        ],
        compiler_params=pltpu.CompilerParams(
            dimension_semantics=("arbitrary", "arbitrary"), vmem_limit_bytes=SLC_VMEM_LIMIT),
        name="slc_win_prompt",
    )(_bf(qr2d), group_major(kpad), group_major(vpad), group_major(kwin), group_major(vwin), sel, oc_gated, gate_blk)


def _split3(x):
    hi = x.astype(jnp.bfloat16)
    r = x - hi.astype(jnp.float32)
    mid = r.astype(jnp.bfloat16)
    lo = (r - mid.astype(jnp.float32)).astype(jnp.bfloat16)
    return hi, mid, lo


def _gla_kernel(xq_ref, xf_ref, xv_ref, xg_ref, lb_ref, gain_ref, o_ref, sfin_ref, st_ref, *, rows, heads, chunk):
    i = pl.program_id(1)

    @pl.when(i == 0)
    def _():
        st_ref[...] = jnp.zeros(st_ref.shape, jnp.float32)

    r_idx = lax.broadcasted_iota(jnp.int32, (rows, rows), 0)
    c_idx = lax.broadcasted_iota(jnp.int32, (rows, rows), 1)
    same = lax.shift_right_logical(r_idx, 5) == lax.shift_right_logical(c_idx, 5)
    causal = same & (c_idx <= r_idx)
    tri = jnp.where(causal, 1.0, 0.0).astype(jnp.bfloat16)
    blk = jnp.where(same, 1.0, 0.0).astype(jnp.bfloat16)
    f32 = jnp.float32
    for hh in range(heads):
        sl = slice(hh * HA_DK, (hh + 1) * HA_DK)
        lb = lb_ref[:, sl]
        qh = jax.nn.silu(xq_ref[:, sl])
        f = lb + (1.0 - lb) * jax.nn.sigmoid(xf_ref[:, sl])
        kh = 1.0 - f
        gh = jnp.log(f)
        parts = _split3(gh)
        bcum = sum(jnp.dot(tri, p, preferred_element_type=f32) for p in parts)
        tot = sum(jnp.dot(blk, p, preferred_element_type=f32) for p in parts)
        q_in = (qh * jnp.exp(bcum)).astype(jnp.bfloat16)
        k_in = (kh * jnp.exp(-bcum)).astype(jnp.bfloat16)
        k_dec = (kh * jnp.exp(tot - bcum)).astype(jnp.bfloat16)
        vb = _bf(xv_ref[:, sl])
        att = lax.dot_general(q_in, k_in, _NT, preferred_element_type=f32)
        att = jnp.where(causal, att, 0.0).astype(jnp.bfloat16)
        o_intra = jnp.dot(att, vb, preferred_element_type=f32)
        st = st_ref[hh]
        o_inter = []
        for c in range(rows // chunk):
            rs = slice(c * chunk, (c + 1) * chunk)
            o_inter.append(lax.dot_general(q_in[rs], st.astype(jnp.bfloat16), _NT, preferred_element_type=f32))
            upd = lax.dot_general(vb[rs], k_dec[rs], _TN, preferred_element_type=f32)
            st = jnp.exp(tot[c * chunk:c * chunk + 1, :]) * st + upd
        st_ref[hh] = st
        o = o_intra + jnp.concatenate(o_inter, axis=0)
        o = o * lax.rsqrt(jnp.mean(o * o, axis=-1, keepdims=True) + EPS) * gain_ref[:, sl]
        o_ref[:, sl] = (o * jax.nn.silu(xg_ref[:, sl])).astype(o_ref.dtype)

    @pl.when(i == pl.num_programs(1) - 1)
    def _():
        for hh in range(heads):
            sfin_ref[hh] = st_ref[hh].T


def _gla_prompt(proj, lb, out_gain):
    t = proj.shape[0]
    hk = HA_HEADS * HA_DK
    assert proj.shape[1] == 4 * hk and HA_DK == HA_DV
    assert t % GLA_ROWS == 0 and GLA_ROWS % HA_CHUNK == 0 and HA_CHUNK == 32 and HA_HEADS % GLA_HEADS == 0
    w = GLA_HEADS * HA_DK
    n_hb = HA_HEADS // GLA_HEADS

    def section(c):
        return pl.BlockSpec((GLA_ROWS, w), functools.partial(lambda h, i, c: (i, c * n_hb + h), c=c))

    vec_spec = pl.BlockSpec((1, w), lambda h, i: (0, h))
    kern = functools.partial(_gla_kernel, rows=GLA_ROWS, heads=GLA_HEADS, chunk=HA_CHUNK)
    return pl.pallas_call(
        kern,
        grid=(n_hb, t // GLA_ROWS),
        in_specs=[section(0), section(1), section(2), section(3), vec_spec, vec_spec],
        out_specs=[pl.BlockSpec((GLA_ROWS, w), lambda h, i: (i, h)),
                   pl.BlockSpec((GLA_HEADS, HA_DK, HA_DV), lambda h, i: (h, 0, 0))],
        out_shape=[jax.ShapeDtypeStruct((t, hk), jnp.bfloat16),
                   jax.ShapeDtypeStruct((HA_HEADS, HA_DK, HA_DV), jnp.float32)],
        scratch_shapes=[pltpu.VMEM((GLA_HEADS, HA_DV, HA_DK), jnp.float32)],
        compiler_params=pltpu.CompilerParams(
            dimension_semantics=("arbitrary", "arbitrary"), vmem_limit_bytes=VMEM_LIMIT),
        name="gla_prompt",
    )(proj, proj, proj, proj, lb.reshape(1, hk), out_gain.reshape(1, hk))


def _page_specs(n_pages):
    return [pl.BlockSpec((PAGE * NB_KV, NB_DH), functools.partial(lambda b, pt, p: (pt[b, p], 0), p=p))
            for p in range(n_pages)]


def _cmp_pages_kernel(pt_ref, *refs, n_pages):
    kp = refs[:n_pages]
    vp = refs[n_pages:2 * n_pages]
    (w1k_ref, w1v_ref, pek_ref, pev_ref, w2k_ref, w2v_ref, gain_ref, ko_ref, vo_ref, x_ref, st_ref) = refs[2 * n_pages:]
    n_sub = n_pages * SUBS_PER_PAGE
    f32 = jnp.float32
    row = lax.broadcasted_iota(jnp.int32, (NB_KV * n_sub, 1), 0)
    last = lax.rem(row, n_sub) == n_sub - 1
    for pages, w1_ref, pe_ref, w2_ref, o_ref, norm in ((kp, w1k_ref, pek_ref, w2k_ref, ko_ref, True),
                                                       (vp, w1v_ref, pev_ref, w2v_ref, vo_ref, False)):
        for p in range(n_pages):
            for n in range(SUBS_PER_PAGE):
                st_ref[pl.ds((p * SUBS_PER_PAGE + n) * SUB_PITCH, SUB_ROWS), :] = pages[p][pl.ds(n * SUB_ROWS, SUB_ROWS), :]
        for g in range(NB_KV):
            for s in range(CMP_STRIDE):
                x_ref[g * n_sub:(g + 1) * n_sub, s * NB_DH:(s + 1) * NB_DH] = _bf(
                    st_ref[pl.ds(s * NB_KV + g, n_sub, stride=SUB_PITCH), :])
        w1 = w1_ref[...]
        a01 = jnp.dot(x_ref[...], w1, preferred_element_type=f32)
        c = jnp.dot(_bf(pe_ref[...]), w1, preferred_element_type=f32)
        a0 = a01[:, :NB_DH] + c[0:1, :NB_DH]
        a1 = a01[:, NB_DH:] + c[1:2, NB_DH:]
        pre = a0 + pltpu.roll(a1, NB_KV * n_sub - 1, 0)
        y = jnp.dot(_bf(jax.nn.gelu(pre)), w2_ref[...], preferred_element_type=f32)
        if norm:
            y = y * lax.rsqrt(jnp.mean(y * y, axis=-1, keepdims=True) + EPS) * gain_ref[...]
        o_ref[...] = jnp.where(last, 0.0, y).astype(o_ref.dtype)


def _cmp_pages(page_table, cache_k, cache_v, pe_k, pe_v, w1_k, w1_v, w2_k, w2_v, gain):
    nb, n_pages = page_table.shape
    n_phys = cache_k.shape[0]
    assert cache_k.shape[1:] == (PAGE, NB_KV, NB_DH) and CMP_LEN == 2 * CMP_STRIDE
    n_sub = n_pages * SUBS_PER_PAGE
    kd = CMP_STRIDE * NB_DH

    def w1cat(w1):
        return _bf(jnp.concatenate([w1[:CMP_STRIDE].reshape(kd, NB_DH), w1[CMP_STRIDE:].reshape(kd, NB_DH)], axis=1))

    def pecat(pe):
        z = jnp.zeros((8, kd), jnp.float32)
        return z.at[0].set(pe[:CMP_STRIDE].reshape(kd)).at[1].set(pe[CMP_STRIDE:].reshape(kd))

    def full(*shape):
        return pl.BlockSpec(shape, lambda b, pt: (0,) * len(shape))

    out_spec = pl.BlockSpec((None, NB_KV * n_sub, NB_DH), lambda b, pt: (b, 0, 0))
    ck = cache_k.reshape(n_phys * PAGE * NB_KV, NB_DH)
    cv = cache_v.reshape(n_phys * PAGE * NB_KV, NB_DH)
    grid_spec = pltpu.PrefetchScalarGridSpec(
        num_scalar_prefetch=1,
        grid=(nb,),
        in_specs=_page_specs(n_pages) * 2 + [full(kd, 2 * NB_DH), full(kd, 2 * NB_DH), full(8, kd), full(8, kd),
                                            full(NB_DH, NB_DH), full(NB_DH, NB_DH), full(1, NB_DH)],
        out_specs=[out_spec, out_spec],
        scratch_shapes=[pltpu.VMEM((NB_KV * n_sub, kd), jnp.bfloat16),
                        pltpu.VMEM((n_sub * SUB_PITCH, NB_DH), jnp.float32)],
    )
    return pl.pallas_call(
        functools.partial(_cmp_pages_kernel, n_pages=n_pages),
        grid_spec=grid_spec,
        out_shape=[jax.ShapeDtypeStruct((nb, NB_KV * n_sub, NB_DH), jnp.bfloat16)] * 2,
        compiler_params=pltpu.CompilerParams(dimension_semantics=("arbitrary",), vmem_limit_bytes=VMEM_LIMIT),
        name="cmp_pages",
    )(page_table, *([ck] * n_pages), *([cv] * n_pages), w1cat(w1_k), w1cat(w1_v), pecat(pe_k), pecat(pe_v),
      _bf(w2_k), _bf(w2_v), gain.reshape(1, NB_DH))


def _softmax_with_tail(s, mask, s_tail, ok_tail):
    m = jnp.max(s, axis=-1, keepdims=True)
    for sc in s_tail:
        m = jnp.maximum(m, sc)
    e = jnp.where(mask, jnp.exp(s - m), 0.0)
    e_tail = [jnp.where(ok, jnp.exp(sc - m), 0.0) for ok, sc in zip(ok_tail, s_tail)]
    den = jnp.sum(e, axis=-1, keepdims=True)
    for et in e_tail:
        den = den + et
    inv = 1.0 / jnp.maximum(den, 1e-30)
    return e * inv, [et * inv for et in e_tail]


def _nsa_sample_kernel(pt_ref, *refs, n_pages, n_new, win_keep):
    ksp = refs[:n_pages]
    vsp = refs[n_pages:2 * n_pages]
    (q_ref, qr_ref, kcb_ref, vcb_ref, ksn_ref, vsn_ref, kwc_ref, vwc_ref, kwn_ref, vwn_ref, ov_ref,
     oc_ref, os_ref, ow_ref, kwo_ref, vwo_ref) = refs[2 * n_pages:]
    f32 = jnp.float32
    past = n_pages * PAGE
    n_sub = past // CMP_STRIDE
    n_cmp = n_sub - CMP_LEN // CMP_STRIDE + 1
    n_slc = -(-(past + n_new) // SLC_LEN)
    new_blk = past // SLC_LEN
    assert (past % SLC_LEN) + n_new <= SLC_LEN and n_sub == LANES and n_slc <= LANES
    rows = HEADS_PER_GROUP * n_new
    all_rows = NB_KV * rows
    sel_rows = NB_KV * n_new
    scale = NB_DH ** -0.5
    groups = range(NB_KV)

    def grp_rows(g, n_tok):
        return pl.ds(g, n_tok, stride=NB_KV)

    def stack(parts):
        return jnp.concatenate(parts, axis=0)

    t_row = lax.rem(lax.broadcasted_iota(jnp.int32, (all_rows, 1), 0), n_new)
    pos = past + t_row
    pos_t = past + lax.rem(lax.broadcasted_iota(jnp.int32, (sel_rows, 1), 0), n_new)
    lane = lax.broadcasted_iota(jnp.int32, (1, LANES), 1)
    q = [q_ref[g] for g in groups]
    qr = [qr_ref[g] for g in groups]
    qrf = stack(qr).astype(f32)

    def tail_scores(new_ref):
        kb = _bf(new_ref[...]).astype(f32)
        out = []
        for i in range(n_new):
            ki = stack([jnp.broadcast_to(kb[i * NB_KV + g:i * NB_KV + g + 1, :], (rows, NB_DH)) for g in groups])
            out.append(jnp.sum(qrf * ki, axis=-1, keepdims=True) * scale)
        return out

    def tail_values(p_tail, new_ref, g):
        vb = _bf(new_ref[...]).astype(f32)
        acc = _bf(p_tail[0][g * rows:(g + 1) * rows]).astype(f32) * vb[g:g + 1, :]
        for i in range(1, n_new):
            acc = acc + _bf(p_tail[i][g * rows:(g + 1) * rows]).astype(f32) * vb[i * NB_KV + g:i * NB_KV + g + 1, :]
        return acc

    mask_c = (lane * CMP_STRIDE + (CMP_LEN - 1) <= pos) & (lane < n_cmp)
    s = stack([lax.dot_general(q[g], kcb_ref[g * n_sub:(g + 1) * n_sub, :], _NT, preferred_element_type=f32)
               for g in groups]) * scale
    s = jnp.where(mask_c, s, NEG)
    m = jnp.max(s, axis=-1, keepdims=True)
    e = jnp.where(mask_c, jnp.exp(s - m), 0.0)
    p = e / jnp.maximum(jnp.sum(e, axis=-1, keepdims=True), 1e-30)
    pb = _bf(p)
    for g in groups:
        oc_ref[g] = jnp.dot(pb[g * rows:(g + 1) * rows], vcb_ref[g * n_sub:(g + 1) * n_sub, :],
                            preferred_element_type=f32)
    psum = []
    for g in groups:
        acc = p[g * rows:g * rows + n_new]
        for j in range(1, HEADS_PER_GROUP):
            acc = acc + p[g * rows + j * n_new:g * rows + (j + 1) * n_new]
        psum.append(acc)
    psum = stack(psum)
    hi = _bf(psum)
    lo = _bf(psum - hi.astype(f32))
    ov = ov_ref[...]
    imp = jnp.dot(hi, ov, preferred_element_type=f32) + jnp.dot(lo, ov, preferred_element_type=f32)
    cur = lax.shift_right_logical(pos_t, 6)
    forced = (lane == 0) | (lane == cur) | (lane == cur - 1)
    future = lane * SLC_LEN > pos_t
    score = jnp.where(future, NEG, jnp.where(forced, BIG, imp))
    score = jnp.where(lane >= n_slc, REMOVED, score)
    lane_f = lax.broadcasted_iota(jnp.int32, (sel_rows, LANES), 1).astype(f32)
    sel = jnp.zeros((sel_rows, LANES), f32)
    for _ in range(min(N_SELECT, n_slc)):
        mx = jnp.max(score, axis=-1, keepdims=True)
        idx = jnp.min(jnp.where(score == mx, lane_f, float(LANES)), axis=-1, keepdims=True)
        hit = lane_f == idx
        sel = jnp.where(hit, 1.0, sel)
        score = jnp.where(hit, REMOVED, score)
    sel_r = stack([sel[g * n_new:(g + 1) * n_new] for g in groups for _ in range(HEADS_PER_GROUP)])
    expand = jnp.where(lax.broadcasted_iota(jnp.int32, (LANES, past), 0)
                       == lax.shift_right_logical(lax.broadcasted_iota(jnp.int32, (LANES, past), 1), 6),
                       1.0, 0.0).astype(jnp.bfloat16)
    picked = jnp.dot(_bf(sel_r), expand, preferred_element_type=f32) > 0.5
    s_all = stack([jnp.concatenate([lax.dot_general(qr[g], _bf(ksp[pg][grp_rows(g, PAGE), :]), _NT,
                                                    preferred_element_type=f32)
                                    for pg in range(n_pages)], axis=1) for g in groups]) * scale
    s_all = jnp.where(picked, s_all, NEG)
    new_sel = sel_r[:, new_blk:new_blk + 1] > 0.5
    ok_new = [new_sel & (t_row >= i) for i in range(n_new)]
    s_new = [jnp.where(ok, sc, NEG) for ok, sc in zip(ok_new, tail_scores(ksn_ref))]
    p_all, p_new = _softmax_with_tail(s_all, picked, s_new, ok_new)
    p_all = _bf(p_all)
    for g in groups:
        acc = tail_values(p_new, vsn_ref, g)
        for pg in range(n_pages):
            acc = acc + jnp.dot(p_all[g * rows:(g + 1) * rows, pg * PAGE:(pg + 1) * PAGE],
                                _bf(vsp[pg][grp_rows(g, PAGE), :]), preferred_element_type=f32)
        os_ref[g] = acc
    mask_w = lax.broadcasted_iota(jnp.int32, (1, win_keep), 1) >= t_row
    s_w = stack([lax.dot_general(qr[g], _bf(kwc_ref[grp_rows(g, win_keep), :]), _NT, preferred_element_type=f32)
                 for g in groups]) * scale
    s_w = jnp.where(mask_w, s_w, NEG)
    ok_wn = [t_row >= i for i in range(n_new)]
    s_wn = [jnp.where(ok, sc, NEG) for ok, sc in zip(ok_wn, tail_scores(kwn_ref))]
    p_w, p_wn = _softmax_with_tail(s_w, mask_w, s_wn, ok_wn)
    p_w = _bf(p_w)
    for g in groups:
        ow_ref[g] = (tail_values(p_wn, vwn_ref, g)
                     + jnp.dot(p_w[g * rows:(g + 1) * rows], _bf(vwc_ref[grp_rows(g, win_keep), :]),
                               preferred_element_type=f32))
    new_rows = n_new * NB_KV
    kept_rows = win_keep * NB_KV - new_rows
    for src_ref, add_ref, dst_ref in ((kwc_ref, kwn_ref, kwo_ref), (vwc_ref, vwn_ref, vwo_ref)):
        dst_ref[0:kept_rows, :] = src_ref[new_rows:new_rows + kept_rows, :]
        dst_ref[kept_rows:kept_rows + new_rows, :] = add_ref[...]


def _nsa_sample(page_table, q, qr, kcb, vcb, cache_sk, cache_sv, ks_new, vs_new, cache_wk, cache_wv, kw_new, vw_new):
    nb, n_pages = page_table.shape
    t = q.shape[1]
    n_phys = cache_sk.shape[0]
    keep = cache_wk.shape[1]
    assert keep == WINDOW and (t * NB_KV) % 8 == 0 and t < keep
    rows = HEADS_PER_GROUP * t
    n_sub = n_pages * SUBS_PER_PAGE

    def q_rows(a):
        a = a.reshape(nb, t, NB_KV, HEADS_PER_GROUP, NB_DH)
        return _bf(jnp.transpose(a, (0, 2, 3, 1, 4)).reshape(nb, NB_KV, rows, NB_DH))

    def q_unrows(a):
        a = a.reshape(nb, NB_KV, HEADS_PER_GROUP, t, NB_DH)
        return jnp.transpose(a, (0, 3, 1, 2, 4)).reshape(nb, t, NB_HEADS, NB_DH)

    def per_b(*shape):
        return pl.BlockSpec((None,) + shape, lambda b, pt: (b,) + (0,) * len(shape))

    def rows_view(a):
        return a.reshape(a.shape[0], a.shape[1] * NB_KV, NB_DH)

    n_cmp = n_sub - CMP_LEN // CMP_STRIDE + 1
    q_spec = per_b(NB_KV, rows, NB_DH)
    win_spec = per_b(keep * NB_KV, NB_DH)
    new_spec = per_b(t * NB_KV, NB_DH)
    cmp_spec = per_b(NB_KV * n_sub, NB_DH)
    in_specs = (_page_specs(n_pages) * 2
                + [q_spec, q_spec, cmp_spec, cmp_spec, new_spec, new_spec, win_spec, win_spec, new_spec, new_spec,
                   pl.BlockSpec((LANES, LANES), lambda b, pt: (0, 0))])
    grid_spec = pltpu.PrefetchScalarGridSpec(
        num_scalar_prefetch=1, grid=(nb,), in_specs=in_specs, out_specs=[q_spec] * 3 + [win_spec] * 2)
    sk = cache_sk.reshape(n_phys * PAGE * NB_KV, NB_DH)
    sv = cache_sv.reshape(n_phys * PAGE * NB_KV, NB_DH)
    o_c, o_s, o_w, wk_next, wv_next = pl.pallas_call(
        functools.partial(_nsa_sample_kernel, n_pages=n_pages, n_new=t, win_keep=keep),
        grid_spec=grid_spec,
        out_shape=([jax.ShapeDtypeStruct((nb, NB_KV, rows, NB_DH), jnp.float32)] * 3
                   + [jax.ShapeDtypeStruct((nb, keep * NB_KV, NB_DH), jnp.float32)] * 2),
        compiler_params=pltpu.CompilerParams(dimension_semantics=("arbitrary",), vmem_limit_bytes=VMEM_LIMIT),
        name="nsa_sample",
    )(page_table, *([sk] * n_pages), *([sv] * n_pages), q_rows(q), q_rows(qr), kcb, vcb,
      rows_view(ks_new), rows_view(vs_new), rows_view(cache_wk), rows_view(cache_wv), rows_view(kw_new),
      rows_view(vw_new), _overlap_matrix(LANES, n_cmp, LANES))
    return (q_unrows(o_c), q_unrows(o_s), q_unrows(o_w),
            wk_next.reshape(cache_wk.shape), wv_next.reshape(cache_wv.shape))


def _mm_kernel(x_ref, w_ref, o_ref, wb_ref):
    @pl.when(pl.program_id(1) == 0)
    def _():
        wb_ref[...] = w_ref[...].astype(jnp.bfloat16)

    o_ref[...] = jnp.dot(x_ref[...], wb_ref[...], preferred_element_type=jnp.float32)


def _mm_tiles(m, k, n):
    if k <= 2048:
        tm = 1024
        tn = 1024 if (n % 1024 == 0 or n >= 4096) else 512
    else:
        tm, tn = 512, 512
    return min(tm, m), min(tn, n)


def _mm(x, w):
    lead = x.shape[:-1]
    k = x.shape[-1]
    n = w.shape[1]
    x2 = _bf(x.reshape(-1, k))
    m = x2.shape[0]
    tm, tn = _mm_tiles(m, k, n)
    assert m % tm == 0
    out = pl.pallas_call(
        _mm_kernel,
        grid=(pl.cdiv(n, tn), m // tm),
        in_specs=[pl.BlockSpec((tm, k), lambda j, i: (i, 0)),
                  pl.BlockSpec((k, tn), lambda j, i: (0, j))],
        out_specs=pl.BlockSpec((tm, tn), lambda j, i: (i, j)),
        out_shape=jax.ShapeDtypeStruct((m, n), jnp.float32),
        scratch_shapes=[pltpu.VMEM((k, tn), jnp.bfloat16)],
        compiler_params=pltpu.CompilerParams(
            dimension_semantics=("arbitrary", "arbitrary"), vmem_limit_bytes=VMEM_LIMIT),
        name="mm",
    )(x2, w)
    return out.reshape(*lead, n)


def _mm_res_kernel(x_ref, w_ref, r_ref, o_ref, wb_ref):
    @pl.when(pl.program_id(1) == 0)
    def _():
        wb_ref[...] = _bf(w_ref[...])

    o_ref[...] = r_ref[...] + jnp.dot(x_ref[...], wb_ref[...], preferred_element_type=jnp.float32)


def _mm_res(x, w, resid):
    lead = x.shape[:-1]
    k = x.shape[-1]
    n = w.shape[1]
    x2 = _bf(x.reshape(-1, k))
    r2 = resid.reshape(-1, n)
    m = x2.shape[0]
    tm, tn = _mm_tiles(m, k, n)
    assert m % tm == 0 and r2.shape == (m, n)
    tile = pl.BlockSpec((tm, tn), lambda j, i: (i, j))
    out = pl.pallas_call(
        _mm_res_kernel,
        grid=(pl.cdiv(n, tn), m // tm),
        in_specs=[pl.BlockSpec((tm, k), lambda j, i: (i, 0)), pl.BlockSpec((k, tn), lambda j, i: (0, j)), tile],
        out_specs=tile,
        out_shape=jax.ShapeDtypeStruct((m, n), jnp.float32),
        scratch_shapes=[pltpu.VMEM((k, tn), jnp.bfloat16)],
        compiler_params=pltpu.CompilerParams(
            dimension_semantics=("arbitrary", "arbitrary"), vmem_limit_bytes=VMEM_LIMIT),
        name="mm_res",
    )(x2, w, r2)
    return out.reshape(*lead, n)


def _ple_kernel(hn_ref, wg_ref, p_ref, wp_ref, h_ref, o_ref, wgb_ref, wpb_ref):
    @pl.when(pl.program_id(1) == 0)
    def _():
        wgb_ref[...] = _bf(wg_ref[...])
        wpb_ref[...] = _bf(wp_ref[...])

    gate = jax.nn.sigmoid(jnp.dot(hn_ref[...], wgb_ref[...], preferred_element_type=jnp.float32))
    o_ref[...] = h_ref[...] + jnp.dot(p_ref[...], wpb_ref[...], preferred_element_type=jnp.float32) * gate


def _ple_fused(h, hn, p, w_p, w_g):
    lead = h.shape[:-1]
    d = h.shape[-1]
    kp = p.shape[-1]
    h2 = h.reshape(-1, d)
    m = h2.shape[0]
    tm, tn = min(m, PLE_TM), PLE_TN
    assert m % tm == 0 and d % tn == 0 and w_g.shape == (d, d) and w_p.shape == (kp, d)
    tile = pl.BlockSpec((tm, tn), lambda j, i: (i, j))
    out = pl.pallas_call(
        _ple_kernel,
        grid=(d // tn, m // tm),
        in_specs=[pl.BlockSpec((tm, d), lambda j, i: (i, 0)), pl.BlockSpec((d, tn), lambda j, i: (0, j)),
                  pl.BlockSpec((tm, kp), lambda j, i: (i, 0)), pl.BlockSpec((kp, tn), lambda j, i: (0, j)), tile],
        out_specs=tile,
        out_shape=jax.ShapeDtypeStruct((m, d), jnp.float32),
        scratch_shapes=[pltpu.VMEM((d, tn), jnp.bfloat16), pltpu.VMEM((kp, tn), jnp.bfloat16)],
        compiler_params=pltpu.CompilerParams(
            dimension_semantics=("arbitrary", "arbitrary"), vmem_limit_bytes=VMEM_LIMIT),
        name="ple",
    )(_bf(hn.reshape(-1, d)), w_g, _bf(p.reshape(-1, kp)), w_p, h2)
    return out.reshape(*lead, d)


def _ffn_up_kernel(x_ref, wu_ref, wg_ref, cw_ref, cb_ref, prev_ref, act_ref, state_ref, wub_ref, wgb_ref, halo_ref):
    m = pl.program_id(1)
    tm = x_ref.shape[0]
    f32 = jnp.float32

    @pl.when(m == 0)
    def _():
        wub_ref[...] = _bf(wu_ref[...])
        wgb_ref[...] = _bf(wg_ref[...])
        halo_ref[0:CONV_W - 1, :] = prev_ref[...].astype(f32)

    x = x_ref[...]
    u = jnp.dot(x, wub_ref[...], preferred_element_type=f32)
    g = jnp.dot(x, wgb_ref[...], preferred_element_type=f32)
    row = lax.broadcasted_iota(jnp.int32, (tm, 1), 0)
    before1 = halo_ref[1:2, :]
    before2 = halo_ref[0:1, :]
    g1 = jnp.where(row == 0, before1, pltpu.roll(g, 1, 0))
    g2 = pltpu.roll(g, 2, 0)
    g2 = jnp.where(row == 0, before2, jnp.where(row == 1, before1, g2))
    conv = cb_ref[...] + cw_ref[0:1, :] * g2 + cw_ref[1:2, :] * g1 + cw_ref[2:3, :] * g
    act_ref[...] = (jax.nn.silu(conv) * u).astype(act_ref.dtype)
    tail = g[tm - (CONV_W - 1):tm, :]
    halo_ref[0:CONV_W - 1, :] = tail
    state_ref[...] = tail


def _ffn_up_prompt(x, w_up, conv_w, conv_b, prev):
    m, k = x.shape
    assert CONV_W == 3 and m % FFN_TM == 0 and w_up.shape == (k, 2 * D_FF)
    w_u, w_g = w_up[:, :D_FF], w_up[:, D_FF:]

    def col(r):
        return pl.BlockSpec((r, FFN_TN), lambda j, i: (0, j))

    return pl.pallas_call(
        _ffn_up_kernel,
        grid=(pl.cdiv(D_FF, FFN_TN), m // FFN_TM),
        in_specs=[pl.BlockSpec((FFN_TM, k), lambda j, i: (i, 0)), col(k), col(k), col(CONV_W), col(1), col(CONV_W - 1)],
        out_specs=[pl.BlockSpec((FFN_TM, FFN_TN), lambda j, i: (i, j)), col(CONV_W - 1)],
        out_shape=[jax.ShapeDtypeStruct((m, D_FF), jnp.bfloat16),
                   jax.ShapeDtypeStruct((CONV_W - 1, D_FF), jnp.float32)],
        scratch_shapes=[pltpu.VMEM((k, FFN_TN), jnp.bfloat16), pltpu.VMEM((k, FFN_TN), jnp.bfloat16),
                        pltpu.VMEM((8, FFN_TN), jnp.float32)],
        compiler_params=pltpu.CompilerParams(
            dimension_semantics=("arbitrary", "arbitrary"), vmem_limit_bytes=VMEM_LIMIT),
        name="ffn_up_prompt",
    )(_bf(x), w_u, w_g, conv_w, conv_b.reshape(1, D_FF), prev)


def _rmsnorm(x, g):
    xf = x.astype(jnp.float32)
    y = xf * lax.rsqrt(jnp.mean(xf * xf, axis=-1, keepdims=True) + EPS)
    return (y * g.astype(jnp.float32)).astype(x.dtype)


def _rope(x, pos):
    half = x.shape[-1] // 2
    inv = ROPE_THETA ** (-jnp.arange(half, dtype=jnp.float32) / half)
    ang = pos.astype(jnp.float32)[:, None] * inv[None, :]
    cos, sin = jnp.cos(ang)[:, None, :], jnp.sin(ang)[:, None, :]
    xf = x.astype(jnp.float32)
    x1, x2 = xf[..., :half], xf[..., half:]
    return jnp.concatenate([x1 * cos - x2 * sin, x2 * cos + x1 * sin], axis=-1).astype(x.dtype)


def _gla_chunked(q, k, v, logf, s0):
    b, t, h, dk = q.shape
    dv = v.shape[-1]
    c = math.gcd(t, HA_CHUNK)
    n = t // c

    def to_chunks(a):
        return jnp.transpose(a.reshape(b, n, c, h, a.shape[-1]), (1, 0, 3, 2, 4))

    qc, kc, vc, gc = to_chunks(q), to_chunks(k), to_chunks(v), to_chunks(logf)
    bcum = jnp.cumsum(gc, axis=3)
    blast = bcum[:, :, :, -1, :]
    q_in = qc * jnp.exp(bcum)
    k_in = kc * jnp.exp(-bcum)
    k_dec = kc * jnp.exp(blast[:, :, :, None, :] - bcum)
    causal = jnp.tril(jnp.ones((c, c), dtype=bool))
    att = jnp.where(causal, jnp.einsum('nbhik,nbhjk->nbhij', q_in, k_in), 0.0)
    o_intra = jnp.einsum('nbhij,nbhjv->nbhiv', att, vc)

    def step(s, inp):
        qi, kd, vi, dl = inp
        o_inter = jnp.einsum('bhik,bhkv->bhiv', qi, s)
        s = jnp.exp(dl)[..., None] * s + jnp.einsum('bhik,bhiv->bhkv', kd, vi)
        return s, o_inter

    s_final, o_inter = lax.scan(step, s0.astype(jnp.float32), (q_in, k_dec, vc, blast))
    o = o_intra + o_inter
    return jnp.transpose(o, (1, 0, 3, 2, 4)).reshape(b, t, h, dv), s_final


def _hgrn2_layer(xn, s0, lb, w_in, out_gain, w_out, resid):
    b, t, _ = xn.shape
    hk = HA_HEADS * HA_DK
    hv = HA_HEADS * HA_DV
    proj = _mm(xn, w_in)
    if s0 is None:
        assert b == 1
        o, s_new = _gla_prompt(proj[0], lb, out_gain)
        return _mm_res(o[None], w_out, resid), s_new[None]
    q = jax.nn.silu(proj[..., :hk])
    f = lb + (1.0 - lb) * jax.nn.sigmoid(proj[..., hk:2 * hk])
    i_in = proj[..., 2 * hk:2 * hk + hv]
    gate = proj[..., 2 * hk + hv:]
    shp_k = (b, t, HA_HEADS, HA_DK)
    shp_v = (b, t, HA_HEADS, HA_DV)
    o, s_new = _gla_chunked(q.reshape(shp_k), (1.0 - f).reshape(shp_k), i_in.reshape(shp_v),
                            jnp.log(f).reshape(shp_k), s0)
    o = _rmsnorm(o, out_gain.reshape(HA_HEADS, HA_DV)) * jax.nn.silu(gate).reshape(shp_v)
    return _mm_res(o.reshape(b, t, hv), w_out, resid), s_new


def _conv_ffn(xn, prev, w_up, conv_w, conv_b, w_down, resid):
    b, t, _ = xn.shape
    if b == 1 and t % FFN_TM == 0:
        act, state = _ffn_up_prompt(xn[0], w_up, conv_w, conv_b, prev[0])
        return _mm_res(act[None], w_down, resid), state[None].astype(xn.dtype)
    up = _mm(xn, w_up)
    u, gt = up[..., :D_FF], up[..., D_FF:]
    ext = jnp.concatenate([prev.astype(gt.dtype), gt], axis=1)
    conv = conv_b
    for j in range(CONV_W):
        conv = conv + conv_w[j] * ext[:, j:j + t]
    return _mm_res(jax.nn.silu(conv) * u, w_down, resid), ext[:, t:]


def _ple(h, p, gain, w_p, w_g):
    return _ple_fused(h, _rmsnorm(h, gain), p, w_p, w_g)


def _compress(rows, pe, w1, w2):
    b, l, g, d = rows.shape
    n_sub = l // CMP_STRIDE
    r = CMP_LEN // CMP_STRIDE
    n_cmp = n_sub - r + 1
    sub = rows[:, :n_sub * CMP_STRIDE].reshape(b, n_sub, CMP_STRIDE, g, d)
    pre = None
    for j in range(r):
        sl = slice(j * CMP_STRIDE, (j + 1) * CMP_STRIDE)
        part = jnp.einsum('bnsgd,sde->bnge', sub + pe[sl][None, None, :, None, :], w1[sl])[:, j:j + n_cmp]
        pre = part if pre is None else pre + part
    return jnp.einsum('bnge,ef->bngf', jax.nn.gelu(pre), w2)


def _nsa_layer(xn, pos, shared, w_in, q_gain, w_out, banded, resid):
    b, t, _ = xn.shape
    hd = NB_HEADS * NB_DH
    q = _rmsnorm(_mm(xn, w_in[:, :hd]).reshape(b, t, NB_HEADS, NB_DH), q_gain)
    gates = jax.nn.sigmoid(_mm(xn, w_in[:, hd:])).reshape(b, t, 3, NB_HEADS)
    qr = _rope(q, pos)
    if banded:
        assert b == 1
        kcb, vcb, kpad, vpad, win_k, win_v = shared
        oc_gated, sel = _cmp_topk_prompt(q.reshape(t, hd), kcb[0], vcb[0], gates[0, :, 0, :])
        o = _slc_win_prompt(qr.reshape(t, hd), kpad[0], vpad[0], win_k[0], win_v[0], sel, oc_gated,
                            gates[0, :, 1:3, :])
        return _mm_res(o[None], w_out, resid), None
    o_c, o_s, o_w, wk_next, wv_next = _nsa_sample(shared[0], q, qr, *shared[1:])
    o = (gates[:, :, 0, :, None] * o_c + gates[:, :, 1, :, None] * o_s + gates[:, :, 2, :, None] * o_w)
    return _mm_res(o.reshape(b, t, hd), w_out, resid), (wk_next, wv_next)


def _shared_kv(h, pos, past, wt):
    b, t, _ = h.shape
    hn = _bf(_rmsnorm(h, wt['kv_norm']))
    gw = NB_KV * NB_DH
    k_cmp, v_cmp, k_slc, v_slc, k_win, v_win = [
        _mm(hn, wt['kv_w'][:, c * gw:(c + 1) * gw]).reshape(b, t, NB_KV, NB_DH) for c in range(6)]
    k_slc = _rope(_rmsnorm(k_slc, wt['k_norm_slc']), pos)
    k_win = _rope(_rmsnorm(k_win, wt['k_norm_win']), pos)
    if past is None:
        keep = min(WINDOW, t)
        kcb = _rmsnorm(_compress(k_cmp, wt['cmp_pe_k'], wt['cmp_w1_k'], wt['cmp_w2_k']), wt['k_norm_cmp'])
        vcb = _compress(v_cmp, wt['cmp_pe_v'], wt['cmp_w1_v'], wt['cmp_w2_v'])
        assert t % SLC_LEN == 0
        shared = (kcb, vcb, k_slc, v_slc, k_win, v_win)
        rows = (k_cmp, v_cmp, k_slc, v_slc, k_win[:, -keep:], v_win[:, -keep:])
    else:
        page_table, ckc, cvc, cks, cvs, cwk, cwv = past
        assert t < CMP_STRIDE and (page_table.shape[1] * PAGE) % CMP_STRIDE == 0
        kcb, vcb = _cmp_pages(page_table, ckc, cvc, wt['cmp_pe_k'], wt['cmp_pe_v'], wt['cmp_w1_k'], wt['cmp_w1_v'],
                              wt['cmp_w2_k'], wt['cmp_w2_v'], wt['k_norm_cmp'])
        shared = (page_table, kcb, vcb, cks, cvs, k_slc, v_slc, cwk, cwv, k_win, v_win)
        rows = (k_cmp, v_cmp, k_slc, v_slc, None, None)
    return shared, rows


def _trunk(x, p, start, s_hgrn, s_conv, past, wt):
    b, t, _ = x.shape
    pos = start + jnp.arange(t)
    banded = past is None
    lb = jnp.cumsum(jax.nn.softmax(wt['a_lb_logits'].astype(jnp.float32), axis=0), axis=0)
    h = x
    hgrn_out, conv_out = [], []
    shared, rows = None, None
    for i in range(DEPTH):
        hn = _rmsnorm(h, wt['norm_mix'][i])
        if i < N_A:
            h, s = _hgrn2_layer(hn, None if s_hgrn is None else s_hgrn[i], lb[i], wt['a_w_in'][i],
                                wt['a_out_norm'][i], wt['a_w_out'][i], h)
            hgrn_out.append(s)
        else:
            j = i - N_A
            h, win_next = _nsa_layer(hn, pos, shared, wt['b_w_in'][j], wt['b_q_norm'][j], wt['b_w_out'][j],
                                     banded, h)
            if win_next is not None:
                assert N_B == 1
                rows = rows[:4] + win_next
        h, cs = _conv_ffn(_rmsnorm(h, wt['norm_ffn'][i]), s_conv[i], wt['ffn_w_up'][i], wt['ffn_conv_w'][i],
                          wt['ffn_conv_b'][i], wt['ffn_w_down'][i], h)
        conv_out.append(cs)
        h = _ple(h, p[i], wt['norm_ple'][i], wt['ple_w_p'][i], wt['ple_w_g'][i])
        if i == N_A - 1:
            shared, rows = _shared_kv(h, pos, past, wt)
    return h, jnp.stack(hgrn_out), jnp.stack(conv_out), rows


def kernel(x_prompt, x_sample, state_hgrn, state_conv, cache_cmp_k, cache_cmp_v, cache_slc_k, cache_slc_v,
           cache_win_k, cache_win_v, page_table, p_prompt, p_sample, norm_mix, norm_ffn, norm_ple,
           a_w_in, a_lb_logits, a_out_norm, a_w_out, kv_norm, kv_w, k_norm_cmp, k_norm_slc, k_norm_win,
           cmp_pe_k, cmp_pe_v, cmp_w1_k, cmp_w2_k, cmp_w1_v, cmp_w2_v, b_w_in, b_q_norm, b_w_out,
           ffn_w_up, ffn_conv_w, ffn_conv_b, ffn_w_down, ple_w_p, ple_w_g):
    wt = {
        'norm_mix': norm_mix, 'norm_ffn': norm_ffn, 'norm_ple': norm_ple,
        'a_w_in': a_w_in, 'a_lb_logits': a_lb_logits, 'a_out_norm': a_out_norm, 'a_w_out': a_w_out,
        'kv_norm': kv_norm, 'kv_w': kv_w, 'k_norm_cmp': k_norm_cmp, 'k_norm_slc': k_norm_slc,
        'k_norm_win': k_norm_win, 'cmp_pe_k': cmp_pe_k, 'cmp_pe_v': cmp_pe_v, 'cmp_w1_k': cmp_w1_k,
        'cmp_w2_k': cmp_w2_k, 'cmp_w1_v': cmp_w1_v, 'cmp_w2_v': cmp_w2_v,
        'b_w_in': b_w_in, 'b_q_norm': b_q_norm, 'b_w_out': b_w_out,
        'ffn_w_up': ffn_w_up, 'ffn_conv_w': ffn_conv_w, 'ffn_conv_b': ffn_conv_b, 'ffn_w_down': ffn_w_down,
        'ple_w_p': ple_w_p, 'ple_w_g': ple_w_g,
    }
    bp, past_len = x_prompt.shape[0], page_table.shape[1] * cache_cmp_k.shape[1]
    s_conv0 = jnp.zeros((DEPTH, bp, CONV_W - 1, D_FF), x_prompt.dtype)
    y_prompt, hgrn_p, conv_p, rows_p = _trunk(x_prompt, p_prompt, 0, None, s_conv0, None, wt)
    past = (page_table, cache_cmp_k, cache_cmp_v, cache_slc_k, cache_slc_v, cache_win_k, cache_win_v)
    y_sample, hgrn_s, conv_s, rows_s = _trunk(x_sample, p_sample, past_len, state_hgrn, state_conv, past, wt)
    kc_p, vc_p, ks_p, vs_p, wk_p, wv_p = rows_p
    kc_s, vc_s, ks_s, vs_s, wk_s, wv_s = rows_s
    return (y_prompt, y_sample, hgrn_p, hgrn_s, conv_p, conv_s, kc_p, vc_p, ks_p, vs_p,
            kc_s, vc_s, ks_s, vs_s, wk_p, wv_p, wk_s, wv_s)
```
